```python
import jax
import jax.numpy as jnp
from jax import lax
import numpy as np

D_MODEL = 4096
BATCH = 2
SEQ = 4096
DEPTH = 1
DEC_BATCH = 32
DEC_SEQ = 4
PAST_LEN = 8192
PAGE_SIZE = 128

HEAD_DIM = 128
FOX_HEADS = 16
FOX_KV_HEADS = 4
FOX_GROUP = FOX_HEADS // FOX_KV_HEADS
NSA_HEADS = 16
NSA_KV_HEADS = 2
NSA_GROUP = NSA_HEADS // NSA_KV_HEADS
NSA_BLOCK = 64
NSA_TOP_N = 16
NSA_WINDOW = 512
NSA_BRANCHES = 3
NSA_N_KV = 4
SEL_FORCE = 1e4
ROPE_THETA = 10000.0
Q_BLOCK = 128
PLE_DIM = 256
PEER_HEADS = 8
PEER_KEYS = 128
PEER_EXPERTS = PEER_KEYS * PEER_KEYS
PEER_TOPK = 16
PEER_QDIM = 256
PEER_HALF = PEER_QDIM // 2
RMS_EPS = 1e-6
FORGET_BIAS_INIT = 5.0

FOX_W = FOX_HEADS * HEAD_DIM
FOX_KV_W = FOX_KV_HEADS * HEAD_DIM
NSA_W = NSA_HEADS * HEAD_DIM
NSA_KV_W = NSA_KV_HEADS * HEAD_DIM
IN_SIZES = (FOX_W, FOX_KV_W, FOX_KV_W, FOX_HEADS, NSA_W, NSA_KV_W, NSA_KV_W, NSA_KV_W, NSA_KV_W, NSA_KV_W, NSA_KV_W, NSA_HEADS * NSA_BRANCHES)
D_IN = sum(IN_SIZES)
IN_SPLIT_IDX = [int(v) for v in np.cumsum(IN_SIZES)[:-1]]

kernel_name = "fox_nsa_peer_hybrid_step"


def rmsnorm(x, g):
    xf = x.astype(jnp.float32)
    y = xf * lax.rsqrt(jnp.mean(xf * xf, axis=-1, keepdims=True) + RMS_EPS)
    return (y * g.astype(jnp.float32)).astype(x.dtype)


def rope(x, pos):
    half = HEAD_DIM // 2
    inv = ROPE_THETA ** (-jnp.arange(half, dtype=jnp.float32) / half)
    ang = pos.astype(jnp.float32)[:, None] * inv[None, :]
    cos = jnp.cos(ang)[:, None, :]
    sin = jnp.sin(ang)[:, None, :]
    xf = x.astype(jnp.float32)
    x1, x2 = xf[..., :half], xf[..., half:]
    return jnp.concatenate([x1 * cos - x2 * sin, x2 * cos + x1 * sin], axis=-1).astype(x.dtype)


def masked_softmax(s, mask):
    s = jnp.where(mask, s.astype(jnp.float32), -jnp.inf)
    m = jnp.max(s, axis=-1, keepdims=True)
    m = jnp.where(jnp.isfinite(m), m, 0.0)
    e = jnp.exp(s - m)
    return e / jnp.maximum(jnp.sum(e, axis=-1, keepdims=True), 1e-30)


def to_qblocks(a):
    b, t = a.shape[:2]
    return jnp.moveaxis(a.reshape((b, t // Q_BLOCK, Q_BLOCK) + a.shape[2:]), 1, 0)


def from_qblocks(a, width):
    nb, b = a.shape[:2]
    return jnp.moveaxis(a, 0, 1).reshape(b, nb * Q_BLOCK, width)


def mixer_inputs(h, w_in, b_f, pos):
    b, t, _ = h.shape
    q_f, k_f, v_f, f_lg, q_n, kc, vc, ks, vs, kw, vw, g_n = jnp.split(h @ w_in, IN_SPLIT_IDX, axis=-1)
    heads = lambda a, n: a.reshape(b, t, n, HEAD_DIM)
    q_f = q_f.reshape(b, t, FOX_KV_HEADS, FOX_GROUP, HEAD_DIM)
    logf = jax.nn.log_sigmoid((f_lg + b_f).astype(jnp.float32))
    q_n = heads(q_n, NSA_HEADS)
    q_r = rope(q_n, pos).reshape(b, t, NSA_KV_HEADS, NSA_GROUP, HEAD_DIM)
    q_n = q_n.reshape(b, t, NSA_KV_HEADS, NSA_GROUP, HEAD_DIM)
    ks = rope(heads(ks, NSA_KV_HEADS), pos)
    kw = rope(heads(kw, NSA_KV_HEADS), pos)
    g_n = jax.nn.sigmoid(g_n.reshape(b, t, NSA_KV_HEADS, NSA_GROUP, NSA_BRANCHES))
    return (q_f, heads(k_f, FOX_KV_HEADS), heads(v_f, FOX_KV_HEADS), logf, q_n, q_r,
            heads(kc, NSA_KV_HEADS), heads(vc, NSA_KV_HEADS), ks, heads(vs, NSA_KV_HEADS), kw, heads(vw, NSA_KV_HEADS), g_n)


def fox_core(q, k, v, cq, ck, pos_q, pos_k):
    s = jnp.einsum('bqgrd,bkgd->bgrqk', q, k).astype(jnp.float32) * (HEAD_DIM ** -0.5)
    s = s + jnp.transpose(cq, (0, 2, 3, 1))[..., None] - jnp.transpose(ck, (0, 2, 3, 1))[..., None, :]
    p = masked_softmax(s, pos_k[None, :] <= pos_q[:, None])
    return jnp.einsum('bgrqk,bkgd->bqgrd', p.astype(v.dtype), v)


def fox_prompt(q, k, v, logf):
    b, t = q.shape[:2]
    c = jnp.cumsum(logf, axis=1).reshape(b, t, FOX_KV_HEADS, FOX_GROUP)
    pos = jnp.arange(t)
    out = lax.map(lambda a: fox_core(a[0], k, v, a[1], c, a[2], pos),
                  (to_qblocks(q), to_qblocks(c), pos.reshape(t // Q_BLOCK, Q_BLOCK)))
    return from_qblocks(out, FOX_W)


def fox_sample(q, k_new, v_new, logf_new, kv_past, logf_past, pos_q):
    db, tq = q.shape[:2]
    k = jnp.concatenate([kv_past[:, :, 0], k_new], axis=1)
    v = jnp.concatenate([kv_past[:, :, 1], v_new], axis=1)
    lf = jnp.concatenate([logf_past.astype(jnp.float32), logf_new], axis=1)
    L = lf.shape[1]
    c = jnp.cumsum(lf, axis=1).reshape(db, L, FOX_KV_HEADS, FOX_GROUP)
    o = fox_core(q, k, v, c[:, L - tq:], c, pos_q, jnp.arange(L))
    return o.reshape(db, tq, FOX_W)


def compress(kc, vc, w_ck, w_cv):
    b, L, g, d = kc.shape
    nb = L // NSA_BLOCK
    kb = kc.reshape(b, nb, NSA_BLOCK, g, d) * w_ck[None, None, :, None, :]
    vb = vc.reshape(b, nb, NSA_BLOCK, g, d) * w_cv[None, None, :, None, :]
    return jnp.mean(kb, axis=2), jnp.mean(vb, axis=2)


def to_kv_blocks(a):
    b, L, g, d = a.shape
    return jnp.transpose(a.reshape(b, L // NSA_BLOCK, NSA_BLOCK, g, d), (0, 3, 1, 2, 4))


def nsa_core(qn, qr, pos_q, kcmp, vcmp, ksg, vsg, kw, vw, pos_w, gates):
    b, tq = qn.shape[:2]
    scale = HEAD_DIM ** -0.5
    nb = kcmp.shape[1]
    blk = jnp.arange(nb)
    s_c = jnp.einsum('bqgrd,bngd->bgrqn', qn, kcmp).astype(jnp.float32) * scale
    valid_c = (blk[None, :] + 1) * NSA_BLOCK - 1 <= pos_q[:, None]
    p_c = masked_softmax(s_c, valid_c)
    o_c = jnp.einsum('bgrqn,bngd->bqgrd', p_c.astype(vcmp.dtype), vcmp)
    imp = jnp.sum(p_c, axis=2)
    cur = pos_q // NSA_BLOCK
    score = jnp.where(blk[None, :] == cur[:, None], SEL_FORCE, jnp.where(valid_c, imp, -1.0))
    n_sel = min(NSA_TOP_N, nb)
    top, idx = lax.top_k(score, n_sel)
    gather = jax.vmap(jax.vmap(lambda kb, ii: kb[ii]))
    k_sel = gather(ksg, idx).reshape(b, NSA_KV_HEADS, tq, n_sel * NSA_BLOCK, HEAD_DIM)
    v_sel = gather(vsg, idx).reshape(b, NSA_KV_HEADS, tq, n_sel * NSA_BLOCK, HEAD_DIM)
    tok = idx[..., None] * NSA_BLOCK + jnp.arange(NSA_BLOCK)
    m_s = ((top >= 0.0)[..., None] & (tok <= pos_q[None, None, :, None, None])).reshape(b, NSA_KV_HEADS, tq, n_sel * NSA_BLOCK)
    s_s = jnp.einsum('bqgrd,bgqkd->bgrqk', qr, k_sel).astype(jnp.float32) * scale
    p_s = masked_softmax(s_s, m_s[:, :, None])
    o_s = jnp.einsum('bgrqk,bgqkd->bqgrd', p_s.astype(v_sel.dtype), v_sel)
    s_w = jnp.einsum('bqgrd,bkgd->bgrqk', qr, kw).astype(jnp.float32) * scale
    dpos = pos_q[:, None] - pos_w[None, :]
    m_w = (dpos >= 0) & (dpos < NSA_WINDOW) & (pos_w[None, :] >= 0)
    p_w = masked_softmax(s_w, m_w)
    o_w = jnp.einsum('bgrqk,bkgd->bqgrd', p_w.astype(vw.dtype), vw)
    return gates[..., 0:1] * o_c + gates[..., 1:2] * o_s + gates[..., 2:3] * o_w


def nsa_prompt(qn, qr, kc, vc, ks, vs, kw, vw, gates, w_ck, w_cv):
    b, t = qn.shape[:2]
    kcmp, vcmp = compress(kc, vc, w_ck, w_cv)
    ksg, vsg = to_kv_blocks(ks), to_kv_blocks(vs)
    pad = jnp.zeros((b, NSA_WINDOW, NSA_KV_HEADS, HEAD_DIM), kw.dtype)
    kwp = jnp.concatenate([pad, kw], axis=1)
    vwp = jnp.concatenate([pad, vw], axis=1)
    pos_wp = jnp.arange(-NSA_WINDOW, t)
    span = NSA_WINDOW + Q_BLOCK

    def block(a):
        i, qn_b, qr_b, g_b = a
        start = i * Q_BLOCK
        kw_b = lax.dynamic_slice_in_dim(kwp, start, span, axis=1)
        vw_b = lax.dynamic_slice_in_dim(vwp, start, span, axis=1)
        pw_b = lax.dynamic_slice_in_dim(pos_wp, start, span)
        return nsa_core(qn_b, qr_b, start + jnp.arange(Q_BLOCK), kcmp, vcmp, ksg, vsg, kw_b, vw_b, pw_b, g_b)

    out = lax.map(block, (jnp.arange(t // Q_BLOCK), to_qblocks(qn), to_qblocks(qr), to_qblocks(gates)))
    return from_qblocks(out, NSA_W)


def nsa_sample(qn, qr, kc, vc, ks, vs, kw, vw, gates, nsa_past, win_buf, w_ck, w_cv, pos_q):
    db, tq = qn.shape[:2]
    L = PAST_LEN + tq
    Lp = -(-L // NSA_BLOCK) * NSA_BLOCK

    def full(past_rows, new_rows):
        a = jnp.concatenate([past_rows, new_rows], axis=1)
        return jnp.pad(a, ((0, 0), (0, Lp - L), (0, 0), (0, 0)))

    kcmp, vcmp = compress(full(nsa_past[:, :, 0], kc), full(nsa_past[:, :, 1], vc), w_ck, w_cv)
    ksg = to_kv_blocks(full(nsa_past[:, :, 2], ks))
    vsg = to_kv_blocks(full(nsa_past[:, :, 3], vs))
    wb = win_buf.shape[1]
    kw_a = jnp.concatenate([win_buf[:, :, 0], kw], axis=1)
    vw_a = jnp.concatenate([win_buf[:, :, 1], vw], axis=1)
    pos_w = PAST_LEN - wb + jnp.arange(wb + tq)
    o = nsa_core(qn, qr, pos_q, kcmp, vcmp, ksg, vsg, kw_a, vw_a, pos_w, gates)
    new_win = jnp.stack([kw_a[:, tq:], vw_a[:, tq:]], axis=2)
    return o.reshape(db, tq, NSA_W), new_win


def merge_branches(h, o_f, o_n, w_a, w_b, w_merge, b_merge, w_o):
    gate = jax.nn.sigmoid(h @ w_merge + b_merge)
    g_a, g_b = jnp.split(gate, 2, axis=-1)
    return (g_a * (o_f @ w_a) + g_b * (o_n @ w_b)) @ w_o


def peer_ffn(h, w_pq, peer_keys, peer_u, peer_v):
    b, t, _ = h.shape
    q = (h @ w_pq).reshape(b, t, PEER_HEADS, 2, PEER_HALF)
    s = jnp.einsum('bthcd,hcnd->bthcn', q, peer_keys).astype(jnp.float32)
    s1, i1 = lax.top_k(s[..., 0, :], PEER_TOPK)
    s2, i2 = lax.top_k(s[..., 1, :], PEER_TOPK)
    cand = (s1[..., :, None] + s2[..., None, :]).reshape(b, t, PEER_HEADS, PEER_TOPK * PEER_TOPK)
    cidx = (i1[..., :, None] * PEER_KEYS + i2[..., None, :]).reshape(b, t, PEER_HEADS, PEER_TOPK * PEER_TOPK)
    top, sel = lax.top_k(cand, PEER_TOPK)
    eidx = jnp.take_along_axis(cidx, sel, axis=-1).reshape(b, t, PEER_HEADS * PEER_TOPK)
    g = jax.nn.softmax(top, axis=-1).reshape(b, t, PEER_HEADS * PEER_TOPK)
    act = jax.nn.gelu(jnp.einsum('btd,nd->btn', h, peer_u))
    wsel = (g * jnp.take_along_axis(act, eidx, axis=-1).astype(jnp.float32)).astype(act.dtype)
    coef = jnp.zeros(act.shape, act.dtype).at[jnp.arange(b)[:, None, None], jnp.arange(t)[None, :, None], eidx].add(wsel)
    return coef @ peer_v


def channel_and_ple(x, p, g_ffn, w_pq, peer_keys, peer_u, peer_v, g_ple, w_ple_gate, w_ple):
    x = x + peer_ffn(rmsnorm(x, g_ffn), w_pq, peer_keys, peer_u, peer_v)
    gate = jax.nn.sigmoid(rmsnorm(x, g_ple) @ w_ple_gate)
    return x + gate * (p @ w_ple)


def setup_inputs(seed: int = 0) -> dict:
    key = jax.random.key(seed)
    k = jax.random.split(key, 32)
    f32 = jnp.float32
    nrm = lambda kk, shape, scale: jax.random.normal(kk, shape, f32) * scale
    D = D_MODEL
    n_pages = PAST_LEN // PAGE_SIZE
    n_used = DEC_BATCH * n_pages
    n_pool = n_used + max(1, n_used // 4)
    win_buf = min(NSA_WINDOW, PAST_LEN)
    page_table = jax.random.permutation(k[0], n_pool)[:n_used].reshape(DEC_BATCH, n_pages).astype(jnp.int32)
    return {
        "x_prompt": nrm(k[1], (BATCH, SEQ, D), 1.0),
        "x_sample": nrm(k[2], (DEC_BATCH, DEC_SEQ, D), 1.0),
        "cache_fox_kv": nrm(k[3], (DEPTH, n_pool, PAGE_SIZE, 2, FOX_KV_HEADS, HEAD_DIM), 1.0),
        "cache_fox_logf": jax.nn.log_sigmoid(FORGET_BIAS_INIT + nrm(k[4], (DEPTH, n_pool, PAGE_SIZE, FOX_HEADS), 1.0)),
        "cache_nsa_kv": nrm(k[5], (DEPTH, n_pool, PAGE_SIZE, NSA_N_KV, NSA_KV_HEADS, HEAD_DIM), 1.0),
        "state_nsa_win": nrm(k[6], (DEPTH, DEC_BATCH, win_buf, 2, NSA_KV_HEADS, HEAD_DIM), 1.0),
        "page_table": page_table,
        "p_prompt": nrm(k[7], (DEPTH, BATCH, SEQ, PLE_DIM), 1.0),
        "p_sample": nrm(k[8], (DEPTH, DEC_BATCH, DEC_SEQ, PLE_DIM), 1.0),
        "g_mix": 1.0 + nrm(k[9], (DEPTH, D), 0.05),
        "w_in": nrm(k[10], (DEPTH, D, D_IN), D ** -0.5),
        "b_f": FORGET_BIAS_INIT + nrm(k[11], (DEPTH, FOX_HEADS), 0.5),
        "w_ck": 1.0 + nrm(k[12], (DEPTH, NSA_BLOCK, HEAD_DIM), 0.1),
        "w_cv": 1.0 + nrm(k[13], (DEPTH, NSA_BLOCK, HEAD_DIM), 0.1),
        "w_a": nrm(k[14], (DEPTH, FOX_W, D), FOX_W ** -0.5),
        "w_b": nrm(k[15], (DEPTH, NSA_W, D), NSA_W ** -0.5),
        "w_merge": nrm(k[16], (DEPTH, D, 2 * D), D ** -0.5),
        "b_merge": nrm(k[17], (DEPTH, 2 * D), 0.01),
        "w_o": nrm(k[18], (DEPTH, D, D), D ** -0.5),
        "g_ffn": 1.0 + nrm(k[19], (DEPTH, D), 0.05),
        "w_pq": nrm(k[20], (DEPTH, D, PEER_HEADS * PEER_QDIM), D ** -0.5),
        "peer_keys": nrm(k[21], (DEPTH, PEER_HEADS, 2, PEER_KEYS, PEER_HALF), PEER_HALF ** -0.5),
        "peer_u": nrm(k[22], (DEPTH, PEER_EXPERTS, D), D ** -0.5),
        "peer_v": nrm(k[23], (DEPTH, PEER_EXPERTS, D), (PEER_HEADS * PEER_TOPK) ** -0.5),
        "g_ple": 1.0 + nrm(k[24], (DEPTH, D), 0.05),
        "w_ple_gate": nrm(k[25], (DEPTH, D, D), D ** -0.5),
        "w_ple": nrm(k[26], (DEPTH, PLE_DIM, D), PLE_DIM ** -0.5),
        "g_final": 1.0 + nrm(k[27], (D,), 0.05),
    }


def reference(x_prompt, x_sample, cache_fox_kv, cache_fox_logf, cache_nsa_kv, state_nsa_win, page_table,
              p_prompt, p_sample, g_mix, w_in, b_f, w_ck, w_cv, w_a, w_b, w_merge, b_merge, w_o,
              g_ffn, w_pq, peer_keys, peer_u, peer_v, g_ple, w_ple_gate, w_ple, g_final):
    xp, xs = x_prompt, x_sample
    b, t = xp.shape[:2]
    db, ts = xs.shape[:2]
    pos_p = jnp.arange(t)
    pos_s = PAST_LEN + jnp.arange(ts)
    wb_p = min(NSA_WINDOW, t)
    fkv_p, flf_p, nkv_p, nwin_p = [], [], [], []
    fkv_s, flf_s, nkv_s, nwin_s = [], [], [], []
    for i in range(DEPTH):
        hp = rmsnorm(xp, g_mix[i])
        q_f, k_f, v_f, logf, q_n, q_r, kc, vc, ks, vs, kw, vw, g_n = mixer_inputs(hp, w_in[i], b_f[i], pos_p)
        o_f = fox_prompt(q_f, k_f, v_f, logf)
        o_n = nsa_prompt(q_n, q_r, kc, vc, ks, vs, kw, vw, g_n, w_ck[i], w_cv[i])
        xp = xp + merge_branches(hp, o_f, o_n, w_a[i], w_b[i], w_merge[i], b_merge[i], w_o[i])
        xp = channel_and_ple(xp, p_prompt[i], g_ffn[i], w_pq[i], peer_keys[i], peer_u[i], peer_v[i], g_ple[i], w_ple_gate[i], w_ple[i])
        fkv_p.append(jnp.stack([k_f, v_f], axis=2))
        flf_p.append(logf)
        nkv_p.append(jnp.stack([kc, vc, ks, vs], axis=2))
        nwin_p.append(jnp.stack([kw[:, t - wb_p:], vw[:, t - wb_p:]], axis=2))
        hs = rmsnorm(xs, g_mix[i])
        q_f, k_f, v_f, logf, q_n, q_r, kc, vc, ks, vs, kw, vw, g_n = mixer_inputs(hs, w_in[i], b_f[i], pos_s)
        kv_past = cache_fox_kv[i][page_table].reshape(db, PAST_LEN, 2, FOX_KV_HEADS, HEAD_DIM)
        lf_past = cache_fox_logf[i][page_table].reshape(db, PAST_LEN, FOX_HEADS)
        o_f = fox_sample(q_f, k_f, v_f, logf, kv_past, lf_past, pos_s)
        nsa_past = cache_nsa_kv[i][page_table].reshape(db, PAST_LEN, NSA_N_KV, NSA_KV_HEADS, HEAD_DIM)
        o_n, new_win = nsa_sample(q_n, q_r, kc, vc, ks, vs, kw, vw, g_n, nsa_past, state_nsa_win[i], w_ck[i], w_cv[i], pos_s)
        xs = xs + merge_branches(hs, o_f, o_n, w_a[i], w_b[i], w_merge[i], b_merge[i], w_o[i])
        xs = channel_and_ple(xs, p_sample[i], g_ffn[i], w_pq[i], peer_keys[i], peer_u[i], peer_v[i], g_ple[i], w_ple_gate[i], w_ple[i])
        fkv_s.append(jnp.stack([k_f, v_f], axis=2))
        flf_s.append(logf)
        nkv_s.append(jnp.stack([kc, vc, ks, vs], axis=2))
        nwin_s.append(new_win)
    y_prompt = rmsnorm(xp, g_final)
    y_sample = rmsnorm(xs, g_final)
    return (y_prompt, y_sample,
            jnp.stack(fkv_p), jnp.stack(flf_p), jnp.stack(nkv_p), jnp.stack(nwin_p),
            jnp.stack(fkv_s), jnp.stack(flf_s), jnp.stack(nkv_s), jnp.stack(nwin_s))
```

```python
import functools

import jax
import jax.numpy as jnp
from jax import lax
from jax.experimental import pallas as pl
from jax.experimental.pallas import tpu as pltpu

F32 = jnp.float32
BF16 = jnp.bfloat16
NEG_INF = float("-inf")

HEAD_DIM = 128
FOX_HEADS = 16
FOX_KV_HEADS = 4
FOX_GROUP = FOX_HEADS // FOX_KV_HEADS
NSA_HEADS = 16
NSA_KV_HEADS = 2
NSA_GROUP = NSA_HEADS // NSA_KV_HEADS
NSA_BLOCK = 64
BLOCK_SHIFT = NSA_BLOCK.bit_length() - 1
NSA_TOP_N = 16
NSA_WINDOW = 512
NSA_BRANCHES = 3
SEL_FORCE = 1e4
ROPE_THETA = 10000.0
PAGE_SIZE = 128
PEER_HEADS = 8
PEER_KEYS = 128
PEER_TOPK = 16
PEER_HALF = 128
RMS_EPS = 1e-6
SCALE = HEAD_DIM ** -0.5

FOX_W = FOX_HEADS * HEAD_DIM
FOX_KV_W = FOX_KV_HEADS * HEAD_DIM
NSA_W = NSA_HEADS * HEAD_DIM
NSA_KV_W = NSA_KV_HEADS * HEAD_DIM
LANES = 128
VMEM_LIMIT = 56 * 1024 * 1024

NT_DIMS = (((1,), (1,)), ((), ()))


def _params(*sem):
    return pltpu.CompilerParams(dimension_semantics=sem, vmem_limit_bytes=VMEM_LIMIT)


def _tile(n, target, align=8):
    if n <= target:
        return n
    best = None
    for t in range(align, target + 1, align):
        if n % t == 0:
            best = t
    assert best is not None, (n, target, align)
    return best


def _dot(a, b):
    return jnp.dot(a, b, preferred_element_type=F32)


def _dot_nt(a, b):
    return lax.dot_general(a, b, NT_DIMS, preferred_element_type=F32)


def _iota(shape, dim, dtype=jnp.int32):
    return lax.broadcasted_iota(dtype, shape, dim)


def _rmsnorm_body(x_ref, g_ref, o_ref):
    x = x_ref[...]
    y = x * lax.rsqrt(jnp.mean(x * x, axis=-1, keepdims=True) + RMS_EPS)
    o_ref[...] = (y * g_ref[...]).astype(o_ref.dtype)


def _rmsnorm(x, g, out_dtype):
    n, d = x.shape
    tr = _tile(n, 256)
    return pl.pallas_call(
        _rmsnorm_body,
        grid=(n // tr,),
        in_specs=[pl.BlockSpec((tr, d), lambda i: (i, 0)), pl.BlockSpec((1, d), lambda i: (0, 0))],
        out_specs=pl.BlockSpec((tr, d), lambda i: (i, 0)),
        out_shape=jax.ShapeDtypeStruct((n, d), out_dtype),
        compiler_params=_params("parallel"),
    )(x, g.reshape(1, d))


def _gelu_tanh(x):
    return 0.5 * x * (1.0 + jnp.tanh(0.7978845608028654 * (x + 0.044715 * (x * x * x))))


def _mm_body(*refs, n_extra, epilogue, nt, nk, split_out):
    x_ref, w_ref = refs[0], refs[1]
    extra = refs[2:2 + n_extra]
    o_ref = refs[2 + n_extra]
    w = w_ref[...].astype(BF16)
    part = _dot_nt(x_ref[...], w) if nt else _dot(x_ref[...], w)

    def finish(acc):
        res = epilogue(acc, *[e[...] for e in extra]).astype(o_ref.dtype)
        if split_out:
            for j in range(o_ref.shape[0]):
                o_ref[j] = res[:, j * LANES:(j + 1) * LANES]
        else:
            o_ref[...] = res

    if nk == 1:
        finish(part)
        return
    acc_ref = refs[3 + n_extra]
    k = pl.program_id(2)

    @pl.when(k == 0)
    def _():
        acc_ref[...] = part

    @pl.when(k > 0)
    def _():
        acc_ref[...] += part

    @pl.when(k == nk - 1)
    def _():
        finish(acc_ref[...])


def _matmul(x, w, *, tm, tn, tk, out_dtype, epilogue=None, extra=(), nt=False, split_out=False):
    m, kdim = x.shape
    n = w.shape[0] if nt else w.shape[1]
    tm, tn, tk = _tile(m, tm), _tile(n, tn, LANES), _tile(kdim, tk, LANES)
    nk = kdim // tk
    if epilogue is None:
        epilogue = lambda acc: acc
    in_specs = [
        pl.BlockSpec((tm, tk), lambda i, j, k: (i, k)),
        pl.BlockSpec((tn, tk), lambda i, j, k: (j, k)) if nt else pl.BlockSpec((tk, tn), lambda i, j, k: (k, j)),
    ]
    args = [x, w]
    for arr, kind, off in extra:
        if kind == "tile":
            in_specs.append(pl.BlockSpec((tm, tn), lambda i, j, k, off=off: (i, j + off)))
        else:
            in_specs.append(pl.BlockSpec((1, tn), lambda i, j, k, off=off: (0, j + off)))
        args.append(arr)
    if split_out:
        out_shape = jax.ShapeDtypeStruct((n // LANES, m, LANES), out_dtype)
        out_spec = pl.BlockSpec((tn // LANES, tm, LANES), lambda i, j, k: (j, i, 0))
    else:
        out_shape = jax.ShapeDtypeStruct((m, n), out_dtype)
        out_spec = pl.BlockSpec((tm, tn), lambda i, j, k: (i, j))
    body = functools.partial(_mm_body, n_extra=len(extra), epilogue=epilogue, nt=nt, nk=nk, split_out=split_out)
    return pl.pallas_call(
        body,
        grid=(m // tm, n // tn, nk),
        in_specs=in_specs,
        out_specs=out_spec,
        out_shape=out_shape,
        scratch_shapes=[pltpu.VMEM((tm, tn), F32)] if nk > 1 else [],
        compiler_params=_params("parallel", "parallel", "arbitrary"),
    )(*args)


def _rope(x, cos, sin_signed):
    return x * cos + pltpu.roll(x, HEAD_DIM // 2, 1) * sin_signed


def _log_sigmoid(z):
    return jnp.minimum(z, 0.0) - jnp.log1p(jnp.exp(-jnp.abs(z)))


def _post_body(p_ref, p2_ref, cos_ref, sin_ref, bf_ref,
               qf_ref, fkv_ref, qn_ref, qr_ref, nkv_ref, win_ref, logf_ref, gate_ref):
    cos = cos_ref[...]
    sin = sin_ref[...]
    o = 0
    qf_ref[...] = p_ref[:, o:o + FOX_W].astype(BF16)
    o += FOX_W
    fkv_ref[...] = p_ref[:, o:o + 2 * FOX_KV_W]
    o += 2 * FOX_KV_W
    for h in range(NSA_HEADS):
        q = p_ref[:, o + h * HEAD_DIM:o + (h + 1) * HEAD_DIM]
        qn_ref[:, h * HEAD_DIM:(h + 1) * HEAD_DIM] = q.astype(BF16)
        qr_ref[:, h * HEAD_DIM:(h + 1) * HEAD_DIM] = _rope(q, cos, sin).astype(BF16)
    o += NSA_W
    nkv_ref[:, 0:2 * NSA_KV_W] = p_ref[:, o:o + 2 * NSA_KV_W]
    o += 2 * NSA_KV_W
    for g in range(NSA_KV_HEADS):
        k = p_ref[:, o + g * HEAD_DIM:o + (g + 1) * HEAD_DIM]
        nkv_ref[:, 2 * NSA_KV_W + g * HEAD_DIM:2 * NSA_KV_W + (g + 1) * HEAD_DIM] = _rope(k, cos, sin)
    o += NSA_KV_W
    nkv_ref[:, 3 * NSA_KV_W:4 * NSA_KV_W] = p_ref[:, o:o + NSA_KV_W]
    o += NSA_KV_W
    for g in range(NSA_KV_HEADS):
        k = p_ref[:, o + g * HEAD_DIM:o + (g + 1) * HEAD_DIM]
        win_ref[:, g * HEAD_DIM:(g + 1) * HEAD_DIM] = _rope(k, cos, sin)
    o += NSA_KV_W
    win_ref[:, NSA_KV_W:2 * NSA_KV_W] = p_ref[:, o:o + NSA_KV_W]
    logf = _log_sigmoid(p2_ref[:, 0:LANES] + bf_ref[...])
    logf_ref[...] = logf[:, 0:FOX_HEADS]
    gate_ref[...] = jax.nn.sigmoid(p2_ref[:, LANES:3 * LANES])


def _post_projection(p, p2, cos, sin, bf_row):
    n = p.shape[0]
    tr = _tile(n, 256)
    row = lambda w: pl.BlockSpec((tr, w), lambda i: (i, 0))
    out_shapes = [
        jax.ShapeDtypeStruct((n, FOX_W), BF16),
        jax.ShapeDtypeStruct((n, 2 * FOX_KV_W), F32),
        jax.ShapeDtypeStruct((n, NSA_W), BF16),
        jax.ShapeDtypeStruct((n, NSA_W), BF16),
        jax.ShapeDtypeStruct((n, 4 * NSA_KV_W), F32),
        jax.ShapeDtypeStruct((n, 2 * NSA_KV_W), F32),
        jax.ShapeDtypeStruct((n, FOX_HEADS), F32),
        jax.ShapeDtypeStruct((n, 2 * LANES), F32),
    ]
    return pl.pallas_call(
        _post_body,
        grid=(n // tr,),
        in_specs=[row(p.shape[1]), row(p2.shape[1]), row(LANES), row(LANES),
                  pl.BlockSpec((1, LANES), lambda i: (0, 0))],
        out_specs=[row(s.shape[1]) for s in out_shapes],
        out_shape=out_shapes,
        compiler_params=_params("parallel"),
    )(p, p2, cos, sin, bf_row)


def _split3(x):
    hi = x.astype(BF16)
    r1 = x - hi.astype(F32)
    mid = r1.astype(BF16)
    lo = (r1 - mid.astype(F32)).astype(BF16)
    return hi, mid, lo


def _cumsum_core(lf, u_ref, o_ref, carry_ref, first):
    @pl.when(first)
    def _():
        carry_ref[...] = jnp.zeros_like(carry_ref)

    hi, mid, lo = _split3(lf)
    u = u_ref[...]
    c = (_dot(hi, u) + _dot(mid, u)) + _dot(lo, u) + carry_ref[...]
    o_ref[...] = c
    r = c.shape[1]
    carry_ref[...] = c[:, r - 1:r]


def _cumsum_prompt_body(lf_ref, u_ref, o_ref, carry_ref):
    _cumsum_core(lf_ref[...], u_ref, o_ref, carry_ref, pl.program_id(1) == 0)


def _cumsum_paged_body(pt_ref, lf_ref, new_ref, u_ref, o_ref, carry_ref, *, n_pages):
    p = pl.program_id(1)
    lf = jnp.where(p == n_pages, new_ref[...], lf_ref[...])
    _cumsum_core(lf, u_ref, o_ref, carry_ref, p == 0)


def _upper_ones(r):
    return (jnp.arange(r)[:, None] <= jnp.arange(r)[None, :]).astype(BF16)


def _cumsum_prompt(lf_t):
    b, h, t = lf_t.shape
    r = _tile(t, 512, LANES)
    return pl.pallas_call(
        _cumsum_prompt_body,
        grid=(b, t // r),
        in_specs=[pl.BlockSpec((None, h, r), lambda i, j: (i, 0, j)),
                  pl.BlockSpec((r, r), lambda i, j: (0, 0))],
        out_specs=pl.BlockSpec((None, h, r), lambda i, j: (i, 0, j)),
        out_shape=jax.ShapeDtypeStruct((b, h, t), F32),
        scratch_shapes=[pltpu.VMEM((h, 1), F32)],
        compiler_params=_params("parallel", "arbitrary"),
    )(lf_t, _upper_ones(r))


def _cumsum_paged(pt_flat, cache_lf_t, new_lf_t, db, n_pages):
    h = cache_lf_t.shape[1]
    grid_spec = pltpu.PrefetchScalarGridSpec(
        num_scalar_prefetch=1,
        grid=(db, n_pages + 1),
        in_specs=[
            pl.BlockSpec((None, h, PAGE_SIZE),
                         lambda b, p, pt: (pt[b * n_pages + jnp.minimum(p, n_pages - 1)], 0, 0)),
            pl.BlockSpec((None, h, PAGE_SIZE), lambda b, p, pt: (b, 0, 0)),
            pl.BlockSpec((PAGE_SIZE, PAGE_SIZE), lambda b, p, pt: (0, 0)),
        ],
        out_specs=pl.BlockSpec((None, h, PAGE_SIZE), lambda b, p, pt: (b, 0, p)),
        scratch_shapes=[pltpu.VMEM((h, 1), F32)],
    )
    return pl.pallas_call(
        functools.partial(_cumsum_paged_body, n_pages=n_pages),
        grid_spec=grid_spec,
        out_shape=jax.ShapeDtypeStruct((db, h, (n_pages + 1) * PAGE_SIZE), F32),
        compiler_params=_params("parallel", "arbitrary"),
    )(pt_flat, cache_lf_t, new_lf_t, _upper_ones(PAGE_SIZE))


def _online_update(s, vb, m_ref, l_ref, acc_ref, idx=None):
    sl = (slice(None),) if idx is None else (idx,)
    m_old = m_ref[sl]
    m_new = jnp.maximum(m_old, jnp.max(s, axis=-1, keepdims=True))
    m_safe = jnp.where(m_new == NEG_INF, 0.0, m_new)
    alpha = jnp.exp(m_old - m_safe)
    p = jnp.exp(s - m_safe)
    l_ref[sl] = alpha * l_ref[sl] + jnp.sum(p, axis=-1, keepdims=True)
    acc_ref[sl] = alpha * acc_ref[sl] + _dot(p.astype(BF16), vb)
    m_ref[sl] = m_new


def _masked_softmax(s, keep):
    s = jnp.where(keep > 0.5, s, NEG_INF)
    m = jnp.max(s, axis=-1, keepdims=True)
    m = jnp.where(m == NEG_INF, 0.0, m)
    e = jnp.exp(s - m)
    return e / jnp.maximum(jnp.sum(e, axis=-1, keepdims=True), 1e-30)


def _normalize(acc, l):
    return acc / jnp.maximum(l, 1e-30)


def _fox_prompt_body(q_ref, k_ref, v_ref, cq_ref, ck_ref, o_ref, m_ref, l_ref, acc_ref, *, tq):
    i = pl.program_id(2)
    m_ref[...] = jnp.full_like(m_ref, NEG_INF)
    l_ref[...] = jnp.zeros_like(l_ref)
    acc_ref[...] = jnp.zeros_like(acc_ref)
    q_pos = i * tq + _iota((tq, 1), 0)

    def chunk(j, carry):
        ks = pl.multiple_of(j * tq, tq)
        kb = k_ref[pl.ds(ks, tq), :].astype(BF16)
        vb = v_ref[pl.ds(ks, tq), :].astype(BF16)
        ck = ck_ref[j]
        mask = (ks + _iota((1, tq), 1)) <= q_pos
        for r in range(FOX_GROUP):
            q = q_ref[:, r * HEAD_DIM:(r + 1) * HEAD_DIM]
            s = _dot_nt(q, kb) * SCALE + cq_ref[:, r:r + 1] - ck[r:r + 1, :]
            _online_update(jnp.where(mask, s, NEG_INF), vb, m_ref, l_ref, acc_ref, r)
        return carry

    lax.fori_loop(0, i + 1, chunk, 0)
    for r in range(FOX_GROUP):
        o_ref[:, r * HEAD_DIM:(r + 1) * HEAD_DIM] = _normalize(acc_ref[r], l_ref[r]).astype(o_ref.dtype)


def _fox_prompt(qf, fkv, c_t, b, t):
    tq = _tile(t, 256, LANES)
    nq = t // tq
    g_, r_ = FOX_KV_HEADS, FOX_GROUP
    c4 = c_t.reshape(b, g_, r_, t)
    cq = jnp.transpose(c4, (0, 1, 3, 2))
    ck = jnp.transpose(c4.reshape(b, g_, r_, nq, tq), (0, 1, 3, 2, 4))
    return pl.pallas_call(
        functools.partial(_fox_prompt_body, tq=tq),
        grid=(b, g_, nq),
        in_specs=[
            pl.BlockSpec((tq, r_ * HEAD_DIM), lambda bi, g, i: (bi * nq + i, g)),
            pl.BlockSpec((t, HEAD_DIM), lambda bi, g, i: (bi, g)),
            pl.BlockSpec((t, HEAD_DIM), lambda bi, g, i: (bi, g_ + g)),
            pl.BlockSpec((None, None, tq, r_), lambda bi, g, i: (bi, g, i, 0)),
            pl.BlockSpec((None, None, nq, r_, tq), lambda bi, g, i: (bi, g, 0, 0, 0)),
        ],
        out_specs=pl.BlockSpec((tq, r_ * HEAD_DIM), lambda bi, g, i: (bi * nq + i, g)),
        out_shape=jax.ShapeDtypeStruct((b * t, FOX_W), BF16),
        scratch_shapes=[pltpu.VMEM((r_, tq, 1), F32), pltpu.VMEM((r_, tq, 1), F32),
                        pltpu.VMEM((r_, tq, HEAD_DIM), F32)],
        compiler_params=_params("parallel", "parallel", "arbitrary"),
    )(qf, fkv, fkv, cq, ck)


def _compress_body(x_ref, w_ref, o_ref):
    x = x_ref[...]
    nb = x.shape[0] // NSA_BLOCK
    xb = x.reshape(nb, NSA_BLOCK, x.shape[1]) * w_ref[...][None]
    o_ref[...] = jnp.sum(xb, axis=1) * (1.0 / NSA_BLOCK)


def _compress_prompt(nkv, w_cmp, n_rows):
    width = 2 * NSA_KV_W
    tr = _tile(n_rows, 512, NSA_BLOCK * 8)
    return pl.pallas_call(
        _compress_body,
        grid=(n_rows // tr,),
        in_specs=[pl.BlockSpec((tr, width), lambda i: (i, 0)),
                  pl.BlockSpec((NSA_BLOCK, width), lambda i: (0, 0))],
        out_specs=pl.BlockSpec((tr // NSA_BLOCK, width), lambda i: (i, 0)),
        out_shape=jax.ShapeDtypeStruct((n_rows // NSA_BLOCK, width), F32),
        compiler_params=_params("parallel"),
    )(nkv, w_cmp)


def _compress_paged_body(pt_ref, x_ref, new_ref, w_ref, o_ref, *, n_pages):
    p = pl.program_id(1)
    x = jnp.where(p == n_pages, new_ref[...], x_ref[...])
    nb = PAGE_SIZE // NSA_BLOCK
    xb = x.reshape(nb, NSA_BLOCK, x.shape[1]) * w_ref[...][None]
    o_ref[...] = jnp.sum(xb, axis=1) * (1.0 / NSA_BLOCK)


def _compress_paged(pt_flat, cache_nsa, new_pad, w_cmp, db, n_pages):
    width = 2 * NSA_KV_W
    nb = PAGE_SIZE // NSA_BLOCK
    grid_spec = pltpu.PrefetchScalarGridSpec(
        num_scalar_prefetch=1,
        grid=(db, n_pages + 1),
        in_specs=[
            pl.BlockSpec((None, PAGE_SIZE, width),
                         lambda b, p, pt: (pt[b * n_pages + jnp.minimum(p, n_pages - 1)], 0, 0)),
            pl.BlockSpec((None, PAGE_SIZE, width), lambda b, p, pt: (b, 0, 0)),
            pl.BlockSpec((NSA_BLOCK, width), lambda b, p, pt: (0, 0)),
        ],
        out_specs=pl.BlockSpec((None, None, nb, width), lambda b, p, pt: (b, p, 0, 0)),
    )
    out = pl.pallas_call(
        functools.partial(_compress_paged_body, n_pages=n_pages),
        grid_spec=grid_spec,
        out_shape=jax.ShapeDtypeStruct((db, n_pages + 1, nb, width), F32),
        compiler_params=_params("parallel", "arbitrary"),
    )(pt_flat, cache_nsa, new_pad, w_cmp)
    return out.reshape(db, (n_pages + 1) * nb, width)


def _select_blocks(score, n_sel):
    nb = score.shape[1]
    blk = _iota(score.shape, 1).astype(F32)
    sel = jnp.zeros(score.shape, F32)
    work = score
    for _ in range(n_sel):
        mx = jnp.max(work, axis=-1, keepdims=True)
        ix = jnp.min(jnp.where(work == mx, blk, float(nb)), axis=-1, keepdims=True)
        hit = blk == ix
        sel = jnp.where(hit, jnp.where(mx >= 0.0, 1.0, 0.0), sel)
        work = jnp.where(hit, NEG_INF, work)
    return sel


def _expand_blocks(sel_b, key_start, n_keys):
    nb = sel_b.shape[1]
    key_blk = lax.shift_right_logical(key_start + _iota((nb, n_keys), 1), BLOCK_SHIFT)
    e =jnp.where(_iota((nb, n_keys), 0) == key_blk, 1.0, 0.0).astype(BF16)
    return _dot(sel_b, e)


def _nsa_prompt_body(qn_ref, qr_ref, kc_ref, vc_ref, ks_ref, vs_ref, kw_ref, vw_ref, g_ref, o_ref,
                     oc_ref, m_s, l_s, acc_s, m_w, l_w, acc_w, *, tq, tk, nb):
    i = pl.program_id(2)
    rr = NSA_GROUP
    stack = lambda ref: jnp.concatenate([ref[:, r * HEAD_DIM:(r + 1) * HEAD_DIM] for r in range(rr)], axis=0)
    tile_rows = lambda a: jnp.concatenate([a] * rr, axis=0)
    qn = stack(qn_ref)
    qr = stack(qr_ref)
    pos_q = i * tq + _iota((tq, 1), 0)

    blk = _iota((tq, nb), 1)
    valid = (blk + 1) * NSA_BLOCK - 1 <= pos_q
    s_c = _dot_nt(qn, kc_ref[...].astype(BF16)) * SCALE
    p_c = _masked_softmax(s_c, tile_rows(jnp.where(valid, 1.0, 0.0)))
    oc_ref[...] = _dot(p_c.astype(BF16), vc_ref[...].astype(BF16))
    imp = p_c[0:tq]
    for r in range(1, rr):
        imp = imp + p_c[r * tq:(r + 1) * tq]
    cur = lax.shift_right_logical(pos_q, BLOCK_SHIFT)
    score = jnp.where(blk == cur, SEL_FORCE, jnp.where(valid, imp, -1.0))
    sel_b = _select_blocks(score, min(NSA_TOP_N, nb)).astype(BF16)

    for ref, val in ((m_s, NEG_INF), (l_s, 0.0), (acc_s, 0.0), (m_w, NEG_INF), (l_w, 0.0), (acc_w, 0.0)):
        ref[...] = jnp.full_like(ref, val)

    def sel_chunk(j, carry):
        ks = pl.multiple_of(j * tk, tk)
        kb = ks_ref[pl.ds(ks, tk), :].astype(BF16)
        vb = vs_ref[pl.ds(ks, tk), :].astype(BF16)
        keep = jnp.where((ks + _iota((1, tk), 1)) <= pos_q, _expand_blocks(sel_b, ks, tk), 0.0)
        s = _dot_nt(qr, kb) * SCALE
        _online_update(jnp.where(tile_rows(keep) > 0.5, s, NEG_INF), vb, m_s, l_s, acc_s)
        return carry

    lax.fori_loop(0, ((i + 1) * tq + tk - 1) // tk, sel_chunk, 0)

    def win_chunk(j, carry):
        ks = pl.multiple_of(j * tq, tq)
        kb = kw_ref[pl.ds(ks, tq), :].astype(BF16)
        vb = vw_ref[pl.ds(ks, tq), :].astype(BF16)
        dpos = pos_q - (ks + _iota((1, tq), 1))
        keep = jnp.where(dpos >= 0, jnp.where(dpos < NSA_WINDOW, 1.0, 0.0), 0.0)
        s = _dot_nt(qr, kb) * SCALE
        _online_update(jnp.where(tile_rows(keep) > 0.5, s, NEG_INF), vb, m_w, l_w, acc_w)
        return carry

    lax.fori_loop(jnp.maximum(i - NSA_WINDOW // tq, 0), i + 1, win_chunk, 0)

    gates = g_ref[...]
    for r in range(rr):
        rows = slice(r * tq, (r + 1) * tq)
        gate = lambda br: gates[:, br * rr + r:br * rr + r + 1]
        o = (gate(0) * oc_ref[rows] + gate(1) * _normalize(acc_s[rows], l_s[rows])
             + gate(2) * _normalize(acc_w[rows], l_w[rows]))
        o_ref[:, r * HEAD_DIM:(r + 1) * HEAD_DIM] = o.astype(o_ref.dtype)


def _nsa_prompt(qn, qr, cmp_p, nkv, win, gates, b, t):
    tq = _tile(t, 128, LANES)
    tk = _tile(t, 256, LANES)
    nq = t // tq
    nb = t // NSA_BLOCK
    g_, rows = NSA_KV_HEADS, NSA_GROUP * tq
    qspec = pl.BlockSpec((tq, NSA_GROUP * HEAD_DIM), lambda bi, g, i: (bi * nq + i, g))
    col = lambda off: pl.BlockSpec((t, HEAD_DIM), lambda bi, g, i: (bi, off + g))
    return pl.pallas_call(
        functools.partial(_nsa_prompt_body, tq=tq, tk=tk, nb=nb),
        grid=(b, g_, nq),
        in_specs=[
            qspec, qspec,
            pl.BlockSpec((nb, HEAD_DIM), lambda bi, g, i: (bi, g)),
            pl.BlockSpec((nb, HEAD_DIM), lambda bi, g, i: (bi, g_ + g)),
            col(2 * g_), col(3 * g_),
            col(0), col(g_),
            pl.BlockSpec((tq, LANES), lambda bi, g, i: (bi * nq + i, g)),
        ],
        out_specs=qspec,
        out_shape=jax.ShapeDtypeStruct((b * t, NSA_W), BF16),
        scratch_shapes=[pltpu.VMEM((rows, HEAD_DIM), F32),
                        pltpu.VMEM((rows, 1), F32), pltpu.VMEM((rows, 1), F32), pltpu.VMEM((rows, HEAD_DIM), F32),
                        pltpu.VMEM((rows, 1), F32), pltpu.VMEM((rows, 1), F32), pltpu.VMEM((rows, HEAD_DIM), F32)],
        compiler_params=_params("parallel", "parallel", "arbitrary"),
    )(qn, qr, cmp_p, cmp_p, nkv, nkv, win, win, gates)


def _stack_heads(q_ref, col0, n_heads):
    return jnp.concatenate([q_ref[:, col0 + r * HEAD_DIM:col0 + (r + 1) * HEAD_DIM] for r in range(n_heads)], axis=0)


def _fox_decode_body(pt_ref, q_ref, kv_ref, new_ref, ck_ref, cq_ref, o_ref,
                     qs_ref, cqs_ref, m_ref, l_ref, acc_ref, *, n_pages, ts):
    p = pl.program_id(1)
    rr = FOX_GROUP
    rows = rr * ts

    @pl.when(p == 0)
    def _():
        m_ref[...] = jnp.full_like(m_ref, NEG_INF)
        l_ref[...] = jnp.zeros_like(l_ref)
        acc_ref[...] = jnp.zeros_like(acc_ref)
        cq = cq_ref[...]
        for g in range(FOX_KV_HEADS):
            qs_ref[g] = _stack_heads(q_ref, g * rr * HEAD_DIM, rr).astype(BF16)
            cqs_ref[g] = jnp.concatenate([cq[:, g * rr + r:g * rr + r + 1] for r in range(rr)], axis=0)

    def process(src_ref, is_new):
        ck = ck_ref[...]
        if is_new:
            keep_i = jnp.where(_iota((ts, PAGE_SIZE), 1) <= _iota((ts, PAGE_SIZE), 0), 1.0, 0.0)
            keep = jnp.concatenate([keep_i] * rr, axis=0) > 0.5
        for g in range(FOX_KV_HEADS):
            kb = src_ref[:, g * HEAD_DIM:(g + 1) * HEAD_DIM].astype(BF16)
            vb = src_ref[:, FOX_KV_W + g * HEAD_DIM:FOX_KV_W + (g + 1) * HEAD_DIM].astype(BF16)
            ckg = jnp.concatenate(
                [jnp.broadcast_to(ck[g * rr + r:g * rr + r + 1, :], (ts, PAGE_SIZE)) for r in range(rr)], axis=0)
            s = _dot_nt(qs_ref[g], kb) * SCALE + cqs_ref[g] - ckg
            if is_new:
                s = jnp.where(keep, s, NEG_INF)
            _online_update(s, vb, m_ref, l_ref, acc_ref, g)

    @pl.when(p < n_pages)
    def _():
        process(kv_ref, False)

    @pl.when(p == n_pages)
    def _():
        process(new_ref, True)
        for g in range(FOX_KV_HEADS):
            o = _normalize(acc_ref[g], l_ref[g])
            for r in range(rr):
                c0 = (g * rr + r) * HEAD_DIM
                o_ref[:, c0:c0 + HEAD_DIM] = o[r * ts:(r + 1) * ts, :]


def _fox_decode(pt_flat, q_s, cache_kv, new_pad, c_all_t, cq, db, ts, n_pages):
    rows = FOX_GROUP * ts
    grid_spec = pltpu.PrefetchScalarGridSpec(
        num_scalar_prefetch=1,
        grid=(db, n_pages + 1),
        in_specs=[
            pl.BlockSpec((None, ts, FOX_W), lambda b, p, pt: (b, 0, 0)),
            pl.BlockSpec((None, PAGE_SIZE, 2 * FOX_KV_W),
                         lambda b, p, pt: (pt[b * n_pages + jnp.minimum(p, n_pages - 1)], 0, 0)),
            pl.BlockSpec((None, PAGE_SIZE, 2 * FOX_KV_W), lambda b, p, pt: (b, 0, 0)),
            pl.BlockSpec((None, FOX_HEADS, PAGE_SIZE), lambda b, p, pt: (b, 0, p)),
            pl.BlockSpec((None, ts, FOX_HEADS), lambda b, p, pt: (b, 0, 0)),
        ],
        out_specs=pl.BlockSpec((None, ts, FOX_W), lambda b, p, pt: (b, 0, 0)),
        scratch_shapes=[pltpu.VMEM((FOX_KV_HEADS, rows, HEAD_DIM), BF16),
                        pltpu.VMEM((FOX_KV_HEADS, rows, 1), F32),
                        pltpu.VMEM((FOX_KV_HEADS, rows, 1), F32),
                        pltpu.VMEM((FOX_KV_HEADS, rows, 1), F32),
                        pltpu.VMEM((FOX_KV_HEADS, rows, HEAD_DIM), F32)],
    )
    return pl.pallas_call(
        functools.partial(_fox_decode_body, n_pages=n_pages, ts=ts),
        grid_spec=grid_spec,
        out_shape=jax.ShapeDtypeStruct((db, ts, FOX_W), F32),
        compiler_params=_params("parallel", "arbitrary"),
    )(pt_flat, q_s, cache_kv, new_pad, c_all_t, cq)


def _nsa_decode_body(pt_ref, qn_ref, qr_ref, cmp_ref, kv_ref, new_ref, win_ref, wnew_ref, g_ref,
                     o_ref, nwin_ref, qrs_ref, oc_ref, ow_ref, sel_ref, m_ref, l_ref, acc_ref,
                     *, n_pages, ts, past, nbp, wb):
    p = pl.program_id(1)
    rr = NSA_GROUP
    rows = rr * ts
    tile_rows = lambda a: jnp.concatenate([a] * rr, axis=0)
    qi_col = _iota((ts, 1), 0)

    @pl.when(p == 0)
    def _():
        m_ref[...] = jnp.full_like(m_ref, NEG_INF)
        l_ref[...] = jnp.zeros_like(l_ref)
        acc_ref[...] = jnp.zeros_like(acc_ref)
        pos_q = past + qi_col
        blk = _iota((ts, nbp), 1)
        valid = (blk + 1) * NSA_BLOCK - 1 <= pos_q
        exists = blk * NSA_BLOCK < past + ts
        wrow = _iota((ts, wb), 1)
        dpos = wb + qi_col - wrow
        keep_buf = jnp.where(dpos >= 0, jnp.where(dpos < NSA_WINDOW, 1.0, 0.0), 0.0)
        nrow = _iota((ts, PAGE_SIZE), 1)
        keep_new = jnp.where(nrow <= qi_col, 1.0, 0.0)
        keep_w = tile_rows(jnp.concatenate([keep_buf, keep_new], axis=1))
        keep_c = tile_rows(jnp.where(valid, 1.0, 0.0))
        cur = lax.shift_right_logical(pos_q, BLOCK_SHIFT)
        for g in range(NSA_KV_HEADS):
            qn = _stack_heads(qn_ref, g * rr * HEAD_DIM, rr).astype(BF16)
            qr = _stack_heads(qr_ref, g * rr * HEAD_DIM, rr).astype(BF16)
            qrs_ref[g] = qr
            kc = cmp_ref[:, g * HEAD_DIM:(g + 1) * HEAD_DIM].astype(BF16)
            vc = cmp_ref[:, NSA_KV_W + g * HEAD_DIM:NSA_KV_W + (g + 1) * HEAD_DIM].astype(BF16)
            p_c = _masked_softmax(_dot_nt(qn, kc) * SCALE, keep_c)
            oc_ref[g] = _dot(p_c.astype(BF16), vc)
            imp = p_c[0:ts]
            for r in range(1, rr):
                imp = imp + p_c[r * ts:(r + 1) * ts]
            score = jnp.where(blk == cur, SEL_FORCE, jnp.where(valid, imp, -1.0))
            score = jnp.where(exists, score, NEG_INF)
            sel = _select_blocks(score, min(NSA_TOP_N, (past + ts + NSA_BLOCK - 1) // NSA_BLOCK))
            sel_ref[g] = tile_rows(sel).astype(BF16)
            kw = win_ref[:, g * HEAD_DIM:(g + 1) * HEAD_DIM].astype(BF16)
            vw = win_ref[:, NSA_KV_W + g * HEAD_DIM:NSA_KV_W + (g + 1) * HEAD_DIM].astype(BF16)
            kwn = wnew_ref[:, g * HEAD_DIM:(g + 1) * HEAD_DIM].astype(BF16)
            vwn = wnew_ref[:, NSA_KV_W + g * HEAD_DIM:NSA_KV_W + (g + 1) * HEAD_DIM].astype(BF16)
            s_w = jnp.concatenate([_dot_nt(qr, kw), _dot_nt(qr, kwn)], axis=1) * SCALE
            p_w = _masked_softmax(s_w, keep_w).astype(BF16)
            ow_ref[g] = _dot(p_w[:, 0:wb], vw) + _dot(p_w[:, wb:wb + PAGE_SIZE], vwn)
        nwin_ref[0:wb - ts, :] = win_ref[ts:wb, :]
        nwin_ref[wb - ts:wb, :] = wnew_ref[0:ts, :]

    def process(src_ref, is_new):
        if is_new:
            keep_tok = tile_rows(jnp.where(_iota((ts, PAGE_SIZE), 1) <= qi_col, 1.0, 0.0))
        for g in range(NSA_KV_HEADS):
            kb = src_ref[:, g * HEAD_DIM:(g + 1) * HEAD_DIM].astype(BF16)
            vb = src_ref[:, NSA_KV_W + g * HEAD_DIM:NSA_KV_W + (g + 1) * HEAD_DIM].astype(BF16)
            keep = _expand_blocks(sel_ref[g], p * PAGE_SIZE, PAGE_SIZE)
            if is_new:
                keep = keep * keep_tok
            s = _dot_nt(qrs_ref[g], kb) * SCALE
            _online_update(jnp.where(keep > 0.5, s, NEG_INF), vb, m_ref, l_ref, acc_ref, g)

    @pl.when(p < n_pages)
    def _():
        process(kv_ref, False)

    @pl.when(p == n_pages)
    def _():
        process(new_ref, True)
        gates = g_ref[...]
        for g in range(NSA_KV_HEADS):
            o_s = _normalize(acc_ref[g], l_ref[g])

            def gate(br):
                c0 = g * LANES + br * rr
                return jnp.concatenate([gates[:, c0 + r:c0 + r + 1] for r in range(rr)], axis=0)

            o = gate(0) * oc_ref[g] + gate(1) * o_s + gate(2) * ow_ref[g]
            for r in range(rr):
                c0 = (g * rr + r) * HEAD_DIM
                o_ref[:, c0:c0 + HEAD_DIM] = o[r * ts:(r + 1) * ts, :]


def _nsa_decode(pt_flat, qn_s, qr_s, cmp_s, cache_nsa, new_pad, win_buf, wnew_pad, gates_s,
                db, ts, n_pages):
    rows = NSA_GROUP * ts
    past = n_pages * PAGE_SIZE
    nbp = cmp_s.shape[1]
    wb = win_buf.shape[1]
    half = 2 * NSA_KV_W
    grid_spec = pltpu.PrefetchScalarGridSpec(
        num_scalar_prefetch=1,
        grid=(db, n_pages + 1),
        in_specs=[
            pl.BlockSpec((None, ts, NSA_W), lambda b, p, pt: (b, 0, 0)),
            pl.BlockSpec((None, ts, NSA_W), lambda b, p, pt: (b, 0, 0)),
            pl.BlockSpec((None, nbp, half), lambda b, p, pt: (b, 0, 0)),
            pl.BlockSpec((None, PAGE_SIZE, half),
                         lambda b, p, pt: (pt[b * n_pages + jnp.minimum(p, n_pages - 1)], 0, 1)),
            pl.BlockSpec((None, PAGE_SIZE, half), lambda b, p, pt: (b, 0, 1)),
            pl.BlockSpec((None, wb, half), lambda b, p, pt: (b, 0, 0)),
            pl.BlockSpec((None, PAGE_SIZE, half), lambda b, p, pt: (b, 0, 0)),
            pl.BlockSpec((None, ts, 2 * LANES), lambda b, p, pt: (b, 0, 0)),
        ],
        out_specs=[pl.BlockSpec((None, ts, NSA_W), lambda b, p, pt: (b, 0, 0)),
                   pl.BlockSpec((None, wb, half), lambda b, p, pt: (b, 0, 0))],
        scratch_shapes=[pltpu.VMEM((NSA_KV_HEADS, rows, HEAD_DIM), BF16),
                        pltpu.VMEM((NSA_KV_HEADS, rows, HEAD_DIM), F32),
                        pltpu.VMEM((NSA_KV_HEADS, rows, HEAD_DIM), F32),
                        pltpu.VMEM((NSA_KV_HEADS, rows, nbp), BF16),
                        pltpu.VMEM((NSA_KV_HEADS, rows, 1), F32),
                        pltpu.VMEM((NSA_KV_HEADS, rows, 1), F32),
                        pltpu.VMEM((NSA_KV_HEADS, rows, HEAD_DIM), F32)],
    )
    return pl.pallas_call(
        functools.partial(_nsa_decode_body, n_pages=n_pages, ts=ts, past=past, nbp=nbp, wb=wb),
        grid_spec=grid_spec,
        out_shape=[jax.ShapeDtypeStruct((db, ts, NSA_W), F32),
                   jax.ShapeDtypeStruct((db, wb, half), F32)],
        compiler_params=_params("parallel", "arbitrary"),
    )(pt_flat, qn_s, qr_s, cmp_s, cache_nsa, new_pad, win_buf, wnew_pad, gates_s)


def _topk_rows(x, k):
    n = x.shape[0]
    row = _iota(x.shape, 0).astype(F32)
    vals, idxs = [], []
    for _ in range(k):
        mx = jnp.max(x, axis=0, keepdims=True)
        ix = jnp.min(jnp.where(x == mx, row, float(n)), axis=0, keepdims=True)
        x = jnp.where(row == ix, NEG_INF, x)
        vals.append(mx)
        idxs.append(ix)
    return jnp.concatenate(vals, axis=0), jnp.concatenate(idxs, axis=0)


def _peer_select_body(q_ref, keys_ref, a_ref, b_ref, g_ref):
    tm = q_ref.shape[0]
    kk = PEER_TOPK
    a_all, b_all, g_all = [], [], []
    for h in range(PEER_HEADS):
        sub = []
        for c in range(2):
            keys = keys_ref[h, c].astype(BF16)
            col0 = (h * 2 + c) * PEER_HALF
            sub.append(_topk_rows(_dot_nt(keys, q_ref[:, col0:col0 + PEER_HALF]), kk))
        (s1, i1), (s2, i2) = sub
        cand = jnp.concatenate([s1[i:i + 1] + s2 for i in range(kk)], axis=0)
        ca = jnp.concatenate([jnp.broadcast_to(i1[i:i + 1], (kk, tm)) for i in range(kk)], axis=0)
        cb = jnp.concatenate([i2] * kk, axis=0)
        row = _iota(cand.shape, 0).astype(F32)
        tops, a_h, b_h = [], [], []
        for _ in range(kk):
            mx = jnp.max(cand, axis=0, keepdims=True)
            ix = jnp.min(jnp.where(cand == mx, row, float(kk * kk)), axis=0, keepdims=True)
            hit = row == ix
            a_h.append(jnp.sum(jnp.where(hit, ca, 0.0), axis=0, keepdims=True))
            b_h.append(jnp.sum(jnp.where(hit, cb, 0.0), axis=0, keepdims=True))
            cand = jnp.where(hit, NEG_INF, cand)
            tops.append(mx)
        top = jnp.concatenate(tops, axis=0)
        e = jnp.exp(top - top[0:1])
        g_all.append(e / jnp.sum(e, axis=0, keepdims=True))
        a_all.append(jnp.concatenate(a_h, axis=0))
        b_all.append(jnp.concatenate(b_h, axis=0))
    a_ref[...] = jnp.concatenate(a_all, axis=0).T
    b_ref[...] = jnp.concatenate(b_all, axis=0).T
    g_ref[...] = jnp.concatenate(g_all, axis=0).T


def _peer_select(q, peer_keys):
    n = q.shape[0]
    tm = _tile(n, LANES, LANES)
    slots = PEER_HEADS * PEER_TOPK
    out = jax.ShapeDtypeStruct((n, slots), F32)
    spec = pl.BlockSpec((tm, slots), lambda i: (i, 0))
    return pl.pallas_call(
        _peer_select_body,
        grid=(n // tm,),
        in_specs=[pl.BlockSpec((tm, q.shape[1]), lambda i: (i, 0)),
                  pl.BlockSpec(peer_keys.shape, lambda i: (0, 0, 0, 0))],
        out_specs=[spec, spec, spec],
        out_shape=[out, out, out],
        compiler_params=_params("parallel"),
    )(q, peer_keys)


def _peer_gate_body(a_ref, b_ref, g_ref, o_ref):
    tt = a_ref.shape[0]
    slots = a_ref.shape[1]
    sub = _iota((PEER_KEYS, slots), 0).astype(F32)
    for t in range(tt):
        a_row = a_ref[t:t + 1, :]
        b_row = b_ref[t:t + 1, :]
        g_row = g_ref[t:t + 1, :]
        a_hot = jnp.where(sub == a_row, 1.0, 0.0).astype(BF16)
        b_val = jnp.where(sub == b_row, g_row, 0.0)
        hi = b_val.astype(BF16)
        lo = (b_val - hi.astype(F32)).astype(BF16)
        o_ref[:, t, :] = _dot_nt(a_hot, hi) + _dot_nt(a_hot, lo)


def _peer_gate(a_sel, b_sel, g_sel):
    n, slots = a_sel.shape
    tt = _tile(n, 16)
    spec = pl.BlockSpec((tt, slots), lambda i: (i, 0))
    return pl.pallas_call(
        _peer_gate_body,
        grid=(n // tt,),
        in_specs=[spec, spec, spec],
        out_specs=pl.BlockSpec((PEER_KEYS, tt, PEER_KEYS), lambda i: (0, i, 0)),
        out_shape=jax.ShapeDtypeStruct((PEER_KEYS, n, PEER_KEYS), F32),
        compiler_params=_params("parallel"),
    )(a_sel, b_sel, g_sel)


def _peer_out_body(gate_ref, act_ref, v_ref, x_ref, o_ref, *, na):
    a = pl.program_id(2)

    @pl.when(a == 0)
    def _():
        o_ref[...] = x_ref[...]

    coef = jnp.concatenate([(gate_ref[j] * act_ref[j]).astype(BF16) for j in range(na)], axis=1)
    o_ref[...] += _dot(coef, v_ref[...].astype(BF16))


def _peer_out(gate, act, peer_v, x):
    n, d = x.shape
    na = 2
    tm = _tile(n, 640)
    tn = _tile(d, 2048, LANES)
    ga_spec = pl.BlockSpec((na, tm, PEER_KEYS), lambda i, j, a: (a, i, 0))
    return pl.pallas_call(
        functools.partial(_peer_out_body, na=na),
        grid=(n // tm, d // tn, PEER_KEYS // na),
        in_specs=[ga_spec, ga_spec,
                  pl.BlockSpec((na * PEER_KEYS, tn), lambda i, j, a: (a, j)),
                  pl.BlockSpec((tm, tn), lambda i, j, a: (i, j))],
        out_specs=pl.BlockSpec((tm, tn), lambda i, j, a: (i, j)),
        out_shape=jax.ShapeDtypeStruct((n, d), F32),
        compiler_params=_params("parallel", "parallel", "arbitrary"),
    )(gate, act, peer_v, x)


def _rope_tables(pos):
    half = HEAD_DIM // 2
    inv = ROPE_THETA ** (-jnp.arange(half, dtype=F32) / half)
    ang = pos.astype(F32)[:, None] * inv[None, :]
    cos, sin = jnp.cos(ang), jnp.sin(ang)
    return jnp.concatenate([cos, cos], axis=-1), jnp.concatenate([-sin, sin], axis=-1)


def _split_w_in(w_in):
    d = w_in.shape[0]
    o_f = FOX_W + 2 * FOX_KV_W
    o_q = o_f + FOX_HEADS
    o_g = o_q + NSA_W + 6 * NSA_KV_W
    main = jnp.concatenate([w_in[:, :o_f], w_in[:, o_q:o_g]], axis=1)
    forget = jnp.pad(w_in[:, o_f:o_q], ((0, 0), (0, LANES - FOX_HEADS)))
    gates = w_in[:, o_g:].reshape(d, NSA_KV_HEADS, NSA_GROUP, NSA_BRANCHES)
    gates = jnp.transpose(gates, (0, 1, 3, 2)).reshape(d, NSA_KV_HEADS, NSA_BRANCHES * NSA_GROUP)
    gates = jnp.pad(gates, ((0, 0), (0, 0), (0, LANES - NSA_BRANCHES * NSA_GROUP)))
    return main, jnp.concatenate([forget, gates.reshape(d, NSA_KV_HEADS * LANES)], axis=1)


def _pad_rows(a, rows):
    return jnp.pad(a, ((0, 0), (0, rows - a.shape[1]), (0, 0)))


def _layer(x, p_ple, cache_fox_kv, cache_fox_logf, cache_nsa_kv, state_win, pt_flat, dims,
           g_mix, w_in, b_f, w_ck, w_cv, w_a, w_b, w_merge, b_merge, w_o,
           g_ffn, w_pq, peer_keys, peer_u, peer_v, g_ple, w_ple_gate, w_ple):
    b, t, db, ts, n_pages = dims
    n, d = x.shape
    n_p = b * t
    past = n_pages * PAGE_SIZE

    h = _rmsnorm(x, g_mix, BF16)
    w_main, w_small = _split_w_in(w_in)
    proj = _matmul(h, w_main, tm=1664, tn=512, tk=2048, out_dtype=F32)
    proj2 = _matmul(h, w_small, tm=1664, tn=384, tk=2048, out_dtype=F32)
    pos = jnp.concatenate([jnp.tile(jnp.arange(t), b), jnp.tile(past + jnp.arange(ts), db)])
    cos, sin = _rope_tables(pos)
    bf_row = jnp.pad(b_f, (0, LANES - FOX_HEADS)).reshape(1, LANES)
    qf, fkv, qn, qr, nkv, win, logf, gates = _post_projection(proj, proj2, cos, sin, bf_row)

    lf_p_t = jnp.transpose(logf[:n_p].reshape(b, t, FOX_HEADS), (0, 2, 1))
    o_f_p = _fox_prompt(qf, fkv, _cumsum_prompt(lf_p_t), b, t)
    w_cmp = jnp.concatenate([jnp.tile(w_ck, (1, NSA_KV_HEADS)), jnp.tile(w_cv, (1, NSA_KV_HEADS))], axis=1)
    cmp_p = _compress_prompt(nkv, w_cmp, n_p)
    o_n_p = _nsa_prompt(qn, qr, cmp_p, nkv, win, gates, b, t)

    sample = lambda a: a[n_p:].reshape(db, ts, a.shape[1])
    lf_new_t = jnp.pad(jnp.transpose(sample(logf), (0, 2, 1)), ((0, 0), (0, 0), (0, PAGE_SIZE - ts)))
    cache_lf_t = jnp.transpose(cache_fox_logf, (0, 2, 1))
    c_all_t = _cumsum_paged(pt_flat, cache_lf_t, lf_new_t, db, n_pages)
    cq_s = jnp.transpose(c_all_t[:, :, past:past + ts], (0, 2, 1))
    o_f_s = _fox_decode(pt_flat, sample(qf).astype(F32),cache_fox_kv.reshape(-1, PAGE_SIZE, 2 * FOX_KV_W),
                        _pad_rows(sample(fkv), PAGE_SIZE), c_all_t, cq_s, db, ts, n_pages)

    cache_nsa = cache_nsa_kv.reshape(-1, PAGE_SIZE, 4 * NSA_KV_W)
    nkv_new = _pad_rows(sample(nkv), PAGE_SIZE)
    cmp_s = _compress_paged(pt_flat, cache_nsa, nkv_new, w_cmp, db, n_pages)
    cmp_s = _pad_rows(cmp_s, -(-cmp_s.shape[1] // LANES) * LANES)
    wb = state_win.shape[1]
    o_n_s, new_win = _nsa_decode(pt_flat, sample(qn).astype(F32), sample(qr).astype(F32),cmp_s, cache_nsa, nkv_new,
                                 state_win.reshape(db, wb, 2 * NSA_KV_W), _pad_rows(sample(win), PAGE_SIZE),
                                 sample(gates), db, ts, n_pages)

    o_f = jnp.concatenate([o_f_p, o_f_s.reshape(db * ts, FOX_W).astype(BF16)], axis=0)
    o_n = jnp.concatenate([o_n_p, o_n_s.reshape(db * ts, NSA_W).astype(BF16)], axis=0)

    gate = _matmul(h, w_merge, tm=1664, tn=512, tk=2048, out_dtype=F32,
                   epilogue=lambda acc, bias: jax.nn.sigmoid(acc + bias), extra=[(b_merge.reshape(1, -1), "row", 0)])
    m_a = _matmul(o_f, w_a, tm=1664, tn=512, tk=2048, out_dtype=F32,
                  epilogue=lambda acc, ga: ga * acc, extra=[(gate, "tile", 0)])
    tn_b = _tile(d, 512, LANES)
    mixed = _matmul(o_n, w_b, tm=1664, tn=512, tk=2048, out_dtype=BF16,
                    epilogue=lambda acc, gb, ma: ma + gb * acc,
                    extra=[(gate, "tile", d // tn_b), (m_a, "tile", 0)])
    x1 = _matmul(mixed, w_o, tm=1664, tn=512, tk=2048, out_dtype=F32,
                 epilogue=lambda acc, res: res + acc, extra=[(x, "tile", 0)])

    h2 = _rmsnorm(x1, g_ffn, BF16)
    pq = _matmul(h2, w_pq, tm=1664, tn=512, tk=2048, out_dtype=BF16)
    a_sel, b_sel, g_sel = _peer_select(pq, peer_keys)
    gate_e = _peer_gate(a_sel, b_sel, g_sel)
    act = _matmul(h2, peer_u, tm=1664, tn=256, tk=d, out_dtype=F32, nt=True, split_out=True,
                  epilogue=_gelu_tanh)
    x2 = _peer_out(gate_e, act, peer_v, x1)

    h3 = _rmsnorm(x2, g_ple, BF16)
    pw = _matmul(p_ple.astype(BF16), w_ple, tm=1664, tn=512, tk=2048, out_dtype=F32)
    x3 = _matmul(h3, w_ple_gate, tm=1664, tn=512, tk=2048, out_dtype=F32,
                 epilogue=lambda acc, res, pe: res + jax.nn.sigmoid(acc) * pe,
                 extra=[(x2, "tile", 0), (pw, "tile", 0)])

    caches = dict(
        fox_kv_p=fkv[:n_p].reshape(b, t, 2, FOX_KV_HEADS, HEAD_DIM),
        fox_lf_p=logf[:n_p].reshape(b, t, FOX_HEADS),
        nsa_kv_p=nkv[:n_p].reshape(b, t, 4, NSA_KV_HEADS, HEAD_DIM),
        nsa_win_p=win[:n_p].reshape(b, t, 2, NSA_KV_HEADS, HEAD_DIM)[:, t - min(NSA_WINDOW, t):],
        fox_kv_s=fkv[n_p:].reshape(db, ts, 2, FOX_KV_HEADS, HEAD_DIM),
        fox_lf_s=logf[n_p:].reshape(db, ts, FOX_HEADS),
        nsa_kv_s=nkv[n_p:].reshape(db, ts, 4, NSA_KV_HEADS, HEAD_DIM),
        nsa_win_s=new_win.reshape(db, wb, 2, NSA_KV_HEADS, HEAD_DIM),
    )
    return x3, caches


def kernel(x_prompt, x_sample, cache_fox_kv, cache_fox_logf, cache_nsa_kv, state_nsa_win, page_table, p_prompt, p_sample, g_mix, w_in, b_f, w_ck, w_cv, w_a, w_b, w_merge, b_merge, w_o, g_ffn, w_pq, peer_keys, peer_u, peer_v, g_ple, w_ple_gate, w_ple, g_final):
    b, t, d = x_prompt.shape
    db, ts, _ = x_sample.shape
    depth = g_mix.shape[0]
    n_pages = page_table.shape[1]
    n_p = b * t
    dims = (b, t, db, ts, n_pages)
    pt_flat = page_table.reshape(-1).astype(jnp.int32)
    x = jnp.concatenate([x_prompt.reshape(n_p, d), x_sample.reshape(db * ts, d)], axis=0)
    per_layer = []
    for i in range(depth):
        p_ple = jnp.concatenate([p_prompt[i].reshape(n_p, -1), p_sample[i].reshape(db * ts, -1)], axis=0)
        x, caches = _layer(x, p_ple, cache_fox_kv[i], cache_fox_logf[i], cache_nsa_kv[i], state_nsa_win[i],
                           pt_flat, dims, g_mix[i], w_in[i], b_f[i], w_ck[i], w_cv[i], w_a[i], w_b[i],
                           w_merge[i], b_merge[i], w_o[i], g_ffn[i], w_pq[i], peer_keys[i], peer_u[i],
                           peer_v[i], g_ple[i], w_ple_gate[i], w_ple[i])
        per_layer.append(caches)
    y = _rmsnorm(x, g_final, F32)
    stack = lambda name: jnp.stack([c[name] for c in per_layer])
    return (y[:n_p].reshape(b, t, d), y[n_p:].reshape(db, ts, d),
            stack("fox_kv_p"), stack("fox_lf_p"), stack("nsa_kv_p"), stack("nsa_win_p"),
            stack("fox_kv_s"), stack("fox_lf_s"), stack("nsa_kv_s"), stack("nsa_win_s"))
```

```python
import functools

import jax
import jax.numpy as jnp
from jax import lax
from jax.experimental import pallas as pl
from jax.experimental.pallas import tpu as pltpu

F32 = jnp.float32
BF16 = jnp.bfloat16
NEG_INF = float("-inf")

HEAD_DIM = 128
FOX_HEADS = 16
FOX_KV_HEADS = 4
FOX_GROUP = FOX_HEADS // FOX_KV_HEADS
NSA_HEADS = 16
NSA_KV_HEADS = 2
NSA_GROUP = NSA_HEADS // NSA_KV_HEADS
NSA_BLOCK = 64
BLOCK_SHIFT = NSA_BLOCK.bit_length() - 1
NSA_TOP_N = 16
NSA_WINDOW = 512
NSA_BRANCHES = 3
SEL_FORCE = 1e4
ROPE_THETA = 10000.0
PAGE_SIZE = 128
PEER_HEADS = 8
PEER_KEYS = 128
PEER_TOPK = 16
PEER_HALF = 128
RMS_EPS = 1e-6
SCALE = HEAD_DIM ** -0.5
LOG2E = 1.4426950408889634
SCALE2 = SCALE * LOG2E

FOX_W = FOX_HEADS * HEAD_DIM
FOX_KV_W = FOX_KV_HEADS * HEAD_DIM
NSA_W = NSA_HEADS * HEAD_DIM
NSA_KV_W = NSA_KV_HEADS * HEAD_DIM
LANES = 128
VMEM_LIMIT = 56 * 1024 * 1024
PAGES_PER_STEP = 8
PAGE_SLABS = 8
WIN_SLABS = 2 * NSA_KV_HEADS

NT_DIMS = (((1,), (1,)), ((), ()))


def _params(*sem):
    return pltpu.CompilerParams(dimension_semantics=sem, vmem_limit_bytes=VMEM_LIMIT)


def _tile(n, target, align=8):
    if n <= target:
        return n
    best = None
    for t in range(align, target + 1, align):
        if n % t == 0:
            best = t
    assert best is not None, (n, target, align)
    return best


def _dot(a, b):
    return jnp.dot(a, b, preferred_element_type=F32)


def _dot_nt(a, b):
    return lax.dot_general(a, b, NT_DIMS, preferred_element_type=F32)


def _dot_tn(a, b):
    k = a.shape[0]
    kp = -(-k // LANES) * LANES
    if kp != k:
        a = jnp.concatenate([a, jnp.zeros((kp - k, a.shape[1]), a.dtype)], axis=0)
        b = jnp.concatenate([b, jnp.zeros((kp - k, b.shape[1]), b.dtype)], axis=0)
    return _dot(a.T.astype(BF16), b.astype(BF16))


def _iota(shape, dim, dtype=jnp.int32):
    return lax.broadcasted_iota(dtype, shape, dim)


def _rmsnorm_body(x_ref, g_ref, o_ref):
    x = x_ref[...]
    y = x * lax.rsqrt(jnp.mean(x * x, axis=-1, keepdims=True) + RMS_EPS)
    o_ref[...] = (y * g_ref[...]).astype(o_ref.dtype)


def _rmsnorm(x, g, out_dtype):
    n, d = x.shape
    tr = _tile(n, 256)
    return pl.pallas_call(
        _rmsnorm_body,
        grid=(n // tr,),
        in_specs=[pl.BlockSpec((tr, d), lambda i: (i, 0)), pl.BlockSpec((1, d), lambda i: (0, 0))],
        out_specs=pl.BlockSpec((tr, d), lambda i: (i, 0)),
        out_shape=jax.ShapeDtypeStruct((n, d), out_dtype),
        compiler_params=_params("parallel"),
        name="rmsnorm",
    )(x, g.reshape(1, d))


def _gelu_tanh(x):
    return 0.5 * x * (1.0 + jnp.tanh(0.7978845608028654 * (x + 0.044715 * (x * x * x))))


def _mm_body(*refs, n_extra, epilogue, nt, nk, split_out, stacked_lhs):
    x_ref, w_ref = refs[0], refs[1]
    extra = refs[2:2 + n_extra]
    o_ref = refs[2 + n_extra]
    w = w_ref[...].astype(BF16)
    if stacked_lhs:
        x = jnp.concatenate([x_ref[j] for j in range(x_ref.shape[0])], axis=1)
    else:
        x = x_ref[...]
    part = _dot_nt(x, w) if nt else _dot(x, w)

    def finish(acc):
        res = epilogue(acc, *[e[...] for e in extra]).astype(o_ref.dtype)
        if split_out:
            for j in range(o_ref.shape[0]):
                o_ref[j] = res[:, j * LANES:(j + 1) * LANES]
        else:
            o_ref[...] = res

    if nk == 1:
        finish(part)
        return
    acc_ref = refs[3 + n_extra]
    k = pl.program_id(2)

    @pl.when(k == 0)
    def _():
        acc_ref[...] = part

    @pl.when(k > 0)
    def _():
        acc_ref[...] += part

    @pl.when(k == nk - 1)
    def _():
        finish(acc_ref[...])


def _matmul(x, w, *, tm, tn, tk, out_dtype, name, epilogue=None, extra=(), nt=False, split_out=False,
            stacked_lhs=False):
    if stacked_lhs:
        m, kdim = x.shape[1], x.shape[0] * LANES
    else:
        m, kdim = x.shape
    n = w.shape[0] if nt else w.shape[1]
    tm, tn, tk = _tile(m, tm), _tile(n, tn, LANES), _tile(kdim, tk, LANES)
    nk = kdim // tk
    if epilogue is None:
        epilogue = lambda acc: acc
    if stacked_lhs:
        x_spec = pl.BlockSpec((tk // LANES, tm, LANES), lambda i, j, k: (k, i, 0))
    else:
        x_spec = pl.BlockSpec((tm, tk), lambda i, j, k: (i, k))
    in_specs = [
        x_spec,
        pl.BlockSpec((tn, tk), lambda i, j, k: (j, k)) if nt else pl.BlockSpec((tk, tn), lambda i, j, k: (k, j)),
    ]
    args = [x, w]
    for arr, kind, off in extra:
        if kind == "tile":
            in_specs.append(pl.BlockSpec((tm, tn), lambda i, j, k, off=off: (i, j + off)))
        else:
            in_specs.append(pl.BlockSpec((1, tn), lambda i, j, k, off=off: (0, j + off)))
        args.append(arr)
    if split_out:
        out_shape = jax.ShapeDtypeStruct((n // LANES, m, LANES), out_dtype)
        out_spec = pl.BlockSpec((tn // LANES, tm, LANES), lambda i, j, k: (j, i, 0))
    else:
        out_shape = jax.ShapeDtypeStruct((m, n), out_dtype)
        out_spec = pl.BlockSpec((tm, tn), lambda i, j, k: (i, j))
    body = functools.partial(_mm_body, n_extra=len(extra), epilogue=epilogue, nt=nt, nk=nk,
                             split_out=split_out, stacked_lhs=stacked_lhs)
    return pl.pallas_call(
        body,
        grid=(m // tm, n // tn, nk),
        in_specs=in_specs,
        out_specs=out_spec,
        out_shape=out_shape,
        scratch_shapes=[pltpu.VMEM((tm, tn), F32)] if nk > 1 else [],
        compiler_params=_params("parallel", "parallel", "arbitrary"),
        name=name,
    )(*args)


def _rope(x, cos, sin_signed):
    return x * cos + pltpu.roll(x, HEAD_DIM // 2, 1) * sin_signed


def _log_sigmoid(z):
    return jnp.minimum(z, 0.0) - jnp.log1p(jnp.exp(-jnp.abs(z)))


def _post_body(p_ref, p2_ref, cos_ref, sin_ref, bf_ref,
               qf_ref, fkv_ref, qn_ref, qr_ref, nkv_ref, win_ref, logf_ref, gate_ref):
    cos = cos_ref[...]
    sin = sin_ref[...]
    o = 0
    qf_ref[...] = p_ref[:, o:o + FOX_W].astype(BF16)
    o += FOX_W
    fkv_ref[...] = p_ref[:, o:o + 2 * FOX_KV_W]
    o += 2 * FOX_KV_W
    for h in range(NSA_HEADS):
        q = p_ref[:, o + h * HEAD_DIM:o + (h + 1) * HEAD_DIM]
        qn_ref[:, h * HEAD_DIM:(h + 1) * HEAD_DIM] = q.astype(BF16)
        qr_ref[:, h * HEAD_DIM:(h + 1) * HEAD_DIM] = _rope(q, cos, sin).astype(BF16)
    o += NSA_W
    nkv_ref[:, 0:2 * NSA_KV_W] = p_ref[:, o:o + 2 * NSA_KV_W]
    o += 2 * NSA_KV_W
    for g in range(NSA_KV_HEADS):
        k = p_ref[:, o + g * HEAD_DIM:o + (g + 1) * HEAD_DIM]
        nkv_ref[:, 2 * NSA_KV_W + g * HEAD_DIM:2 * NSA_KV_W + (g + 1) * HEAD_DIM] = _rope(k, cos, sin)
    o += NSA_KV_W
    nkv_ref[:, 3 * NSA_KV_W:4 * NSA_KV_W] = p_ref[:, o:o + NSA_KV_W]
    o += NSA_KV_W
    for g in range(NSA_KV_HEADS):
        k = p_ref[:, o + g * HEAD_DIM:o + (g + 1) * HEAD_DIM]
        win_ref[:, g * HEAD_DIM:(g + 1) * HEAD_DIM] = _rope(k, cos, sin)
    o += NSA_KV_W
    win_ref[:, NSA_KV_W:2 * NSA_KV_W] = p_ref[:, o:o + NSA_KV_W]
    logf = _log_sigmoid(p2_ref[:, 0:LANES] + bf_ref[...])
    logf_ref[...] = logf[:, 0:FOX_HEADS]
    gate_ref[...] = jax.nn.sigmoid(p2_ref[:, LANES:3 * LANES])


def _post_projection(p, p2, cos, sin, bf_row):
    n = p.shape[0]
    tr = _tile(n, 256)
    row = lambda w: pl.BlockSpec((tr, w), lambda i: (i, 0))
    out_shapes = [
        jax.ShapeDtypeStruct((n, FOX_W), BF16),
        jax.ShapeDtypeStruct((n, 2 * FOX_KV_W), F32),
        jax.ShapeDtypeStruct((n, NSA_W), BF16),
        jax.ShapeDtypeStruct((n, NSA_W), BF16),
        jax.ShapeDtypeStruct((n, 4 * NSA_KV_W), F32),
        jax.ShapeDtypeStruct((n, 2 * NSA_KV_W), F32),
        jax.ShapeDtypeStruct((n, FOX_HEADS), F32),
        jax.ShapeDtypeStruct((n, 2 * LANES), F32),
    ]
    return pl.pallas_call(
        _post_body,
        grid=(n // tr,),
        in_specs=[row(p.shape[1]), row(p2.shape[1]), row(LANES), row(LANES),
                  pl.BlockSpec((1, LANES), lambda i: (0, 0))],
        out_specs=[row(s.shape[1]) for s in out_shapes],
        out_shape=out_shapes,
        compiler_params=_params("parallel"),
        name="post_projection",
    )(p, p2, cos, sin, bf_row)


def _split3(x):
    hi = x.astype(BF16)
    r1 = x - hi.astype(F32)
    mid = r1.astype(BF16)
    lo = (r1 - mid.astype(F32)).astype(BF16)
    return hi, mid, lo


def _row_cumsum(x, u):
    hi, mid, lo = _split3(x)
    return (_dot(hi, u) + _dot(mid, u)) + _dot(lo, u)


def _cumsum_prompt_body(lf_ref, u_ref, o_ref, carry_ref):
    @pl.when(pl.program_id(1) == 0)
    def _():
        carry_ref[...] = jnp.zeros_like(carry_ref)

    c = _row_cumsum(lf_ref[...], u_ref[...]) + carry_ref[...]
    o_ref[...] = c
    r = c.shape[1]
    carry_ref[...] = c[:, r - 1:r]


def _upper_ones(r):
    return (jnp.arange(r)[:, None] <= jnp.arange(r)[None, :]).astype(BF16)


def _cumsum_prompt(lf_t):
    b, h, t = lf_t.shape
    r = _tile(t, 512, LANES)
    return pl.pallas_call(
        _cumsum_prompt_body,
        grid=(b, t // r),
        in_specs=[pl.BlockSpec((None, h, r), lambda i, j: (i, 0, j)),
                  pl.BlockSpec((r, r), lambda i, j: (0, 0))],
        out_specs=pl.BlockSpec((None, h, r), lambda i, j: (i, 0, j)),
        out_shape=jax.ShapeDtypeStruct((b, h, t), F32),
        scratch_shapes=[pltpu.VMEM((h, 1), F32)],
        compiler_params=_params("parallel", "arbitrary"),
        name="cumsum_prompt",
    )(lf_t, _upper_ones(r))


def _page_cumsum_body(x_ref, u_ref, o_ref):
    o_ref[...] = _row_cumsum(x_ref[...], u_ref[...])


def _page_cumsum(x):
    rows = x.shape[0]
    tr = _tile(rows, 2048)
    return pl.pallas_call(
        _page_cumsum_body,
        grid=(rows // tr,),
        in_specs=[pl.BlockSpec((tr, PAGE_SIZE), lambda i: (i, 0)),
                  pl.BlockSpec((PAGE_SIZE, PAGE_SIZE), lambda i: (0, 0))],
        out_specs=pl.BlockSpec((tr, PAGE_SIZE), lambda i: (i, 0)),
        out_shape=jax.ShapeDtypeStruct((rows, PAGE_SIZE), F32),
        compiler_params=_params("parallel"),
        name="page_cumsum",
    )(x, _upper_ones(PAGE_SIZE))


def _online_update_t(s, vt, m_ref, l_ref, acc_ref, guard):
    m_old = m_ref[...]
    m_new = jnp.maximum(m_old, jnp.max(s, axis=0, keepdims=True))
    m_use = jnp.where(m_new == NEG_INF, 0.0, m_new) if guard else m_new
    alpha = jnp.exp2(m_old - m_use)
    p = jnp.exp2(s - m_use)
    l_ref[...] = alpha * l_ref[...] + jnp.sum(p, axis=0, keepdims=True)
    acc_ref[...] = alpha * acc_ref[...] + _dot(vt, p.astype(BF16))
    m_ref[...] = m_new


def _online_update(s, pv_fn, m_ref, l_ref, acc_ref, guard):
    m_old = m_ref[...]
    m_new = jnp.maximum(m_old, jnp.max(s, axis=-1, keepdims=True))
    m_use = jnp.where(m_new == NEG_INF, 0.0, m_new) if guard else m_new
    alpha = jnp.exp2(m_old - m_use)
    p = jnp.exp2(s - m_use)
    l_ref[...] = alpha * l_ref[...] + jnp.sum(p, axis=-1, keepdims=True)
    acc_ref[...] = alpha * acc_ref[...] + pv_fn(p.astype(BF16))
    m_ref[...] = m_new


def _masked_softmax(s, keep, axis):
    s = jnp.where(keep > 0.5, s, NEG_INF)
    m = jnp.max(s, axis=axis, keepdims=True)
    m = jnp.where(m == NEG_INF, 0.0, m)
    e = jnp.exp2(s - m)
    return e / jnp.maximum(jnp.sum(e, axis=axis, keepdims=True), 1e-30)


def _normalize(acc, l):
    return acc / jnp.maximum(l, 1e-30)


def _init_state(*triples):
    for m_ref, l_ref, acc_ref in triples:
        m_ref[...] = jnp.full_like(m_ref, NEG_INF)
        l_ref[...] = jnp.zeros_like(l_ref)
        acc_ref[...] = jnp.zeros_like(acc_ref)


def _mask_heads(s, keep, n_heads, width):
    return jnp.concatenate(
        [jnp.where(keep, s[:, r * width:(r + 1) * width], NEG_INF) for r in range(n_heads)], axis=1)


def _fox_prompt_body(q_ref, qx_ref, k_ref, kx_ref, v_ref, o_ref,
                     qa_ref, ka_ref, vt_ref, m_ref, l_ref, acc_ref, *, tq, nq):
    i = pl.program_id(2)
    rr = FOX_GROUP

    @pl.when(i == 0)
    def _():
        def prep(j, carry):
            ks = pl.multiple_of(j * tq, tq)
            ka_ref[j, :, 0:HEAD_DIM] = k_ref[pl.ds(ks, tq), :].astype(BF16)
            ka_ref[j, :, HEAD_DIM:2 * HEAD_DIM] = kx_ref[pl.ds(ks, tq), :]
            vt_ref[j] = v_ref[pl.ds(ks, tq), :].T.astype(BF16)
            return carry

        lax.fori_loop(0, nq, prep, 0)

    for r in range(rr):
        qa_ref[r * tq:(r + 1) * tq, 0:HEAD_DIM] = q_ref[:, r * HEAD_DIM:(r + 1) * HEAD_DIM]
        qa_ref[r * tq:(r + 1) * tq, HEAD_DIM:2 * HEAD_DIM] = qx_ref[r]
    _init_state((m_ref, l_ref, acc_ref))

    def full_chunk(j, carry):
        s = _dot_nt(ka_ref[j], qa_ref[...]) * SCALE2
        _online_update_t(s, vt_ref[j], m_ref, l_ref, acc_ref, guard=False)
        return carry

    lax.fori_loop(0, i, full_chunk, 0)
    causal = _iota((tq, tq), 0) <= _iota((tq, tq), 1)
    s = _mask_heads(_dot_nt(ka_ref[i], qa_ref[...]) * SCALE2, causal, rr, tq)
    _online_update_t(s, vt_ref[i], m_ref, l_ref, acc_ref, guard=False)

    o_t = _normalize(acc_ref[...], l_ref[...])
    for r in range(rr):
        o_ref[:, r * HEAD_DIM:(r + 1) * HEAD_DIM] = o_t[:, r * tq:(r + 1) * tq].T.astype(o_ref.dtype)


def _fox_bias_columns(c_t, b, t):
    g_, r_ = FOX_KV_HEADS, FOX_GROUP
    pieces = jnp.stack(_split3(c_t * (1.0 / SCALE)), axis=-1).reshape(b, g_, r_, t, 3)
    kx = -jnp.transpose(pieces, (0, 1, 3, 2, 4)).reshape(b, g_, t, 3 * r_)
    kx = jnp.concatenate([kx, jnp.ones((b, g_, t, 3), BF16)], axis=-1)
    kx = jnp.pad(kx, ((0, 0), (0, 0), (0, 0), (0, LANES - kx.shape[-1])))
    own = (jnp.arange(r_)[:, None] == jnp.arange(3 * r_)[None, :] // 3).astype(BF16)
    own = jnp.broadcast_to(own[None, None, :, None, :], (b, g_, r_, t, 3 * r_))
    qx = jnp.concatenate([own, pieces], axis=-1)
    qx = jnp.pad(qx, ((0, 0), (0, 0), (0, 0), (0, 0), (0, LANES - qx.shape[-1])))
    return qx, kx


def _fox_prompt(qf, fkv, c_t, b, t):
    tq = _tile(t, 256, LANES)
    nq = t // tq
    g_, r_ = FOX_KV_HEADS, FOX_GROUP
    rows = r_ * tq
    qx, kx = _fox_bias_columns(c_t, b, t)
    return pl.pallas_call(
        functools.partial(_fox_prompt_body, tq=tq, nq=nq),
        grid=(b, g_, nq),
        in_specs=[
            pl.BlockSpec((tq, r_ * HEAD_DIM), lambda bi, g, i: (bi * nq + i, g)),
            pl.BlockSpec((None, None, r_, tq, LANES), lambda bi, g, i: (bi, g, 0, i, 0)),
            pl.BlockSpec((t, HEAD_DIM), lambda bi, g, i: (bi, g)),
            pl.BlockSpec((None, None, t, LANES), lambda bi, g, i: (bi, g, 0, 0)),
            pl.BlockSpec((t, HEAD_DIM), lambda bi, g, i: (bi, g_ + g)),
        ],
        out_specs=pl.BlockSpec((tq, r_ * HEAD_DIM), lambda bi, g, i: (bi * nq + i, g)),
        out_shape=jax.ShapeDtypeStruct((b * t, FOX_W), BF16),
        scratch_shapes=[pltpu.VMEM((rows, 2 * HEAD_DIM), BF16),
                        pltpu.VMEM((nq, tq, 2 * HEAD_DIM), BF16),
                        pltpu.VMEM((nq, HEAD_DIM, tq), BF16),
                        pltpu.VMEM((1, rows), F32), pltpu.VMEM((1, rows), F32),
                        pltpu.VMEM((HEAD_DIM, rows), F32)],
        compiler_params=_params("parallel", "parallel", "arbitrary"),
        name="fox_prompt",
    )(qf, qx, fkv, kx, fkv)


def _compress_body(x_ref, w_ref, o_ref):
    x = x_ref[...]
    nb = x.shape[0] // NSA_BLOCK
    xb = x.reshape(nb, NSA_BLOCK, x.shape[1]) * w_ref[...][None]
    o_ref[...] = jnp.sum(xb, axis=1) * (1.0 / NSA_BLOCK)


def _compress_rows(nkv, w_cmp, n_rows):
    width = 2 * NSA_KV_W
    tr = _tile(n_rows, 512, NSA_BLOCK * 8)
    return pl.pallas_call(
        _compress_body,
        grid=(n_rows // tr,),
        in_specs=[pl.BlockSpec((tr, width), lambda i: (i, 0)),
                  pl.BlockSpec((NSA_BLOCK, width), lambda i: (0, 0))],
        out_specs=pl.BlockSpec((tr // NSA_BLOCK, width), lambda i: (i, 0)),
        out_shape=jax.ShapeDtypeStruct((n_rows // NSA_BLOCK, width), F32),
        compiler_params=_params("parallel"),
        name="compress_rows",
    )(nkv, w_cmp)


def _compress_paged_body(pt_ref, *refs, pp):
    page_refs, w_ref, o_ref = refs[:pp], refs[pp], refs[pp + 1]
    nb = PAGE_SIZE // NSA_BLOCK
    for u in range(pp):
        for j in range(2 * NSA_KV_HEADS):
            x = _slab(page_refs[u], j, PAGE_SLABS, PAGE_SIZE)
            xb = x.reshape(nb, NSA_BLOCK, HEAD_DIM) * w_ref[:, j * HEAD_DIM:(j + 1) * HEAD_DIM][None]
            o_ref[u, :, j * HEAD_DIM:(j + 1) * HEAD_DIM] = jnp.sum(xb, axis=1) * (1.0 / NSA_BLOCK)


def _paged_specs(block, n_pages, pp):
    return [pl.BlockSpec((None,) + block, lambda b, p, pt, u=u: (pt[b * n_pages + p * pp + u], 0, 0))
            for u in range(pp)]


def _slab(ref, j, n_slabs, n_rows):
    return ref[pl.ds(j, n_rows, stride=n_slabs), :]


def _compress_paged(pt_flat, cache_nsa, w_cmp, db, n_pages, pp):
    width = 2 * NSA_KV_W
    nb = PAGE_SIZE // NSA_BLOCK
    grid_spec = pltpu.PrefetchScalarGridSpec(
        num_scalar_prefetch=1,
        grid=(db, n_pages // pp),
        in_specs=_paged_specs((PAGE_SIZE * PAGE_SLABS, HEAD_DIM), n_pages, pp)
        + [pl.BlockSpec((NSA_BLOCK, width), lambda b, p, pt: (0, 0))],
        out_specs=pl.BlockSpec((None, pp, nb, width), lambda b, p, pt: (b, p, 0, 0)),
    )
    out = pl.pallas_call(
        functools.partial(_compress_paged_body, pp=pp),
        grid_spec=grid_spec,
        out_shape=jax.ShapeDtypeStruct((db, n_pages, nb, width), F32),
        compiler_params=_params("parallel", "arbitrary"),
        name="compress_paged",
    )(pt_flat, *([cache_nsa] * pp), w_cmp)
    return out.reshape(db, n_pages * nb, width)


def _select_blocks(score, n_sel, axis):
    axis = axis % score.ndim
    nb = score.shape[axis]
    blk = _iota(score.shape, axis).astype(F32)
    sel = jnp.zeros(score.shape, F32)
    work = score
    for _ in range(n_sel):
        mx = jnp.max(work, axis=axis, keepdims=True)
        ix = jnp.min(jnp.where(work == mx, blk, float(nb)), axis=axis, keepdims=True)
        hit = blk == ix
        sel = jnp.where(hit, jnp.where(mx >= 0.0, 1.0, 0.0), sel)
        work = jnp.where(hit, NEG_INF, work)
    return sel


def _expand_blocks(sel_b, key_start, n_keys):
    nb = sel_b.shape[1]
    key_blk = lax.shift_right_logical(key_start + _iota((nb, n_keys), 1), BLOCK_SHIFT)
    e = jnp.where(_iota((nb, n_keys), 0) == key_blk, 1.0, 0.0).astype(BF16)
    return _dot(sel_b, e)


def _nsa_prompt_body(qn_ref, qr_ref, kc_ref, vc_ref, ks_ref, vs_ref, kw_ref, vw_ref, g_ref, o_ref,
                     qs_ref, ksb_ref, vst_ref, kwb_ref, vwt_ref, sel_ref, oc_ref,
                     m_s, l_s, acc_s, m_w, l_w, acc_w, *, tq, nq, nb):
    i = pl.program_id(2)
    rr = NSA_GROUP
    bpc = tq // NSA_BLOCK

    @pl.when(i == 0)
    def _():
        def prep(j, carry):
            ks = pl.multiple_of(j * tq, tq)
            ksb_ref[j] = ks_ref[pl.ds(ks, tq), :].astype(BF16)
            vst_ref[j] = vs_ref[pl.ds(ks, tq), :].T.astype(BF16)
            kwb_ref[j] = kw_ref[pl.ds(ks, tq), :].astype(BF16)
            vwt_ref[j] = vw_ref[pl.ds(ks, tq), :].T.astype(BF16)
            return carry

        lax.fori_loop(0, nq, prep, 0)

    stack = lambda ref: jnp.concatenate([ref[:, r * HEAD_DIM:(r + 1) * HEAD_DIM] for r in range(rr)], axis=0)
    lanes = lambda a: jnp.concatenate([a] * rr, axis=1)
    qn = stack(qn_ref)
    qs_ref[...] = stack(qr_ref)
    pos_q = i * tq + _iota((1, tq), 1)
    _init_state((m_s, l_s, acc_s), (m_w, l_w, acc_w))

    blk = _iota((nb, tq), 0)
    valid = (blk + 1) * NSA_BLOCK - 1 <= pos_q
    s_c = _dot_nt(kc_ref[...].astype(BF16), qn) * SCALE2
    p_c = _masked_softmax(s_c, lanes(jnp.where(valid, 1.0, 0.0)), axis=0)
    oc_ref[...] = _dot_tn(vc_ref[...], p_c)
    imp = p_c[:, 0:tq]
    for r in range(1, rr):
        imp = imp + p_c[:, r * tq:(r + 1) * tq]
    cur = lax.shift_right_logical(pos_q, BLOCK_SHIFT)
    score = jnp.where(blk == cur, SEL_FORCE, jnp.where(valid, imp, -1.0))
    sel_ref[...] = _select_blocks(score, min(NSA_TOP_N, nb), axis=0)

    def sel_keep(j):
        return jnp.concatenate(
            [jnp.broadcast_to(sel_ref[pl.ds(j * bpc + u, 1), :], (NSA_BLOCK, tq)) for u in range(bpc)], axis=0)

    def sel_chunk(j, carry):
        s = _mask_heads(_dot_nt(ksb_ref[j], qs_ref[...]) * SCALE2, sel_keep(j) > 0.5, rr, tq)
        _online_update_t(s, vst_ref[j], m_s, l_s, acc_s, guard=True)
        return carry

    lax.fori_loop(0, i, sel_chunk, 0)
    causal = _iota((tq, tq), 0) <= _iota((tq, tq), 1)
    keep = jnp.where(causal, sel_keep(i), 0.0) > 0.5
    s = _mask_heads(_dot_nt(ksb_ref[i], qs_ref[...]) * SCALE2, keep, rr, tq)
    _online_update_t(s, vst_ref[i], m_s, l_s, acc_s, guard=True)

    def win_chunk(j, carry):
        dpos = pos_q - (j * tq + _iota((tq, 1), 0))
        keep = jnp.where(dpos >= 0, jnp.where(dpos < NSA_WINDOW, 1.0, 0.0), 0.0) > 0.5
        s = _mask_heads(_dot_nt(kwb_ref[j], qs_ref[...]) * SCALE2, keep, rr, tq)
        _online_update_t(s, vwt_ref[j], m_w, l_w, acc_w, guard=True)
        return carry

    lax.fori_loop(jnp.maximum(i - (NSA_WINDOW + tq - 1) // tq, 0), i + 1, win_chunk, 0)

    g_t = g_ref[...].T
    o_s = _normalize(acc_s[...], l_s[...])
    o_w = _normalize(acc_w[...], l_w[...])
    for r in range(rr):
        cols = slice(r * tq, (r + 1) * tq)
        gate = lambda br: g_t[br * rr + r:br * rr + r + 1, :]
        o = gate(0) * oc_ref[:, cols] + gate(1) * o_s[:, cols] + gate(2) * o_w[:, cols]
        o_ref[:, r * HEAD_DIM:(r + 1) * HEAD_DIM] = o.T.astype(o_ref.dtype)


def _nsa_prompt(qn, qr, cmp_p, nkv, win, gates, b, t):
    tq = _tile(t, 256, LANES)
    nq = t // tq
    nb = t // NSA_BLOCK
    g_, rows = NSA_KV_HEADS, NSA_GROUP * tq
    qspec = pl.BlockSpec((tq, NSA_GROUP * HEAD_DIM), lambda bi, g, i: (bi * nq + i, g))
    col = lambda off: pl.BlockSpec((t, HEAD_DIM), lambda bi, g, i: (bi, off + g))
    state = [pltpu.VMEM((1, rows), F32), pltpu.VMEM((1, rows), F32), pltpu.VMEM((HEAD_DIM, rows), F32)]
    return pl.pallas_call(
        functools.partial(_nsa_prompt_body, tq=tq, nq=nq, nb=nb),
        grid=(b, g_, nq),
        in_specs=[
            qspec, qspec,
            pl.BlockSpec((nb, HEAD_DIM), lambda bi, g, i: (bi, g)),
            pl.BlockSpec((nb, HEAD_DIM), lambda bi, g, i: (bi, g_ + g)),
            col(2 * g_), col(3 * g_),
            col(0), col(g_),
            pl.BlockSpec((tq, LANES), lambda bi, g, i: (bi * nq + i, g)),
        ],
        out_specs=qspec,
        out_shape=jax.ShapeDtypeStruct((b * t, NSA_W), BF16),
        scratch_shapes=[pltpu.VMEM((rows, HEAD_DIM), BF16),
                        pltpu.VMEM((nq, tq, HEAD_DIM), BF16), pltpu.VMEM((nq, HEAD_DIM, tq), BF16),
                        pltpu.VMEM((nq, tq, HEAD_DIM), BF16), pltpu.VMEM((nq, HEAD_DIM, tq), BF16),
                        pltpu.VMEM((nb, tq), F32),
                        pltpu.VMEM((HEAD_DIM, rows), F32)] + state + state,
        compiler_params=_params("parallel", "parallel", "arbitrary"),
        name="nsa_prompt",
    )(qn, qr, cmp_p, cmp_p, nkv, nkv, win, win, gates)


def _stack_heads(q_ref, col0, n_heads):
    return jnp.concatenate([q_ref[:, col0 + r * HEAD_DIM:col0 + (r + 1) * HEAD_DIM] for r in range(n_heads)], axis=0)


def _repeat_rows(a, times):
    return jnp.concatenate([jnp.broadcast_to(a[h:h + 1, :], (times, a.shape[1])) for h in range(a.shape[0])], axis=0)


def _fox_decode_body(pt_ref, q_ref, *refs, n_steps, pp, ts):
    kv_refs, lc_refs = refs[:pp], refs[pp:2 * pp]
    new_ref, lcn_ref, o_ref, qs_ref, carry_ref, m_ref, l_ref, acc_ref = refs[2 * pp:]
    p = pl.program_id(1)
    rr = FOX_GROUP
    rows = rr * ts

    @pl.when(p == 0)
    def _():
        _init_state((m_ref, l_ref, acc_ref))
        carry_ref[...] = jnp.zeros_like(carry_ref)
        for g in range(FOX_KV_HEADS):
            qs_ref[g] = _stack_heads(q_ref, g * rr * HEAD_DIM, rr).astype(BF16)

    def attend(pages, c_k, keep):
        bias = _repeat_rows(c_k * LOG2E, ts)
        s = jnp.concatenate(
            [jnp.concatenate([_dot_nt(qs_ref[g], _slab(pg, g, PAGE_SLABS, PAGE_SIZE).astype(BF16))
                              for pg in pages], axis=1)
             for g in range(FOX_KV_HEADS)], axis=0) * SCALE2 - bias
        if keep is not None:
            s = jnp.where(keep, s, NEG_INF)

        def pv(pb):
            out = []
            for g in range(FOX_KV_HEADS):
                acc = None
                for u, pg in enumerate(pages):
                    part = _dot(pb[g * rows:(g + 1) * rows, u * PAGE_SIZE:(u + 1) * PAGE_SIZE],
                                _slab(pg, FOX_KV_HEADS + g, PAGE_SLABS, PAGE_SIZE).astype(BF16))
                    acc = part if acc is None else acc + part
                out.append(acc)
            return jnp.concatenate(out, axis=0)

        _online_update(s, pv, m_ref, l_ref, acc_ref, guard=False)

    run = carry_ref[...]
    c_pages = []
    for u in range(pp):
        lc = lc_refs[u][...]
        c_pages.append(run + lc)
        run = run + lc[:, PAGE_SIZE - 1:PAGE_SIZE]
    carry_ref[...] = run
    attend(kv_refs, jnp.concatenate(c_pages, axis=1), None)

    @pl.when(p == n_steps - 1)
    def _():
        keep_i = jnp.where(_iota((ts, PAGE_SIZE), 1) <= _iota((ts, PAGE_SIZE), 0), 1.0, 0.0)
        keep = jnp.concatenate([keep_i] * FOX_HEADS, axis=0) > 0.5
        attend([new_ref], run + lcn_ref[...], keep)
        o = _normalize(acc_ref[...], l_ref[...])
        for h in range(FOX_HEADS):
            o_ref[:, h * HEAD_DIM:(h + 1) * HEAD_DIM] = o[h * ts:(h + 1) * ts, :]


def _fox_decode(pt_flat, q_s, cache_kv, lc_pool, new_pad, lc_new, db, ts, n_pages, pp):
    rows = FOX_HEADS * ts
    n_steps = n_pages // pp
    page = (PAGE_SIZE * PAGE_SLABS, HEAD_DIM)
    grid_spec = pltpu.PrefetchScalarGridSpec(
        num_scalar_prefetch=1,
        grid=(db, n_steps),
        in_specs=[pl.BlockSpec((None, ts, FOX_W), lambda b, p, pt: (b, 0, 0))]
        + _paged_specs(page, n_pages, pp)
        + _paged_specs((FOX_HEADS, PAGE_SIZE), n_pages, pp)
        + [pl.BlockSpec((None,) + page, lambda b, p, pt: (b, 0, 0)),
           pl.BlockSpec((None, FOX_HEADS, PAGE_SIZE), lambda b, p, pt: (b, 0, 0))],
        out_specs=pl.BlockSpec((None, ts, FOX_W), lambda b, p, pt: (b, 0, 0)),
        scratch_shapes=[pltpu.VMEM((FOX_KV_HEADS, FOX_GROUP * ts, HEAD_DIM), BF16),
                        pltpu.VMEM((FOX_HEADS, 1), F32),
                        pltpu.VMEM((rows, 1), F32), pltpu.VMEM((rows, 1), F32),
                        pltpu.VMEM((rows, HEAD_DIM), F32)],
    )
    return pl.pallas_call(
        functools.partial(_fox_decode_body, n_steps=n_steps, pp=pp, ts=ts),
        grid_spec=grid_spec,
        out_shape=jax.ShapeDtypeStruct((db, ts, FOX_W), F32),
        compiler_params=_params("parallel", "arbitrary"),
        name="fox_decode",
    )(pt_flat, q_s, *([cache_kv] * pp), *([lc_pool] * pp), new_pad, lc_new)


def _nsa_decode_body(pt_ref, qn_ref, qr_ref, cmp_ref, *refs, n_steps, pp, ts, past, nbp, wb):
    kv_refs = refs[:pp]
    (new_ref, win_ref, wnew_ref, g_ref, o_ref, nwin_ref,
     qrs_ref, oc_ref, ow_ref, sel_ref, m_ref, l_ref, acc_ref) = refs[pp:]
    p = pl.program_id(1)
    rr = NSA_GROUP
    rows = rr * ts
    gg = NSA_KV_HEADS
    tile_rows = lambda a, n: jnp.concatenate([a] * n, axis=0)
    qi_col = _iota((ts, 1), 0)

    @pl.when(p == 0)
    def _():
        _init_state((m_ref, l_ref, acc_ref))
        pos_q = past + qi_col
        blk = _iota((ts, nbp), 1)
        valid = (blk + 1) * NSA_BLOCK - 1 <= pos_q
        exists = blk * NSA_BLOCK < past + ts
        keep_c = tile_rows(jnp.where(valid, 1.0, 0.0), rr)
        cur = lax.shift_right_logical(pos_q, BLOCK_SHIFT)
        dpos = wb + qi_col - _iota((ts, wb), 1)
        keep_buf = jnp.where(dpos >= 0, jnp.where(dpos < NSA_WINDOW, 1.0, 0.0), 0.0)
        keep_new = jnp.where(_iota((ts, PAGE_SIZE), 1) <= qi_col, 1.0, 0.0)
        keep_w = tile_rows(jnp.concatenate([keep_buf, keep_new], axis=1), rr)
        for g in range(gg):
            qn = _stack_heads(qn_ref, g * rr * HEAD_DIM, rr).astype(BF16)
            qr = _stack_heads(qr_ref, g * rr * HEAD_DIM, rr).astype(BF16)
            qrs_ref[g] = qr
            kc = cmp_ref[:, g * HEAD_DIM:(g + 1) * HEAD_DIM].astype(BF16)
            vc = cmp_ref[:, NSA_KV_W + g * HEAD_DIM:NSA_KV_W + (g + 1) * HEAD_DIM].astype(BF16)
            p_c = _masked_softmax(_dot_nt(qn, kc) * SCALE2, keep_c, axis=-1)
            oc_ref[g * rows:(g + 1) * rows, :] = _dot(p_c.astype(BF16), vc)
            imp = p_c[0:ts]
            for r in range(1, rr):
                imp = imp + p_c[r * ts:(r + 1) * ts]
            score = jnp.where(blk == cur, SEL_FORCE, jnp.where(valid, imp, -1.0))
            score = jnp.where(exists, score, NEG_INF)
            sel = _select_blocks(score, min(NSA_TOP_N, (past + ts + NSA_BLOCK - 1) // NSA_BLOCK), axis=-1)
            sel_ref[g * rows:(g + 1) * rows, :] = tile_rows(sel, rr).astype(BF16)
            kw = _slab(win_ref, g, WIN_SLABS, wb).astype(BF16)
            vw = _slab(win_ref, gg + g, WIN_SLABS, wb).astype(BF16)
            kwn = _slab(wnew_ref, g, WIN_SLABS, PAGE_SIZE).astype(BF16)
            vwn = _slab(wnew_ref, gg + g, WIN_SLABS, PAGE_SIZE).astype(BF16)
            s_w = jnp.concatenate([_dot_nt(qr, kw), _dot_nt(qr, kwn)], axis=1) * SCALE2
            p_w = _masked_softmax(s_w, keep_w, axis=-1).astype(BF16)
            ow_ref[g * rows:(g + 1) * rows, :] = _dot(p_w[:, 0:wb], vw) + _dot(p_w[:, wb:wb + PAGE_SIZE], vwn)
        nwin_ref[0:(wb - ts) * WIN_SLABS, :] = win_ref[ts * WIN_SLABS:wb * WIN_SLABS, :]
        nwin_ref[(wb - ts) * WIN_SLABS:wb * WIN_SLABS, :] = wnew_ref[0:ts * WIN_SLABS, :]

    def attend(pages, key_start, keep_tok):
        n_keys = len(pages) * PAGE_SIZE
        keep = _expand_blocks(sel_ref[...], key_start, n_keys)
        if keep_tok is not None:
            keep = keep * keep_tok
        s = jnp.concatenate(
            [jnp.concatenate([_dot_nt(qrs_ref[g], _slab(pg, 2 * gg + g, PAGE_SLABS, PAGE_SIZE).astype(BF16))
                              for pg in pages], axis=1)
             for g in range(gg)], axis=0) * SCALE2
        s = jnp.where(keep > 0.5, s, NEG_INF)

        def pv(pb):
            out = []
            for g in range(gg):
                acc = None
                for u, pg in enumerate(pages):
                    part = _dot(pb[g * rows:(g + 1) * rows, u * PAGE_SIZE:(u + 1) * PAGE_SIZE],
                                _slab(pg, 3 * gg + g, PAGE_SLABS, PAGE_SIZE).astype(BF16))
                    acc = part if acc is None else acc + part
                out.append(acc)
            return jnp.concatenate(out, axis=0)

        _online_update(s, pv, m_ref, l_ref, acc_ref, guard=True)

    attend(kv_refs, p * (pp * PAGE_SIZE), None)

    @pl.when(p == n_steps - 1)
    def _():
        keep_tok = tile_rows(jnp.where(_iota((ts, PAGE_SIZE), 1) <= qi_col, 1.0, 0.0), gg * rr)
        attend([new_ref], past, keep_tok)
        o_s = _normalize(acc_ref[...], l_ref[...])
        gates = g_ref[...]

        def gate(br):
            return jnp.concatenate([gates[:, g * LANES + br * rr + r:g * LANES + br * rr + r + 1]
                                    for g in range(gg) for r in range(rr)], axis=0)

        o = gate(0) * oc_ref[...] + gate(1) * o_s + gate(2) * ow_ref[...]
        for h in range(NSA_HEADS):
            o_ref[:, h * HEAD_DIM:(h + 1) * HEAD_DIM] = o[h * ts:(h + 1) * ts, :]


def _nsa_decode(pt_flat, qn_s, qr_s, cmp_s, cache_nsa, new_pad, win_buf, wnew_pad, gates_s,
                db, ts, n_pages, pp):
    rows = NSA_HEADS * ts
    past = n_pages * PAGE_SIZE
    n_steps = n_pages // pp
    nbp = cmp_s.shape[1]
    wb = win_buf.shape[1] // WIN_SLABS
    page = (PAGE_SIZE * PAGE_SLABS, HEAD_DIM)
    whole = lambda shape: pl.BlockSpec((None,) + shape, lambda b, p, pt: (b,) + (0,) * len(shape))
    grid_spec = pltpu.PrefetchScalarGridSpec(
        num_scalar_prefetch=1,
        grid=(db, n_steps),
        in_specs=[whole((ts, NSA_W)), whole((ts, NSA_W)), whole((nbp, 2 * NSA_KV_W))]
        + _paged_specs(page, n_pages, pp)
        + [whole(page), whole((wb * WIN_SLABS, HEAD_DIM)), whole((PAGE_SIZE * WIN_SLABS, HEAD_DIM)),
           whole((ts, NSA_KV_HEADS * LANES))],
        out_specs=[whole((ts, NSA_W)), whole((wb * WIN_SLABS, HEAD_DIM))],
        scratch_shapes=[pltpu.VMEM((NSA_KV_HEADS, NSA_GROUP * ts, HEAD_DIM), BF16),
                        pltpu.VMEM((rows, HEAD_DIM), F32),
                        pltpu.VMEM((rows, HEAD_DIM), F32),
                        pltpu.VMEM((rows, nbp), BF16),
                        pltpu.VMEM((rows, 1), F32), pltpu.VMEM((rows, 1), F32),
                        pltpu.VMEM((rows, HEAD_DIM), F32)],
    )
    return pl.pallas_call(
        functools.partial(_nsa_decode_body, n_steps=n_steps, pp=pp, ts=ts, past=past, nbp=nbp, wb=wb),
        grid_spec=grid_spec,
        out_shape=[jax.ShapeDtypeStruct((db, ts, NSA_W), F32),
                   jax.ShapeDtypeStruct((db, wb * WIN_SLABS, HEAD_DIM), F32)],
        compiler_params=_params("parallel", "arbitrary"),
        name="nsa_decode",
    )(pt_flat, qn_s, qr_s, cmp_s, *([cache_nsa] * pp), new_pad, win_buf, wnew_pad, gates_s)


def _topk_rows(x, k):
    n = x.shape[0]
    row = _iota(x.shape, 0).astype(F32)
    vals, idxs = [], []
    for _ in range(k):
        mx = jnp.max(x, axis=0, keepdims=True)
        ix = jnp.min(jnp.where(x == mx, row, float(n)), axis=0, keepdims=True)
        x = jnp.where(row == ix, NEG_INF, x)
        vals.append(mx)
        idxs.append(ix)
    return jnp.concatenate(vals, axis=0), jnp.concatenate(idxs, axis=0)


def _peer_select_body(q_ref, keys_ref, a_ref, b_ref, g_ref):
    tm = q_ref.shape[0]
    kk = PEER_TOPK
    a_all, b_all, g_all = [], [], []
    for h in range(PEER_HEADS):
        sub = []
        for c in range(2):
            keys = keys_ref[h, c].astype(BF16)
            col0 = (h * 2 + c) * PEER_HALF
            sub.append(_topk_rows(_dot_nt(keys, q_ref[:, col0:col0 + PEER_HALF]), kk))
        (s1, i1), (s2, i2) = sub
        cand = jnp.concatenate([s1[i:i + 1] + s2 for i in range(kk)], axis=0)
        ca = jnp.concatenate([jnp.broadcast_to(i1[i:i + 1], (kk, tm)) for i in range(kk)], axis=0)
        cb = jnp.concatenate([i2] * kk, axis=0)
        row = _iota(cand.shape, 0).astype(F32)
        tops, a_h, b_h = [], [], []
        for _ in range(kk):
            mx = jnp.max(cand, axis=0, keepdims=True)
            ix = jnp.min(jnp.where(cand == mx, row, float(kk * kk)), axis=0, keepdims=True)
            hit = row == ix
            a_h.append(jnp.sum(jnp.where(hit, ca, 0.0), axis=0, keepdims=True))
            b_h.append(jnp.sum(jnp.where(hit, cb, 0.0), axis=0, keepdims=True))
            cand = jnp.where(hit, NEG_INF, cand)
            tops.append(mx)
        top = jnp.concatenate(tops, axis=0)
        e = jnp.exp(top - top[0:1])
        g_all.append(e / jnp.sum(e, axis=0, keepdims=True))
        a_all.append(jnp.concatenate(a_h, axis=0))
        b_all.append(jnp.concatenate(b_h, axis=0))
    a_ref[...] = jnp.concatenate(a_all, axis=0).T
    b_ref[...] = jnp.concatenate(b_all, axis=0).T
    g_ref[...] = jnp.concatenate(g_all, axis=0).T


def _peer_select(q, peer_keys):
    n = q.shape[0]
    tm = _tile(n, LANES, LANES)
    slots = PEER_HEADS * PEER_TOPK
    out = jax.ShapeDtypeStruct((n, slots), F32)
    spec = pl.BlockSpec((tm, slots), lambda i: (i, 0))
    return pl.pallas_call(
        _peer_select_body,
        grid=(n // tm,),
        in_specs=[pl.BlockSpec((tm, q.shape[1]), lambda i: (i, 0)),
                  pl.BlockSpec(peer_keys.shape, lambda i: (0, 0, 0, 0))],
        out_specs=[spec, spec, spec],
        out_shape=[out, out, out],
        compiler_params=_params("parallel"),
        name="peer_select",
    )(q, peer_keys)


def _peer_coef_body(a_ref, b_ref, g_ref, act_ref, o_ref, w_ref):
    tt = a_ref.shape[0]
    slots = a_ref.shape[1]
    sub = _iota((PEER_KEYS, slots), 0).astype(F32)
    for t in range(tt):
        a_row = a_ref[t:t + 1, :]
        b_row = b_ref[t:t + 1, :]
        g_row = g_ref[t:t + 1, :]
        a_hot = jnp.where(sub == a_row, 1.0, 0.0).astype(BF16)
        b_val = jnp.where(sub == b_row, g_row, 0.0)
        hi = b_val.astype(BF16)
        lo = (b_val - hi.astype(F32)).astype(BF16)
        w_ref[:, t, :] = _dot_nt(a_hot, hi) + _dot_nt(a_hot, lo)
    o_ref[...] = (w_ref[...] * act_ref[...]).astype(o_ref.dtype)


def _peer_coef(a_sel, b_sel, g_sel, act):
    n, slots = a_sel.shape
    tt = _tile(n, 16)
    spec = pl.BlockSpec((tt, slots), lambda i: (i, 0))
    cube = pl.BlockSpec((PEER_KEYS, tt, PEER_KEYS), lambda i: (0, i, 0))
    return pl.pallas_call(
        _peer_coef_body,
        grid=(n // tt,),
        in_specs=[spec, spec, spec, cube],
        out_specs=cube,
        out_shape=jax.ShapeDtypeStruct((PEER_KEYS, n, PEER_KEYS), BF16),
        scratch_shapes=[pltpu.VMEM((PEER_KEYS, tt, PEER_KEYS), F32)],
        compiler_params=_params("parallel"),
        name="peer_coef",
    )(a_sel, b_sel, g_sel, act)


def _rope_tables(pos):
    half = HEAD_DIM // 2
    inv = ROPE_THETA ** (-jnp.arange(half, dtype=F32) / half)
    ang = pos.astype(F32)[:, None] * inv[None, :]
    cos, sin = jnp.cos(ang), jnp.sin(ang)
    return jnp.concatenate([cos, cos], axis=-1), jnp.concatenate([-sin, sin], axis=-1)


def _split_w_in(w_in):
    d = w_in.shape[0]
    o_f = FOX_W + 2 * FOX_KV_W
    o_q = o_f + FOX_HEADS
    o_g = o_q + NSA_W + 6 * NSA_KV_W
    main = jnp.concatenate([w_in[:, :o_f], w_in[:, o_q:o_g]], axis=1)
    forget = jnp.pad(w_in[:, o_f:o_q], ((0, 0), (0, LANES - FOX_HEADS)))
    gates = w_in[:, o_g:].reshape(d, NSA_KV_HEADS, NSA_GROUP, NSA_BRANCHES)
    gates = jnp.transpose(gates, (0, 1, 3, 2)).reshape(d, NSA_KV_HEADS, NSA_BRANCHES * NSA_GROUP)
    gates = jnp.pad(gates, ((0, 0), (0, 0), (0, LANES - NSA_BRANCHES * NSA_GROUP)))
    return main, jnp.concatenate([forget, gates.reshape(d, NSA_KV_HEADS * LANES)], axis=1)


def _pad_axis1(a, size):
    return jnp.pad(a, ((0, 0), (0, size - a.shape[1])) + ((0, 0),) * (a.ndim - 2))


def _layer(x, p_ple, cache_fox_kv, cache_fox_logf, cache_nsa_kv, state_win, pt_flat, dims,
           g_mix, w_in, b_f, w_ck, w_cv, w_a, w_b, w_merge, b_merge, w_o,
           g_ffn, w_pq, peer_keys, peer_u, peer_v, g_ple, w_ple_gate, w_ple):
    b, t, db, ts, n_pages = dims
    n, d = x.shape
    n_p = b * t
    past = n_pages * PAGE_SIZE
    pp = _tile(n_pages, PAGES_PER_STEP, 1)

    h = _rmsnorm(x, g_mix, BF16)
    w_main, w_small = _split_w_in(w_in)
    proj = _matmul(h, w_main, tm=1664, tn=512, tk=2048, out_dtype=F32, name="in_proj")
    proj2 = _matmul(h, w_small, tm=1664, tn=384, tk=2048, out_dtype=F32, name="in_proj_small")
    pos = jnp.concatenate([jnp.tile(jnp.arange(t), b), jnp.tile(past + jnp.arange(ts), db)])
    cos, sin = _rope_tables(pos)
    bf_row = jnp.pad(b_f, (0, LANES - FOX_HEADS)).reshape(1, LANES)
    qf, fkv, qn, qr, nkv, win, logf, gates = _post_projection(proj, proj2, cos, sin, bf_row)

    lf_p_t = jnp.transpose(logf[:n_p].reshape(b, t, FOX_HEADS), (0, 2, 1))
    o_f_p = _fox_prompt(qf, fkv, _cumsum_prompt(lf_p_t), b, t)
    w_cmp = jnp.concatenate([jnp.tile(w_ck, (1, NSA_KV_HEADS)), jnp.tile(w_cv, (1, NSA_KV_HEADS))], axis=1)
    cmp_p = _compress_rows(nkv, w_cmp, n_p)
    o_n_p = _nsa_prompt(qn, qr, cmp_p, nkv, win, gates, b, t)

    sample = lambda a: a[n_p:].reshape(db, ts, a.shape[1])
    n_pool = cache_fox_logf.shape[0]
    lc_pool = _page_cumsum(jnp.transpose(cache_fox_logf, (0, 2, 1)).reshape(n_pool * FOX_HEADS, PAGE_SIZE))
    lf_new_t = jnp.pad(jnp.transpose(sample(logf), (0, 2, 1)), ((0, 0), (0, 0), (0, PAGE_SIZE - ts)))
    lc_new = _page_cumsum(lf_new_t.reshape(db * FOX_HEADS, PAGE_SIZE))
    slab_rows = lambda a: a.reshape(a.shape[0], -1, HEAD_DIM)
    fkv_new = slab_rows(_pad_axis1(sample(fkv), PAGE_SIZE))
    o_f_s = _fox_decode(pt_flat, sample(qf).astype(F32), slab_rows(cache_fox_kv),
                        lc_pool.reshape(n_pool, FOX_HEADS, PAGE_SIZE), fkv_new,
                        lc_new.reshape(db, FOX_HEADS, PAGE_SIZE), db, ts, n_pages, pp)

    cache_nsa = slab_rows(cache_nsa_kv)
    nkv_new = _pad_axis1(sample(nkv), PAGE_SIZE)
    cmp_new = _compress_rows(nkv_new.reshape(db * PAGE_SIZE, 4 * NSA_KV_W), w_cmp, db * PAGE_SIZE)
    cmp_s = jnp.concatenate([_compress_paged(pt_flat, cache_nsa, w_cmp, db, n_pages, pp),
                             cmp_new.reshape(db, PAGE_SIZE // NSA_BLOCK, 2 * NSA_KV_W)], axis=1)
    cmp_s = _pad_axis1(cmp_s, -(-cmp_s.shape[1] // LANES) * LANES)
    wb = state_win.shape[1]
    o_n_s, new_win = _nsa_decode(
        pt_flat, sample(qn).astype(F32), sample(qr).astype(F32), cmp_s, cache_nsa, slab_rows(nkv_new),
        slab_rows(state_win), slab_rows(_pad_axis1(sample(win), PAGE_SIZE)),
        sample(gates), db, ts, n_pages, pp)

    o_f = jnp.concatenate([o_f_p, o_f_s.reshape(db * ts, FOX_W).astype(BF16)], axis=0)
    o_n = jnp.concatenate([o_n_p, o_n_s.reshape(db * ts, NSA_W).astype(BF16)], axis=0)

    gate = _matmul(h, w_merge, tm=1664, tn=512, tk=2048, out_dtype=F32, name="merge_gate",
                   epilogue=lambda acc, bias: jax.nn.sigmoid(acc + bias), extra=[(b_merge.reshape(1, -1), "row", 0)])
    m_a = _matmul(o_f, w_a, tm=1664, tn=512, tk=2048, out_dtype=F32, name="merge_fox",
                  epilogue=lambda acc, ga: ga * acc, extra=[(gate, "tile", 0)])
    tn_b = _tile(d, 512, LANES)
    mixed = _matmul(o_n, w_b, tm=1664, tn=512, tk=2048, out_dtype=BF16, name="merge_nsa",
                    epilogue=lambda acc, gb, ma: ma + gb * acc,
                    extra=[(gate, "tile", d // tn_b), (m_a, "tile", 0)])
    x1 = _matmul(mixed, w_o, tm=1664, tn=512, tk=2048, out_dtype=F32, name="out_proj",
                 epilogue=lambda acc, res: res + acc, extra=[(x, "tile", 0)])

    h2 = _rmsnorm(x1, g_ffn, BF16)
    pq = _matmul(h2, w_pq, tm=1664, tn=512, tk=2048, out_dtype=BF16, name="peer_query")
    a_sel, b_sel, g_sel = _peer_select(pq, peer_keys)
    act = _matmul(h2, peer_u, tm=1664, tn=256, tk=d, out_dtype=F32, nt=True, split_out=True,
                  epilogue=_gelu_tanh, name="peer_act")
    coef = _peer_coef(a_sel, b_sel, g_sel, act)
    x2 = _matmul(coef, peer_v, tm=1664, tn=1024, tk=1024, out_dtype=F32, stacked_lhs=True, name="peer_out",
                 epilogue=lambda acc, res: res + acc, extra=[(x1, "tile", 0)])

    h3 = _rmsnorm(x2, g_ple, BF16)
    pw = _matmul(p_ple.astype(BF16), w_ple, tm=1664, tn=512, tk=2048, out_dtype=F32, name="ple_embed")
    x3 = _matmul(h3, w_ple_gate, tm=1664, tn=512, tk=2048, out_dtype=F32, name="ple_gate",
                 epilogue=lambda acc, res, pe: res + jax.nn.sigmoid(acc) * pe,
                 extra=[(x2, "tile", 0), (pw, "tile", 0)])

    caches = dict(
        fox_kv_p=fkv[:n_p].reshape(b, t, 2, FOX_KV_HEADS, HEAD_DIM),
        fox_lf_p=logf[:n_p].reshape(b, t, FOX_HEADS),
        nsa_kv_p=nkv[:n_p].reshape(b, t, 4, NSA_KV_HEADS, HEAD_DIM),
        nsa_win_p=win[:n_p].reshape(b, t, 2, NSA_KV_HEADS, HEAD_DIM)[:, t - min(NSA_WINDOW, t):],
        fox_kv_s=fkv[n_p:].reshape(db, ts, 2, FOX_KV_HEADS, HEAD_DIM),
        fox_lf_s=logf[n_p:].reshape(db, ts, FOX_HEADS),
        nsa_kv_s=nkv[n_p:].reshape(db, ts, 4, NSA_KV_HEADS, HEAD_DIM),
        nsa_win_s=new_win.reshape(db, wb, 2, NSA_KV_HEADS, HEAD_DIM),
    )
    return x3, caches


def kernel(x_prompt, x_sample, cache_fox_kv, cache_fox_logf, cache_nsa_kv, state_nsa_win, page_table, p_prompt, p_sample, g_mix, w_in, b_f, w_ck, w_cv, w_a, w_b, w_merge, b_merge, w_o, g_ffn, w_pq, peer_keys, peer_u, peer_v, g_ple, w_ple_gate, w_ple, g_final):
    b, t, d = x_prompt.shape
    db, ts, _ = x_sample.shape
    depth = g_mix.shape[0]
    n_pages = page_table.shape[1]
    n_p = b * t
    dims = (b, t, db, ts, n_pages)
    pt_flat = page_table.reshape(-1).astype(jnp.int32)
    x = jnp.concatenate([x_prompt.reshape(n_p, d), x_sample.reshape(db * ts, d)], axis=0)
    per_layer = []
    for i in range(depth):
        p_ple = jnp.concatenate([p_prompt[i].reshape(n_p, -1), p_sample[i].reshape(db * ts, -1)], axis=0)
        x, caches = _layer(x, p_ple, cache_fox_kv[i], cache_fox_logf[i], cache_nsa_kv[i], state_nsa_win[i],
                           pt_flat, dims, g_mix[i], w_in[i], b_f[i], w_ck[i], w_cv[i], w_a[i], w_b[i],
                           w_merge[i], b_merge[i], w_o[i], g_ffn[i], w_pq[i], peer_keys[i], peer_u[i],
                           peer_v[i], g_ple[i], w_ple_gate[i], w_ple[i])
        per_layer.append(caches)
    y = _rmsnorm(x, g_final, F32)
    stack = lambda name: jnp.stack([c[name] for c in per_layer])
    return (y[:n_p].reshape(b, t, d), y[n_p:].reshape(db, ts, d),
            stack("fox_kv_p"), stack("fox_lf_p"), stack("nsa_kv_p"), stack("nsa_win_p"),
            stack("fox_kv_s"), stack("fox_lf_s"), stack("nsa_kv_s"), stack("nsa_win_s"))
```

```python
import functools

import jax
import jax.numpy as jnp
from jax import lax
from jax.experimental import pallas as pl
from jax.experimental.pallas import tpu as pltpu

F32 = jnp.float32
BF16 = jnp.bfloat16
NEG_INF = float("-inf")

HEAD_DIM = 128
FOX_HEADS = 16
FOX_KV_HEADS = 4
FOX_GROUP = FOX_HEADS // FOX_KV_HEADS
NSA_HEADS = 16
NSA_KV_HEADS = 2
NSA_GROUP = NSA_HEADS // NSA_KV_HEADS
NSA_BLOCK = 64
BLOCK_SHIFT = NSA_BLOCK.bit_length() - 1
NSA_TOP_N = 16
NSA_WINDOW = 512
NSA_BRANCHES = 3
SEL_FORCE = 1e4
ROPE_THETA = 10000.0
PAGE_SIZE = 128
PEER_HEADS = 8
PEER_KEYS = 128
PEER_TOPK = 16
PEER_HALF = 128
RMS_EPS = 1e-6
SCALE = HEAD_DIM ** -0.5
LOG2E = 1.4426950408889634
SCALE2 = SCALE * LOG2E

FOX_W = FOX_HEADS * HEAD_DIM
FOX_KV_W = FOX_KV_HEADS * HEAD_DIM
NSA_W = NSA_HEADS * HEAD_DIM
NSA_KV_W = NSA_KV_HEADS * HEAD_DIM
LANES = 128
VMEM_LIMIT = 56 * 1024 * 1024
PAGES_PER_STEP = 8
PAGE_SLABS = 8
WIN_SLABS = 2 * NSA_KV_HEADS

NT_DIMS = (((1,), (1,)), ((), ()))
FULL_K = dict(tm=832, tn=512, tk=4096)


def _params(*sem):
    return pltpu.CompilerParams(dimension_semantics=sem, vmem_limit_bytes=VMEM_LIMIT)


def _tile(n, target, align=8):
    if n <= target:
        return n
    best = None
    for t in range(align, target + 1, align):
        if n % t == 0:
            best = t
    assert best is not None, (n, target, align)
    return best


def _dot(a, b):
    return jnp.dot(a, b, preferred_element_type=F32)


def _dot_nt(a, b):
    return lax.dot_general(a, b, NT_DIMS, preferred_element_type=F32)


def _dot_tn(a, b):
    k = a.shape[0]
    kp = -(-k // LANES) * LANES
    if kp != k:
        a = jnp.concatenate([a, jnp.zeros((kp - k, a.shape[1]), a.dtype)], axis=0)
        b = jnp.concatenate([b, jnp.zeros((kp - k, b.shape[1]), b.dtype)], axis=0)
    return _dot(a.T.astype(BF16), b.astype(BF16))


def _iota(shape, dim, dtype=jnp.int32):
    return lax.broadcasted_iota(dtype, shape, dim)


def _rmsnorm_body(x_ref, g_ref, o_ref):
    x = x_ref[...]
    y = x * lax.rsqrt(jnp.mean(x * x, axis=-1, keepdims=True) + RMS_EPS)
    o_ref[...] = (y * g_ref[...]).astype(o_ref.dtype)


def _rmsnorm(x, g, out_dtype, row0=0, n=None):
    d = x.shape[1]
    n = x.shape[0] if n is None else n
    tr = _tile(n, 256)
    while row0 % tr:
        tr //= 2
    first = row0 // tr
    return pl.pallas_call(
        _rmsnorm_body,
        grid=(n // tr,),
        in_specs=[pl.BlockSpec((tr, d), lambda i: (first + i, 0)), pl.BlockSpec((1, d), lambda i: (0, 0))],
        out_specs=pl.BlockSpec((tr, d), lambda i: (i, 0)),
        out_shape=jax.ShapeDtypeStruct((n, d), out_dtype),
        compiler_params=_params("parallel"),
        name="rmsnorm",
    )(x, g.reshape(1, d))


def _gelu_tanh(x):
    return 0.5 * x * (1.0 + jnp.tanh(0.7978845608028654 * (x + 0.044715 * (x * x * x))))


def _mm_body(*refs, n_extra, epilogue, nt, nk, split_out, stacked_lhs):
    x_ref, w_ref = refs[0], refs[1]
    extra = refs[2:2 + n_extra]
    o_ref = refs[2 + n_extra]
    w = w_ref[...].astype(BF16)
    if stacked_lhs:
        x = jnp.concatenate([x_ref[j] for j in range(x_ref.shape[0])], axis=1)
    else:
        x = x_ref[...]
    part = _dot_nt(x, w) if nt else _dot(x, w)

    def finish(acc):
        res = epilogue(acc, *[e[...] for e in extra]).astype(o_ref.dtype)
        if split_out:
            for j in range(o_ref.shape[0]):
                o_ref[j] = res[:, j * LANES:(j + 1) * LANES]
        else:
            o_ref[...] = res

    if nk == 1:
        finish(part)
        return
    acc_ref = refs[3 + n_extra]
    k = pl.program_id(2)

    @pl.when(k == 0)
    def _():
        acc_ref[...] = part

    @pl.when(k > 0)
    def _():
        acc_ref[...] += part

    @pl.when(k == nk - 1)
    def _():
        finish(acc_ref[...])


def _matmul(x, w, *, tm, tn, tk, out_dtype, name, epilogue=None, extra=(), nt=False, split_out=False,
            stacked_lhs=False):
    if stacked_lhs:
        m, kdim = x.shape[1], x.shape[0] * LANES
    else:
        m, kdim = x.shape
    n = w.shape[0] if nt else w.shape[1]
    tm, tn, tk = _tile(m, tm), _tile(n, tn, LANES), _tile(kdim, tk, LANES)
    nk = kdim // tk
    if epilogue is None:
        epilogue = lambda acc: acc
    if stacked_lhs:
        x_spec = pl.BlockSpec((tk // LANES, tm, LANES), lambda i, j, k: (k, i, 0))
    else:
        x_spec = pl.BlockSpec((tm, tk), lambda i, j, k: (i, k))
    in_specs = [
        x_spec,
        pl.BlockSpec((tn, tk), lambda i, j, k: (j, k)) if nt else pl.BlockSpec((tk, tn), lambda i, j, k: (k, j)),
    ]
    args = [x, w]
    for arr, kind, off in extra:
        if kind == "tile":
            in_specs.append(pl.BlockSpec((tm, tn), lambda i, j, k, off=off: (i, j + off)))
        else:
            in_specs.append(pl.BlockSpec((1, tn), lambda i, j, k, off=off: (0, j + off)))
        args.append(arr)
    if split_out:
        out_shape = jax.ShapeDtypeStruct((n // LANES, m, LANES), out_dtype)
        out_spec = pl.BlockSpec((tn // LANES, tm, LANES), lambda i, j, k: (j, i, 0))
    else:
        out_shape = jax.ShapeDtypeStruct((m, n), out_dtype)
        out_spec = pl.BlockSpec((tm, tn), lambda i, j, k: (i, j))
    body = functools.partial(_mm_body, n_extra=len(extra), epilogue=epilogue, nt=nt, nk=nk,
                             split_out=split_out, stacked_lhs=stacked_lhs)
    return pl.pallas_call(
        body,
        grid=(m // tm, n // tn, nk),
        in_specs=in_specs,
        out_specs=out_spec,
        out_shape=out_shape,
        scratch_shapes=[pltpu.VMEM((tm, tn), F32)] if nk > 1 else [],
        compiler_params=_params("parallel", "parallel", "arbitrary"),
        name=name,
    )(*args)


def _rope(x, cos, sin_signed):
    return x * cos + pltpu.roll(x, HEAD_DIM // 2, 1) * sin_signed


def _log_sigmoid(z):
    return jnp.minimum(z, 0.0) - jnp.log1p(jnp.exp(-jnp.abs(z)))


def _post_body(p_ref, p2_ref, cos_ref, sin_ref, bf_ref,
               qf_ref, fkv_ref, qn_ref, qr_ref, nkv_ref, win_ref, logf_ref, gate_ref):
    cos = cos_ref[...]
    sin = sin_ref[...]
    o = 0
    qf_ref[...] = p_ref[:, o:o + FOX_W].astype(BF16)
    o += FOX_W
    fkv_ref[...] = p_ref[:, o:o + 2 * FOX_KV_W]
    o += 2 * FOX_KV_W
    for h in range(NSA_HEADS):
        q = p_ref[:, o + h * HEAD_DIM:o + (h + 1) * HEAD_DIM]
        qn_ref[:, h * HEAD_DIM:(h + 1) * HEAD_DIM] = q.astype(BF16)
        qr_ref[:, h * HEAD_DIM:(h + 1) * HEAD_DIM] = _rope(q, cos, sin).astype(BF16)
    o += NSA_W
    nkv_ref[:, 0:2 * NSA_KV_W] = p_ref[:, o:o + 2 * NSA_KV_W]
    o += 2 * NSA_KV_W
    for g in range(NSA_KV_HEADS):
        k = p_ref[:, o + g * HEAD_DIM:o + (g + 1) * HEAD_DIM]
        nkv_ref[:, 2 * NSA_KV_W + g * HEAD_DIM:2 * NSA_KV_W + (g + 1) * HEAD_DIM] = _rope(k, cos, sin)
    o += NSA_KV_W
    nkv_ref[:, 3 * NSA_KV_W:4 * NSA_KV_W] = p_ref[:, o:o + NSA_KV_W]
    o += NSA_KV_W
    for g in range(NSA_KV_HEADS):
        k = p_ref[:, o + g * HEAD_DIM:o + (g + 1) * HEAD_DIM]
        win_ref[:, g * HEAD_DIM:(g + 1) * HEAD_DIM] = _rope(k, cos, sin)
    o += NSA_KV_W
    win_ref[:, NSA_KV_W:2 * NSA_KV_W] = p_ref[:, o:o + NSA_KV_W]
    logf = _log_sigmoid(p2_ref[:, 0:LANES] + bf_ref[...])
    logf_ref[...] = logf[:, 0:FOX_HEADS]
    gate_ref[...] = jax.nn.sigmoid(p2_ref[:, LANES:3 * LANES])


def _post_projection(p, p2, cos, sin, bf_row):
    n = p.shape[0]
    tr = _tile(n, 256)
    row = lambda w: pl.BlockSpec((tr, w), lambda i: (i, 0))
    out_shapes = [
        jax.ShapeDtypeStruct((n, FOX_W), BF16),
        jax.ShapeDtypeStruct((n, 2 * FOX_KV_W), F32),
        jax.ShapeDtypeStruct((n, NSA_W), BF16),
        jax.ShapeDtypeStruct((n, NSA_W), BF16),
        jax.ShapeDtypeStruct((n, 4 * NSA_KV_W), F32),
        jax.ShapeDtypeStruct((n, 2 * NSA_KV_W), F32),
        jax.ShapeDtypeStruct((n, FOX_HEADS), F32),
        jax.ShapeDtypeStruct((n, 2 * LANES), F32),
    ]
    return pl.pallas_call(
        _post_body,
        grid=(n // tr,),
        in_specs=[row(p.shape[1]), row(p2.shape[1]), row(LANES), row(LANES),
                  pl.BlockSpec((1, LANES), lambda i: (0, 0))],
        out_specs=[row(s.shape[1]) for s in out_shapes],
        out_shape=out_shapes,
        compiler_params=_params("parallel"),
        name="post_projection",
    )(p, p2, cos, sin, bf_row)


def _split3(x):
    hi = x.astype(BF16)
    r1 = x - hi.astype(F32)
    mid = r1.astype(BF16)
    lo = (r1 - mid.astype(F32)).astype(BF16)
    return hi, mid, lo


def _row_cumsum(x, u):
    hi, mid, lo = _split3(x)
    return (_dot(hi, u) + _dot(mid, u)) + _dot(lo, u)


def _cumsum_prompt_body(lf_ref, u_ref, o_ref, carry_ref):
    @pl.when(pl.program_id(1) == 0)
    def _():
        carry_ref[...] = jnp.zeros_like(carry_ref)

    c = _row_cumsum(lf_ref[...], u_ref[...]) + carry_ref[...]
    o_ref[...] = c
    r = c.shape[1]
    carry_ref[...] = c[:, r - 1:r]


def _upper_ones(r):
    return (jnp.arange(r)[:, None] <= jnp.arange(r)[None, :]).astype(BF16)


def _cumsum_prompt(lf_t):
    b, h, t = lf_t.shape
    r = _tile(t, 512, LANES)
    return pl.pallas_call(
        _cumsum_prompt_body,
        grid=(b, t // r),
        in_specs=[pl.BlockSpec((None, h, r), lambda i, j: (i, 0, j)),
                  pl.BlockSpec((r, r), lambda i, j: (0, 0))],
        out_specs=pl.BlockSpec((None, h, r), lambda i, j: (i, 0, j)),
        out_shape=jax.ShapeDtypeStruct((b, h, t), F32),
        scratch_shapes=[pltpu.VMEM((h, 1), F32)],
        compiler_params=_params("parallel", "arbitrary"),
        name="cumsum_prompt",
    )(lf_t, _upper_ones(r))


def _page_cumsum_body(x_ref, u_ref, o_ref):
    o_ref[...] = _row_cumsum(x_ref[...], u_ref[...])


def _page_cumsum(x):
    rows = x.shape[0]
    tr = _tile(rows, 2048)
    return pl.pallas_call(
        _page_cumsum_body,
        grid=(rows // tr,),
        in_specs=[pl.BlockSpec((tr, PAGE_SIZE), lambda i: (i, 0)),
                  pl.BlockSpec((PAGE_SIZE, PAGE_SIZE), lambda i: (0, 0))],
        out_specs=pl.BlockSpec((tr, PAGE_SIZE), lambda i: (i, 0)),
        out_shape=jax.ShapeDtypeStruct((rows, PAGE_SIZE), F32),
        compiler_params=_params("parallel"),
        name="page_cumsum",
    )(x, _upper_ones(PAGE_SIZE))


def _online_update_t(s, vt, m_ref, l_ref, acc_ref, guard):
    m_old = m_ref[...]
    m_new = jnp.maximum(m_old, jnp.max(s, axis=0, keepdims=True))
    m_use = jnp.where(m_new == NEG_INF, 0.0, m_new) if guard else m_new
    alpha = jnp.exp2(m_old - m_use)
    p = jnp.exp2(s - m_use)
    l_ref[...] = alpha * l_ref[...] + jnp.sum(p, axis=0, keepdims=True)
    acc_ref[...] = alpha * acc_ref[...] + _dot(vt, p.astype(BF16))
    m_ref[...] = m_new


def _online_update(s, pv_fn, m_ref, l_ref, acc_ref, guard):
    m_old = m_ref[...]
    m_new = jnp.maximum(m_old, jnp.max(s, axis=-1, keepdims=True))
    m_use = jnp.where(m_new == NEG_INF, 0.0, m_new) if guard else m_new
    alpha = jnp.exp2(m_old - m_use)
    p = jnp.exp2(s - m_use)
    l_ref[...] = alpha * l_ref[...] + jnp.sum(p, axis=-1, keepdims=True)
    acc_ref[...] = alpha * acc_ref[...] + pv_fn(p.astype(BF16))
    m_ref[...] = m_new


def _masked_softmax(s, keep, axis):
    s = jnp.where(keep > 0.5, s, NEG_INF)
    m = jnp.max(s, axis=axis, keepdims=True)
    m = jnp.where(m == NEG_INF, 0.0, m)
    e = jnp.exp2(s - m)
    return e / jnp.maximum(jnp.sum(e, axis=axis, keepdims=True), 1e-30)


def _normalize(acc, l):
    return acc / jnp.maximum(l, 1e-30)


def _init_state(*triples):
    for m_ref, l_ref, acc_ref in triples:
        m_ref[...] = jnp.full_like(m_ref, NEG_INF)
        l_ref[...] = jnp.zeros_like(l_ref)
        acc_ref[...] = jnp.zeros_like(acc_ref)


def _mask_heads(s, keep, n_heads, width):
    return jnp.concatenate(
        [jnp.where(keep, s[:, r * width:(r + 1) * width], NEG_INF) for r in range(n_heads)], axis=1)


def _fox_prompt_body(q_ref, qx_ref, k_ref, kx_ref, v_ref, o_ref,
                     qa_ref, ka_ref, vt_ref, m_ref, l_ref, acc_ref, *, tq, nq):
    i = pl.program_id(2)
    rr = FOX_GROUP

    @pl.when(i == 0)
    def _():
        def prep(j, carry):
            ks = pl.multiple_of(j * tq, tq)
            ka_ref[j, :, 0:HEAD_DIM] = k_ref[pl.ds(ks, tq), :].astype(BF16)
            ka_ref[j, :, HEAD_DIM:2 * HEAD_DIM] = kx_ref[pl.ds(ks, tq), :]
            vt_ref[j] = v_ref[pl.ds(ks, tq), :].T.astype(BF16)
            return carry

        lax.fori_loop(0, nq, prep, 0)

    for r in range(rr):
        qa_ref[r * tq:(r + 1) * tq, 0:HEAD_DIM] = q_ref[:, r * HEAD_DIM:(r + 1) * HEAD_DIM]
        qa_ref[r * tq:(r + 1) * tq, HEAD_DIM:2 * HEAD_DIM] = qx_ref[r]
    _init_state((m_ref, l_ref, acc_ref))

    def full_chunk(j, carry):
        s = _dot_nt(ka_ref[j], qa_ref[...]) * SCALE2
        _online_update_t(s, vt_ref[j], m_ref, l_ref, acc_ref, guard=False)
        return carry

    lax.fori_loop(0, i, full_chunk, 0)
    causal = _iota((tq, tq), 0) <= _iota((tq, tq), 1)
    s = _mask_heads(_dot_nt(ka_ref[i], qa_ref[...]) * SCALE2, causal, rr, tq)
    _online_update_t(s, vt_ref[i], m_ref, l_ref, acc_ref, guard=False)

    o_t = _normalize(acc_ref[...], l_ref[...])
    for r in range(rr):
        o_ref[:, r * HEAD_DIM:(r + 1) * HEAD_DIM] = o_t[:, r * tq:(r + 1) * tq].T.astype(o_ref.dtype)


def _fox_bias_columns(c_t, b, t):
    g_, r_ = FOX_KV_HEADS, FOX_GROUP
    pieces = jnp.stack(_split3(c_t * (1.0 / SCALE)), axis=-1).reshape(b, g_, r_, t, 3)
    kx = -jnp.transpose(pieces, (0, 1, 3, 2, 4)).reshape(b, g_, t, 3 * r_)
    kx = jnp.concatenate([kx, jnp.ones((b, g_, t, 3), BF16)], axis=-1)
    kx = jnp.pad(kx, ((0, 0), (0, 0), (0, 0), (0, LANES - kx.shape[-1])))
    own = (jnp.arange(r_)[:, None] == jnp.arange(3 * r_)[None, :] // 3).astype(BF16)
    own = jnp.broadcast_to(own[None, None, :, None, :], (b, g_, r_, t, 3 * r_))
    qx = jnp.concatenate([own, pieces], axis=-1)
    qx = jnp.pad(qx, ((0, 0), (0, 0), (0, 0), (0, 0), (0, LANES - qx.shape[-1])))
    return qx, kx


def _fox_prompt(qf, fkv, c_t, b, t):
    tq = _tile(t, 256, LANES)
    nq = t // tq
    g_, r_ = FOX_KV_HEADS, FOX_GROUP
    rows = r_ * tq
    qx, kx = _fox_bias_columns(c_t, b, t)
    return pl.pallas_call(
        functools.partial(_fox_prompt_body, tq=tq, nq=nq),
        grid=(b, g_, nq),
        in_specs=[
            pl.BlockSpec((tq, r_ * HEAD_DIM), lambda bi, g, i: (bi * nq + i, g)),
            pl.BlockSpec((None, None, r_, tq, LANES), lambda bi, g, i: (bi, g, 0, i, 0)),
            pl.BlockSpec((t, HEAD_DIM), lambda bi, g, i: (bi, g)),
            pl.BlockSpec((None, None, t, LANES), lambda bi, g, i: (bi, g, 0, 0)),
            pl.BlockSpec((t, HEAD_DIM), lambda bi, g, i: (bi, g_ + g)),
        ],
        out_specs=pl.BlockSpec((tq, r_ * HEAD_DIM), lambda bi, g, i: (bi * nq + i, g)),
        out_shape=jax.ShapeDtypeStruct((b * t, FOX_W), BF16),
        scratch_shapes=[pltpu.VMEM((rows, 2 * HEAD_DIM), BF16),
                        pltpu.VMEM((nq, tq, 2 * HEAD_DIM), BF16),
                        pltpu.VMEM((nq, HEAD_DIM, tq), BF16),
                        pltpu.VMEM((1, rows), F32), pltpu.VMEM((1, rows), F32),
                        pltpu.VMEM((HEAD_DIM, rows), F32)],
        compiler_params=_params("parallel", "parallel", "arbitrary"),
        name="fox_prompt",
    )(qf, qx, fkv, kx, fkv)


def _compress_body(x_ref, w_ref, o_ref):
    x = x_ref[...]
    nb = x.shape[0] // NSA_BLOCK
    xb = x.reshape(nb, NSA_BLOCK, x.shape[1]) * w_ref[...][None]
    o_ref[...] = jnp.sum(xb, axis=1) * (1.0 / NSA_BLOCK)


def _compress_rows(nkv, w_cmp, n_rows):
    width = 2 * NSA_KV_W
    tr = _tile(n_rows, 512, NSA_BLOCK * 8)
    return pl.pallas_call(
        _compress_body,
        grid=(n_rows // tr,),
        in_specs=[pl.BlockSpec((tr, width), lambda i: (i, 0)),
                  pl.BlockSpec((NSA_BLOCK, width), lambda i: (0, 0))],
        out_specs=pl.BlockSpec((tr // NSA_BLOCK, width), lambda i: (i, 0)),
        out_shape=jax.ShapeDtypeStruct((n_rows // NSA_BLOCK, width), F32),
        compiler_params=_params("parallel"),
        name="compress_rows",
    )(nkv, w_cmp)


def _compress_paged_body(pt_ref, *refs, pp):
    page_refs, w_ref, o_ref = refs[:pp], refs[pp], refs[pp + 1]
    nb = PAGE_SIZE // NSA_BLOCK
    for u in range(pp):
        for j in range(2 * NSA_KV_HEADS):
            x = _slab(page_refs[u], j, PAGE_SLABS, PAGE_SIZE)
            xb = x.reshape(nb, NSA_BLOCK, HEAD_DIM) * w_ref[:, j * HEAD_DIM:(j + 1) * HEAD_DIM][None]
            o_ref[u, :, j * HEAD_DIM:(j + 1) * HEAD_DIM] = jnp.sum(xb, axis=1) * (1.0 / NSA_BLOCK)


def _paged_specs(block, n_pages, pp):
    return [pl.BlockSpec((None,) + block, lambda b, p, pt, u=u: (pt[b * n_pages + p * pp + u], 0, 0))
            for u in range(pp)]


def _slab(ref, j, n_slabs, n_rows):
    return ref[pl.ds(j, n_rows, stride=n_slabs), :]


def _compress_paged(pt_flat, cache_nsa, w_cmp, db, n_pages, pp):
    width = 2 * NSA_KV_W
    nb = PAGE_SIZE // NSA_BLOCK
    grid_spec = pltpu.PrefetchScalarGridSpec(
        num_scalar_prefetch=1,
        grid=(db, n_pages // pp),
        in_specs=_paged_specs((PAGE_SIZE * PAGE_SLABS, HEAD_DIM), n_pages, pp)
        + [pl.BlockSpec((NSA_BLOCK, width), lambda b, p, pt: (0, 0))],
        out_specs=pl.BlockSpec((None, pp, nb, width), lambda b, p, pt: (b, p, 0, 0)),
    )
    out = pl.pallas_call(
        functools.partial(_compress_paged_body, pp=pp),
        grid_spec=grid_spec,
        out_shape=jax.ShapeDtypeStruct((db, n_pages, nb, width), F32),
        compiler_params=_params("parallel", "arbitrary"),
        name="compress_paged",
    )(pt_flat, *([cache_nsa] * pp), w_cmp)
    return out.reshape(db, n_pages * nb, width)


def _select_blocks(score, n_sel, axis):
    axis = axis % score.ndim
    nb = score.shape[axis]
    blk = _iota(score.shape, axis).astype(F32)
    sel = jnp.zeros(score.shape, F32)
    work = score
    for _ in range(n_sel):
        mx = jnp.max(work, axis=axis, keepdims=True)
        ix = jnp.min(jnp.where(work == mx, blk, float(nb)), axis=axis, keepdims=True)
        hit = blk == ix
        sel = jnp.where(hit, jnp.where(mx >= 0.0, 1.0, 0.0), sel)
        work = jnp.where(hit, NEG_INF, work)
    return sel


def _expand_blocks(sel_b, key_start, n_keys):
    nb = sel_b.shape[1]
    key_blk = lax.shift_right_logical(key_start + _iota((nb, n_keys), 1), BLOCK_SHIFT)
    e = jnp.where(_iota((nb, n_keys), 0) == key_blk, 1.0, 0.0).astype(BF16)
    return _dot(sel_b, e)


def _nsa_prompt_body(qn_ref, qr_ref, kc_ref, vc_ref, ks_ref, vs_ref, kw_ref, vw_ref, g_ref, o_ref,
                     qs_ref, ksb_ref, vst_ref, kwb_ref, vwt_ref, sel_ref, oc_ref,
                     m_s, l_s, acc_s, m_w, l_w, acc_w, *, tq, nq, nb):
    i = pl.program_id(2)
    rr = NSA_GROUP
    bpc = tq // NSA_BLOCK

    @pl.when(i == 0)
    def _():
        def prep(j, carry):
            ks = pl.multiple_of(j * tq, tq)
            ksb_ref[j] = ks_ref[pl.ds(ks, tq), :].astype(BF16)
            vst_ref[j] = vs_ref[pl.ds(ks, tq), :].T.astype(BF16)
            kwb_ref[j] = kw_ref[pl.ds(ks, tq), :].astype(BF16)
            vwt_ref[j] = vw_ref[pl.ds(ks, tq), :].T.astype(BF16)
            return carry

        lax.fori_loop(0, nq, prep, 0)

    stack = lambda ref: jnp.concatenate([ref[:, r * HEAD_DIM:(r + 1) * HEAD_DIM] for r in range(rr)], axis=0)
    lanes = lambda a: jnp.concatenate([a] * rr, axis=1)
    qn = stack(qn_ref)
    qs_ref[...] = stack(qr_ref)
    pos_q = i * tq + _iota((1, tq), 1)
    _init_state((m_s, l_s, acc_s), (m_w, l_w, acc_w))

    blk = _iota((nb, tq), 0)
    valid = (blk + 1) * NSA_BLOCK - 1 <= pos_q
    s_c = _dot_nt(kc_ref[...].astype(BF16), qn) * SCALE2
    p_c = _masked_softmax(s_c, lanes(jnp.where(valid, 1.0, 0.0)), axis=0)
    oc_ref[...] = _dot_tn(vc_ref[...], p_c)
    imp = p_c[:, 0:tq]
    for r in range(1, rr):
        imp = imp + p_c[:, r * tq:(r + 1) * tq]
    cur = lax.shift_right_logical(pos_q, BLOCK_SHIFT)
    score = jnp.where(blk == cur, SEL_FORCE, jnp.where(valid, imp, -1.0))
    sel_ref[...] = _select_blocks(score, min(NSA_TOP_N, nb), axis=0)

    def sel_keep(j):
        return jnp.concatenate(
            [jnp.broadcast_to(sel_ref[pl.ds(j * bpc + u, 1), :], (NSA_BLOCK, tq)) for u in range(bpc)], axis=0)

    def sel_chunk(j, carry):
        s = _mask_heads(_dot_nt(ksb_ref[j], qs_ref[...]) * SCALE2, sel_keep(j) > 0.5, rr, tq)
        _online_update_t(s, vst_ref[j], m_s, l_s, acc_s, guard=True)
        return carry

    lax.fori_loop(0, i, sel_chunk, 0)
    causal = _iota((tq, tq), 0) <= _iota((tq, tq), 1)
    keep = jnp.where(causal, sel_keep(i), 0.0) > 0.5
    s = _mask_heads(_dot_nt(ksb_ref[i], qs_ref[...]) * SCALE2, keep, rr, tq)
    _online_update_t(s, vst_ref[i], m_s, l_s, acc_s, guard=True)

    def win_chunk(j, carry):
        dpos = pos_q - (j * tq + _iota((tq, 1), 0))
        keep = jnp.where(dpos >= 0, jnp.where(dpos < NSA_WINDOW, 1.0, 0.0), 0.0) > 0.5
        s = _mask_heads(_dot_nt(kwb_ref[j], qs_ref[...]) * SCALE2, keep, rr, tq)
        _online_update_t(s, vwt_ref[j], m_w, l_w, acc_w, guard=True)
        return carry

    lax.fori_loop(jnp.maximum(i - (NSA_WINDOW + tq - 1) // tq, 0), i + 1, win_chunk, 0)

    g_t = g_ref[...].T
    o_s = _normalize(acc_s[...], l_s[...])
    o_w = _normalize(acc_w[...], l_w[...])
    for r in range(rr):
        cols = slice(r * tq, (r + 1) * tq)
        gate = lambda br: g_t[br * rr + r:br * rr + r + 1, :]
        o = gate(0) * oc_ref[:, cols] + gate(1) * o_s[:, cols] + gate(2) * o_w[:, cols]
        o_ref[:, r * HEAD_DIM:(r + 1) * HEAD_DIM] = o.T.astype(o_ref.dtype)


def _nsa_prompt(qn, qr, cmp_p, nkv, win, gates, b, t):
    tq = _tile(t, 256, LANES)
    nq = t // tq
    nb = t // NSA_BLOCK
    g_, rows = NSA_KV_HEADS, NSA_GROUP * tq
    qspec = pl.BlockSpec((tq, NSA_GROUP * HEAD_DIM), lambda bi, g, i: (bi * nq + i, g))
    col = lambda off: pl.BlockSpec((t, HEAD_DIM), lambda bi, g, i: (bi, off + g))
    state = [pltpu.VMEM((1, rows), F32), pltpu.VMEM((1, rows), F32), pltpu.VMEM((HEAD_DIM, rows), F32)]
    return pl.pallas_call(
        functools.partial(_nsa_prompt_body, tq=tq, nq=nq, nb=nb),
        grid=(b, g_, nq),
        in_specs=[
            qspec, qspec,
            pl.BlockSpec((nb, HEAD_DIM), lambda bi, g, i: (bi, g)),
            pl.BlockSpec((nb, HEAD_DIM), lambda bi, g, i: (bi, g_ + g)),
            col(2 * g_), col(3 * g_),
            col(0), col(g_),
            pl.BlockSpec((tq, LANES), lambda bi, g, i: (bi * nq + i, g)),
        ],
        out_specs=qspec,
        out_shape=jax.ShapeDtypeStruct((b * t, NSA_W), BF16),
        scratch_shapes=[pltpu.VMEM((rows, HEAD_DIM), BF16),
                        pltpu.VMEM((nq, tq, HEAD_DIM), BF16), pltpu.VMEM((nq, HEAD_DIM, tq), BF16),
                        pltpu.VMEM((nq, tq, HEAD_DIM), BF16), pltpu.VMEM((nq, HEAD_DIM, tq), BF16),
                        pltpu.VMEM((nb, tq), F32),
                        pltpu.VMEM((HEAD_DIM, rows), F32)] + state + state,
        compiler_params=_params("parallel", "parallel", "arbitrary"),
        name="nsa_prompt",
    )(qn, qr, cmp_p, cmp_p, nkv, nkv, win, win, gates)


def _stack_heads(q_ref, col0, n_heads):
    return jnp.concatenate([q_ref[:, col0 + r * HEAD_DIM:col0 + (r + 1) * HEAD_DIM] for r in range(n_heads)], axis=0)


def _repeat_rows(a, times):
    return jnp.concatenate([jnp.broadcast_to(a[h:h + 1, :], (times, a.shape[1])) for h in range(a.shape[0])], axis=0)


def _fox_decode_body(pt_ref, q_ref, *refs, n_steps, pp, ts):
    kv_refs, lc_refs = refs[:pp], refs[pp:2 * pp]
    new_ref, lcn_ref, o_ref, qs_ref, carry_ref, m_ref, l_ref, acc_ref = refs[2 * pp:]
    p = pl.program_id(1)
    rr = FOX_GROUP
    rows = rr * ts

    @pl.when(p == 0)
    def _():
        _init_state((m_ref, l_ref, acc_ref))
        carry_ref[...] = jnp.zeros_like(carry_ref)
        for g in range(FOX_KV_HEADS):
            qs_ref[g] = _stack_heads(q_ref, g * rr * HEAD_DIM, rr).astype(BF16)

    def attend(pages, c_k, keep):
        bias = _repeat_rows(c_k * LOG2E, ts)
        s = jnp.concatenate(
            [jnp.concatenate([_dot_nt(qs_ref[g], _slab(pg, g, PAGE_SLABS, PAGE_SIZE).astype(BF16))
                              for pg in pages], axis=1)
             for g in range(FOX_KV_HEADS)], axis=0) * SCALE2 - bias
        if keep is not None:
            s = jnp.where(keep, s, NEG_INF)

        def pv(pb):
            out = []
            for g in range(FOX_KV_HEADS):
                acc = None
                for u, pg in enumerate(pages):
                    part = _dot(pb[g * rows:(g + 1) * rows, u * PAGE_SIZE:(u + 1) * PAGE_SIZE],
                                _slab(pg, FOX_KV_HEADS + g, PAGE_SLABS, PAGE_SIZE).astype(BF16))
                    acc = part if acc is None else acc + part
                out.append(acc)
            return jnp.concatenate(out, axis=0)

        _online_update(s, pv, m_ref, l_ref, acc_ref, guard=False)

    run = carry_ref[...]
    c_pages = []
    for u in range(pp):
        lc = lc_refs[u][...]
        c_pages.append(run + lc)
        run = run + lc[:, PAGE_SIZE - 1:PAGE_SIZE]
    carry_ref[...] = run
    attend(kv_refs, jnp.concatenate(c_pages, axis=1), None)

    @pl.when(p == n_steps - 1)
    def _():
        keep_i = jnp.where(_iota((ts, PAGE_SIZE), 1) <= _iota((ts, PAGE_SIZE), 0), 1.0, 0.0)
        keep = jnp.concatenate([keep_i] * FOX_HEADS, axis=0) > 0.5
        attend([new_ref], run + lcn_ref[...], keep)
        o = _normalize(acc_ref[...], l_ref[...])
        for h in range(FOX_HEADS):
            o_ref[:, h * HEAD_DIM:(h + 1) * HEAD_DIM] = o[h * ts:(h + 1) * ts, :]


def _fox_decode(pt_flat, q_s, cache_kv, lc_pool, new_pad, lc_new, db, ts, n_pages, pp):
    rows = FOX_HEADS * ts
    n_steps = n_pages // pp
    page = (PAGE_SIZE * PAGE_SLABS, HEAD_DIM)
    grid_spec = pltpu.PrefetchScalarGridSpec(
        num_scalar_prefetch=1,
        grid=(db, n_steps),
        in_specs=[pl.BlockSpec((None, ts, FOX_W), lambda b, p, pt: (b, 0, 0))]
        + _paged_specs(page, n_pages, pp)
        + _paged_specs((FOX_HEADS, PAGE_SIZE), n_pages, pp)
        + [pl.BlockSpec((None,) + page, lambda b, p, pt: (b, 0, 0)),
           pl.BlockSpec((None, FOX_HEADS, PAGE_SIZE), lambda b, p, pt: (b, 0, 0))],
        out_specs=pl.BlockSpec((None, ts, FOX_W), lambda b, p, pt: (b, 0, 0)),
        scratch_shapes=[pltpu.VMEM((FOX_KV_HEADS, FOX_GROUP * ts, HEAD_DIM), BF16),
                        pltpu.VMEM((FOX_HEADS, 1), F32),
                        pltpu.VMEM((rows, 1), F32), pltpu.VMEM((rows, 1), F32),
                        pltpu.VMEM((rows, HEAD_DIM), F32)],
    )
    return pl.pallas_call(
        functools.partial(_fox_decode_body, n_steps=n_steps, pp=pp, ts=ts),
        grid_spec=grid_spec,
        out_shape=jax.ShapeDtypeStruct((db, ts, FOX_W), F32),
        compiler_params=_params("parallel", "arbitrary"),
        name="fox_decode",
    )(pt_flat, q_s, *([cache_kv] * pp), *([lc_pool] * pp), new_pad, lc_new)


def _nsa_decode_body(pt_ref, qn_ref, qr_ref, cmp_ref, *refs, n_steps, pp, ts, past, nbp, wb):
    kv_refs = refs[:pp]
    (new_ref, win_ref, wnew_ref, g_ref, o_ref, nwin_ref,
     qrs_ref, oc_ref, ow_ref, sel_ref, m_ref, l_ref, acc_ref) = refs[pp:]
    p = pl.program_id(1)
    rr = NSA_GROUP
    rows = rr * ts
    gg = NSA_KV_HEADS
    tile_rows = lambda a, n: jnp.concatenate([a] * n, axis=0)
    qi_col = _iota((ts, 1), 0)

    @pl.when(p == 0)
    def _():
        _init_state((m_ref, l_ref, acc_ref))
        pos_q = past + qi_col
        blk = _iota((ts, nbp), 1)
        valid = (blk + 1) * NSA_BLOCK - 1 <= pos_q
        exists = blk * NSA_BLOCK < past + ts
        keep_c = tile_rows(jnp.where(valid, 1.0, 0.0), rr)
        cur = lax.shift_right_logical(pos_q, BLOCK_SHIFT)
        dpos = wb + qi_col - _iota((ts, wb), 1)
        keep_buf = jnp.where(dpos >= 0, jnp.where(dpos < NSA_WINDOW, 1.0, 0.0), 0.0)
        keep_new = jnp.where(_iota((ts, PAGE_SIZE), 1) <= qi_col, 1.0, 0.0)
        keep_w = tile_rows(jnp.concatenate([keep_buf, keep_new], axis=1), rr)
        for g in range(gg):
            qn = _stack_heads(qn_ref, g * rr * HEAD_DIM, rr).astype(BF16)
            qr = _stack_heads(qr_ref, g * rr * HEAD_DIM, rr).astype(BF16)
            qrs_ref[g] = qr
            kc = cmp_ref[:, g * HEAD_DIM:(g + 1) * HEAD_DIM].astype(BF16)
            vc = cmp_ref[:, NSA_KV_W + g * HEAD_DIM:NSA_KV_W + (g + 1) * HEAD_DIM].astype(BF16)
            p_c = _masked_softmax(_dot_nt(qn, kc) * SCALE2, keep_c, axis=-1)
            oc_ref[g * rows:(g + 1) * rows, :] = _dot(p_c.astype(BF16), vc)
            imp = p_c[0:ts]
            for r in range(1, rr):
                imp = imp + p_c[r * ts:(r + 1) * ts]
            score = jnp.where(blk == cur, SEL_FORCE, jnp.where(valid, imp, -1.0))
            score = jnp.where(exists, score, NEG_INF)
            sel = _select_blocks(score, min(NSA_TOP_N, (past + ts + NSA_BLOCK - 1) // NSA_BLOCK), axis=-1)
            sel_ref[g * rows:(g + 1) * rows, :] = tile_rows(sel, rr).astype(BF16)
            kw = _slab(win_ref, g, WIN_SLABS, wb).astype(BF16)
            vw = _slab(win_ref, gg + g, WIN_SLABS, wb).astype(BF16)
            kwn = _slab(wnew_ref, g, WIN_SLABS, PAGE_SIZE).astype(BF16)
            vwn = _slab(wnew_ref, gg + g, WIN_SLABS, PAGE_SIZE).astype(BF16)
            s_w = jnp.concatenate([_dot_nt(qr, kw), _dot_nt(qr, kwn)], axis=1) * SCALE2
            p_w = _masked_softmax(s_w, keep_w, axis=-1).astype(BF16)
            ow_ref[g * rows:(g + 1) * rows, :] = _dot(p_w[:, 0:wb], vw) + _dot(p_w[:, wb:wb + PAGE_SIZE], vwn)
        nwin_ref[0:(wb - ts) * WIN_SLABS, :] = win_ref[ts * WIN_SLABS:wb * WIN_SLABS, :]
        nwin_ref[(wb - ts) * WIN_SLABS:wb * WIN_SLABS, :] = wnew_ref[0:ts * WIN_SLABS, :]

    def attend(pages, key_start, keep_tok):
        n_keys = len(pages) * PAGE_SIZE
        keep = _expand_blocks(sel_ref[...], key_start, n_keys)
        if keep_tok is not None:
            keep = keep * keep_tok
        s = jnp.concatenate(
            [jnp.concatenate([_dot_nt(qrs_ref[g], _slab(pg, 2 * gg + g, PAGE_SLABS, PAGE_SIZE).astype(BF16))
                              for pg in pages], axis=1)
             for g in range(gg)], axis=0) * SCALE2
        s = jnp.where(keep > 0.5, s, NEG_INF)

        def pv(pb):
            out = []
            for g in range(gg):
                acc = None
                for u, pg in enumerate(pages):
                    part = _dot(pb[g * rows:(g + 1) * rows, u * PAGE_SIZE:(u + 1) * PAGE_SIZE],
                                _slab(pg, 3 * gg + g, PAGE_SLABS, PAGE_SIZE).astype(BF16))
                    acc = part if acc is None else acc + part
                out.append(acc)
            return jnp.concatenate(out, axis=0)

        _online_update(s, pv, m_ref, l_ref, acc_ref, guard=True)

    attend(kv_refs, p * (pp * PAGE_SIZE), None)

    @pl.when(p == n_steps - 1)
    def _():
        keep_tok = tile_rows(jnp.where(_iota((ts, PAGE_SIZE), 1) <= qi_col, 1.0, 0.0), gg * rr)
        attend([new_ref], past, keep_tok)
        o_s = _normalize(acc_ref[...], l_ref[...])
        gates = g_ref[...]

        def gate(br):
            return jnp.concatenate([gates[:, g * LANES + br * rr + r:g * LANES + br * rr + r + 1]
                                    for g in range(gg) for r in range(rr)], axis=0)

        o = gate(0) * oc_ref[...] + gate(1) * o_s + gate(2) * ow_ref[...]
        for h in range(NSA_HEADS):
            o_ref[:, h * HEAD_DIM:(h + 1) * HEAD_DIM] = o[h * ts:(h + 1) * ts, :]


def _nsa_decode(pt_flat, qn_s, qr_s, cmp_s, cache_nsa, new_pad, win_buf, wnew_pad, gates_s,
                db, ts, n_pages, pp):
    rows = NSA_HEADS * ts
    past = n_pages * PAGE_SIZE
    n_steps = n_pages // pp
    nbp = cmp_s.shape[1]
    wb = win_buf.shape[1] // WIN_SLABS
    page = (PAGE_SIZE * PAGE_SLABS, HEAD_DIM)
    whole = lambda shape: pl.BlockSpec((None,) + shape, lambda b, p, pt: (b,) + (0,) * len(shape))
    grid_spec = pltpu.PrefetchScalarGridSpec(
        num_scalar_prefetch=1,
        grid=(db, n_steps),
        in_specs=[whole((ts, NSA_W)), whole((ts, NSA_W)), whole((nbp, 2 * NSA_KV_W))]
        + _paged_specs(page, n_pages, pp)
        + [whole(page), whole((wb * WIN_SLABS, HEAD_DIM)), whole((PAGE_SIZE * WIN_SLABS, HEAD_DIM)),
           whole((ts, NSA_KV_HEADS * LANES))],
        out_specs=[whole((ts, NSA_W)), whole((wb * WIN_SLABS, HEAD_DIM))],
        scratch_shapes=[pltpu.VMEM((NSA_KV_HEADS, NSA_GROUP * ts, HEAD_DIM), BF16),
                        pltpu.VMEM((rows, HEAD_DIM), F32),
                        pltpu.VMEM((rows, HEAD_DIM), F32),
                        pltpu.VMEM((rows, nbp), BF16),
                        pltpu.VMEM((rows, 1), F32), pltpu.VMEM((rows, 1), F32),
                        pltpu.VMEM((rows, HEAD_DIM), F32)],
    )
    return pl.pallas_call(
        functools.partial(_nsa_decode_body, n_steps=n_steps, pp=pp, ts=ts, past=past, nbp=nbp, wb=wb),
        grid_spec=grid_spec,
        out_shape=[jax.ShapeDtypeStruct((db, ts, NSA_W), F32),
                   jax.ShapeDtypeStruct((db, wb * WIN_SLABS, HEAD_DIM), F32)],
        compiler_params=_params("parallel", "arbitrary"),
        name="nsa_decode",
    )(pt_flat, qn_s, qr_s, cmp_s, *([cache_nsa] * pp), new_pad, win_buf, wnew_pad, gates_s)


def _topk_rows(x, k):
    n = x.shape[0]
    row = _iota(x.shape, 0).astype(F32)
    vals, idxs = [], []
    for _ in range(k):
        mx = jnp.max(x, axis=0, keepdims=True)
        ix = jnp.min(jnp.where(x == mx, row, float(n)), axis=0, keepdims=True)
        x = jnp.where(row == ix, NEG_INF, x)
        vals.append(mx)
        idxs.append(ix)
    return jnp.concatenate(vals, axis=0), jnp.concatenate(idxs, axis=0)


def _peer_select_body(q_ref, keys_ref, a_ref, b_ref, g_ref):
    tm = q_ref.shape[0]
    kk = PEER_TOPK
    a_all, b_all, g_all = [], [], []
    for h in range(PEER_HEADS):
        sub = []
        for c in range(2):
            keys = keys_ref[h, c].astype(BF16)
            col0 = (h * 2 + c) * PEER_HALF
            sub.append(_topk_rows(_dot_nt(keys, q_ref[:, col0:col0 + PEER_HALF]), kk))
        (s1, i1), (s2, i2) = sub
        width = [kk // (i + 1) for i in range(kk)]
        n_cand = sum(width)
        pad = -(-n_cand // 8) * 8 - n_cand
        fill = lambda v: [jnp.full((pad, tm), v, F32)] if pad else []
        cand = jnp.concatenate([s1[i:i + 1] + s2[0:width[i]] for i in range(kk)] + fill(NEG_INF), axis=0)
        ca = jnp.concatenate([jnp.broadcast_to(i1[i:i + 1], (width[i], tm)) for i in range(kk)] + fill(0.0), axis=0)
        cb = jnp.concatenate([i2[0:width[i]] for i in range(kk)] + fill(0.0), axis=0)
        row = _iota(cand.shape, 0).astype(F32)
        tops, a_h, b_h = [], [], []
        for _ in range(kk):
            mx = jnp.max(cand, axis=0, keepdims=True)
            ix = jnp.min(jnp.where(cand == mx, row, float(cand.shape[0])), axis=0, keepdims=True)
            hit = row == ix
            a_h.append(jnp.sum(jnp.where(hit, ca, 0.0), axis=0, keepdims=True))
            b_h.append(jnp.sum(jnp.where(hit, cb, 0.0), axis=0, keepdims=True))
            cand = jnp.where(hit, NEG_INF, cand)
            tops.append(mx)
        top = jnp.concatenate(tops, axis=0)
        e = jnp.exp(top - top[0:1])
        g_all.append(e / jnp.sum(e, axis=0, keepdims=True))
        a_all.append(jnp.concatenate(a_h, axis=0))
        b_all.append(jnp.concatenate(b_h, axis=0))
    a_ref[...] = jnp.concatenate(a_all, axis=0).T
    b_ref[...] = jnp.concatenate(b_all, axis=0).T
    g_ref[...] = jnp.concatenate(g_all, axis=0).T


def _peer_select(q, peer_keys):
    n = q.shape[0]
    tm = _tile(n, LANES, LANES)
    slots = PEER_HEADS * PEER_TOPK
    out = jax.ShapeDtypeStruct((n, slots), F32)
    spec = pl.BlockSpec((tm, slots), lambda i: (i, 0))
    return pl.pallas_call(
        _peer_select_body,
        grid=(n // tm,),
        in_specs=[pl.BlockSpec((tm, q.shape[1]), lambda i: (i, 0)),
                  pl.BlockSpec(peer_keys.shape, lambda i: (0, 0, 0, 0))],
        out_specs=[spec, spec, spec],
        out_shape=[out, out, out],
        compiler_params=_params("parallel"),
        name="peer_select",
    )(q, peer_keys)


def _peer_coef_body(a_ref, b_ref, g_ref, act_ref, o_ref, w_ref):
    tt = a_ref.shape[0]
    slots = a_ref.shape[1]
    sub = _iota((PEER_KEYS, slots), 0).astype(F32).astype(BF16)
    one, zero = jnp.ones((), BF16), jnp.zeros((), BF16)
    for t in range(tt):
        g_row = g_ref[t:t + 1, :]
        g_hi = g_row.astype(BF16)
        g_lo = (g_row - g_hi.astype(F32)).astype(BF16)
        a_hot = jnp.where(sub == a_ref[t:t + 1, :].astype(BF16), one, zero)
        hit = sub == b_ref[t:t + 1, :].astype(BF16)
        b_hi = jnp.where(hit, g_hi, zero)
        b_lo = jnp.where(hit, g_lo, zero)
        w_ref[:, t, :] = _dot_nt(jnp.concatenate([a_hot, a_hot], axis=1),
                                 jnp.concatenate([b_hi, b_lo], axis=1))
    o_ref[...] = (w_ref[...] * act_ref[...]).astype(o_ref.dtype)


def _peer_coef(a_sel, b_sel, g_sel, act):
    n, slots = a_sel.shape
    tt = _tile(n, 16)
    spec = pl.BlockSpec((tt, slots), lambda i: (i, 0))
    cube = pl.BlockSpec((PEER_KEYS, tt, PEER_KEYS), lambda i: (0, i, 0))
    return pl.pallas_call(
        _peer_coef_body,
        grid=(n // tt,),
        in_specs=[spec, spec, spec, cube],
        out_specs=cube,
        out_shape=jax.ShapeDtypeStruct((PEER_KEYS, n, PEER_KEYS), BF16),
        scratch_shapes=[pltpu.VMEM((PEER_KEYS, tt, PEER_KEYS), F32)],
        compiler_params=_params("parallel"),
        name="peer_coef",
    )(a_sel, b_sel, g_sel, act)


def _rope_tables(pos):
    half = HEAD_DIM // 2
    inv = ROPE_THETA ** (-jnp.arange(half, dtype=F32) / half)
    ang = pos.astype(F32)[:, None] * inv[None, :]
    cos, sin = jnp.cos(ang), jnp.sin(ang)
    return jnp.concatenate([cos, cos], axis=-1), jnp.concatenate([-sin, sin], axis=-1)


def _split_w_in(w_in):
    d = w_in.shape[0]
    o_f = FOX_W + 2 * FOX_KV_W
    o_q = o_f + FOX_HEADS
    o_g = o_q + NSA_W + 6 * NSA_KV_W
    main = jnp.concatenate([w_in[:, :o_f], w_in[:, o_q:o_g]], axis=1)
    forget = jnp.pad(w_in[:, o_f:o_q], ((0, 0), (0, LANES - FOX_HEADS)))
    gates = w_in[:, o_g:].reshape(d, NSA_KV_HEADS, NSA_GROUP, NSA_BRANCHES)
    gates = jnp.transpose(gates, (0, 1, 3, 2)).reshape(d, NSA_KV_HEADS, NSA_BRANCHES * NSA_GROUP)
    gates = jnp.pad(gates, ((0, 0), (0, 0), (0, LANES - NSA_BRANCHES * NSA_GROUP)))
    return main, jnp.concatenate([forget, gates.reshape(d, NSA_KV_HEADS * LANES)], axis=1)


def _pad_axis1(a, size):
    return jnp.pad(a, ((0, 0), (0, size - a.shape[1])) + ((0, 0),) * (a.ndim - 2))


def _layer(x, p_ple, cache_fox_kv, cache_fox_logf, cache_nsa_kv, state_win, pt_flat, dims,
           g_mix, w_in, b_f, w_ck, w_cv, w_a, w_b, w_merge, b_merge, w_o,
           g_ffn, w_pq, peer_keys, peer_u, peer_v, g_ple, w_ple_gate, w_ple):
    b, t, db, ts, n_pages = dims
    n, d = x.shape
    n_p = b * t
    past = n_pages * PAGE_SIZE
    pp = _tile(n_pages, PAGES_PER_STEP, 1)

    h = _rmsnorm(x, g_mix, BF16)
    w_main, w_small = _split_w_in(w_in)
    proj = _matmul(h, w_main, out_dtype=F32, name="in_proj", **FULL_K)
    proj2 = _matmul(h, w_small, tm=1664, tn=384, tk=2048, out_dtype=F32, name="in_proj_small")
    pos = jnp.concatenate([jnp.tile(jnp.arange(t), b), jnp.tile(past + jnp.arange(ts), db)])
    cos, sin = _rope_tables(pos)
    bf_row = jnp.pad(b_f, (0, LANES - FOX_HEADS)).reshape(1, LANES)
    qf, fkv, qn, qr, nkv, win, logf, gates = _post_projection(proj, proj2, cos, sin, bf_row)

    lf_p_t = jnp.transpose(logf[:n_p].reshape(b, t, FOX_HEADS), (0, 2, 1))
    o_f_p = _fox_prompt(qf, fkv, _cumsum_prompt(lf_p_t), b, t)
    w_cmp = jnp.concatenate([jnp.tile(w_ck, (1, NSA_KV_HEADS)), jnp.tile(w_cv, (1, NSA_KV_HEADS))], axis=1)
    cmp_p = _compress_rows(nkv, w_cmp, n_p)
    o_n_p = _nsa_prompt(qn, qr, cmp_p, nkv, win, gates, b, t)

    sample = lambda a: a[n_p:].reshape(db, ts, a.shape[1])
    n_pool = cache_fox_logf.shape[0]
    lc_pool = _page_cumsum(jnp.transpose(cache_fox_logf, (0, 2, 1)).reshape(n_pool * FOX_HEADS, PAGE_SIZE))
    lf_new_t = jnp.pad(jnp.transpose(sample(logf), (0, 2, 1)), ((0, 0), (0, 0), (0, PAGE_SIZE - ts)))
    lc_new = _page_cumsum(lf_new_t.reshape(db * FOX_HEADS, PAGE_SIZE))
    slab_rows = lambda a: a.reshape(a.shape[0], -1, HEAD_DIM)
    fkv_new = slab_rows(_pad_axis1(sample(fkv), PAGE_SIZE))
    o_f_s = _fox_decode(pt_flat, sample(qf).astype(F32), slab_rows(cache_fox_kv),
                        lc_pool.reshape(n_pool, FOX_HEADS, PAGE_SIZE), fkv_new,
                        lc_new.reshape(db, FOX_HEADS, PAGE_SIZE), db, ts, n_pages, pp)

    cache_nsa = slab_rows(cache_nsa_kv)
    nkv_new = _pad_axis1(sample(nkv), PAGE_SIZE)
    cmp_new = _compress_rows(nkv_new.reshape(db * PAGE_SIZE, 4 * NSA_KV_W), w_cmp, db * PAGE_SIZE)
    cmp_s = jnp.concatenate([_compress_paged(pt_flat, cache_nsa, w_cmp, db, n_pages, pp),
                             cmp_new.reshape(db, PAGE_SIZE // NSA_BLOCK, 2 * NSA_KV_W)], axis=1)
    cmp_s = _pad_axis1(cmp_s, -(-cmp_s.shape[1] // LANES) * LANES)
    wb = state_win.shape[1]
    o_n_s, new_win = _nsa_decode(
        pt_flat, sample(qn).astype(F32), sample(qr).astype(F32), cmp_s, cache_nsa, slab_rows(nkv_new),
        slab_rows(state_win), slab_rows(_pad_axis1(sample(win), PAGE_SIZE)),
        sample(gates), db, ts, n_pages, pp)

    o_f = jnp.concatenate([o_f_p, o_f_s.reshape(db * ts, FOX_W).astype(BF16)], axis=0)
    o_n = jnp.concatenate([o_n_p, o_n_s.reshape(db * ts, NSA_W).astype(BF16)], axis=0)

    gate = _matmul(h, w_merge, out_dtype=F32, name="merge_gate", **FULL_K,
                   epilogue=lambda acc, bias: jax.nn.sigmoid(acc + bias), extra=[(b_merge.reshape(1, -1), "row", 0)])
    m_a = _matmul(o_f, w_a, tm=1664, tn=512, tk=2048, out_dtype=F32, name="merge_fox",
                  epilogue=lambda acc, ga: ga * acc, extra=[(gate, "tile", 0)])
    tn_b = _tile(d, 512, LANES)
    mixed = _matmul(o_n, w_b, tm=1664, tn=512, tk=2048, out_dtype=BF16, name="merge_nsa",
                    epilogue=lambda acc, gb, ma: ma + gb * acc,
                    extra=[(gate, "tile", d // tn_b), (m_a, "tile", 0)])
    x1 = _matmul(mixed, w_o, out_dtype=F32, name="out_proj", **FULL_K,
                 epilogue=lambda acc, res: res + acc, extra=[(x, "tile", 0)])

    h2 = _rmsnorm(x1, g_ffn, BF16)
    pq = _matmul(h2, w_pq, out_dtype=BF16, name="peer_query", **FULL_K)
    a_sel, b_sel, g_sel = _peer_select(pq, peer_keys)
    act = _matmul(h2, peer_u, out_dtype=F32, nt=True, split_out=True, epilogue=_gelu_tanh, name="peer_act",
                  **FULL_K)
    coef = _peer_coef(a_sel, b_sel, g_sel, act)
    x2 = _matmul(coef, peer_v, tm=1664, tn=512, tk=2048, out_dtype=F32, stacked_lhs=True, name="peer_out",
                 epilogue=lambda acc, res: res + acc, extra=[(x1, "tile", 0)])

    h3 = _rmsnorm(x2, g_ple, BF16)
    pw = _matmul(p_ple.astype(BF16), w_ple, tm=1664, tn=512, tk=2048, out_dtype=F32, name="ple_embed")
    x3 = _matmul(h3, w_ple_gate, out_dtype=F32, name="ple_gate", **FULL_K,
                 epilogue=lambda acc, res, pe: res + jax.nn.sigmoid(acc) * pe,
                 extra=[(x2, "tile", 0), (pw, "tile", 0)])

    caches = dict(
        fox_kv_p=fkv[:n_p].reshape(b, t, 2, FOX_KV_HEADS, HEAD_DIM),
        fox_lf_p=logf[:n_p].reshape(b, t, FOX_HEADS),
        nsa_kv_p=nkv[:n_p].reshape(b, t, 4, NSA_KV_HEADS, HEAD_DIM),
        nsa_win_p=win[:n_p].reshape(b, t, 2, NSA_KV_HEADS, HEAD_DIM)[:, t - min(NSA_WINDOW, t):],
        fox_kv_s=fkv[n_p:].reshape(db, ts, 2, FOX_KV_HEADS, HEAD_DIM),
        fox_lf_s=logf[n_p:].reshape(db, ts, FOX_HEADS),
        nsa_kv_s=nkv[n_p:].reshape(db, ts, 4, NSA_KV_HEADS, HEAD_DIM),
        nsa_win_s=new_win.reshape(db, wb, 2, NSA_KV_HEADS, HEAD_DIM),
    )
    return x3, caches


def kernel(x_prompt, x_sample, cache_fox_kv, cache_fox_logf, cache_nsa_kv, state_nsa_win, page_table, p_prompt, p_sample, g_mix, w_in, b_f, w_ck, w_cv, w_a, w_b, w_merge, b_merge, w_o, g_ffn, w_pq, peer_keys, peer_u, peer_v, g_ple, w_ple_gate, w_ple, g_final):
    b, t, d = x_prompt.shape
    db, ts, _ = x_sample.shape
    depth = g_mix.shape[0]
    n_pages = page_table.shape[1]
    n_p = b * t
    dims = (b, t, db, ts, n_pages)
    pt_flat = page_table.reshape(-1).astype(jnp.int32)
    x = jnp.concatenate([x_prompt.reshape(n_p, d), x_sample.reshape(db * ts, d)], axis=0)
    per_layer = []
    for i in range(depth):
        p_ple = jnp.concatenate([p_prompt[i].reshape(n_p, -1), p_sample[i].reshape(db * ts, -1)], axis=0)
        x, caches = _layer(x, p_ple, cache_fox_kv[i], cache_fox_logf[i], cache_nsa_kv[i], state_nsa_win[i],
                           pt_flat, dims, g_mix[i], w_in[i], b_f[i], w_ck[i], w_cv[i], w_a[i], w_b[i],
                           w_merge[i], b_merge[i], w_o[i], g_ffn[i], w_pq[i], peer_keys[i], peer_u[i],
                           peer_v[i], g_ple[i], w_ple_gate[i], w_ple[i])
        per_layer.append(caches)
    y_p = _rmsnorm(x, g_final, F32, 0, n_p)
    y_s = _rmsnorm(x, g_final, F32, n_p, db * ts)
    stack = lambda name: jnp.stack([c[name] for c in per_layer])
    return (y_p.reshape(b, t, d), y_s.reshape(db, ts, d),
            stack("fox_kv_p"), stack("fox_lf_p"), stack("nsa_kv_p"), stack("nsa_win_p"),
            stack("fox_kv_s"), stack("fox_lf_s"), stack("nsa_kv_s"), stack("nsa_win_s"))
```

```python
import functools

import jax
import jax.numpy as jnp
from jax import lax
from jax.experimental import pallas as pl
from jax.experimental.pallas import tpu as pltpu

F32 = jnp.float32
BF16 = jnp.bfloat16
NEG_INF = float("-inf")

HEAD_DIM = 128
FOX_HEADS = 16
FOX_KV_HEADS = 4
FOX_GROUP = FOX_HEADS // FOX_KV_HEADS
NSA_HEADS = 16
NSA_KV_HEADS = 2
NSA_GROUP = NSA_HEADS // NSA_KV_HEADS
NSA_BLOCK = 64
BLOCK_SHIFT = NSA_BLOCK.bit_length() - 1
NSA_TOP_N = 16
NSA_WINDOW = 512
NSA_BRANCHES = 3
SEL_FORCE = 1e4
ROPE_THETA = 10000.0
PAGE_SIZE = 128
PEER_HEADS = 8
PEER_KEYS = 128
PEER_TOPK = 16
PEER_HALF = 128
RMS_EPS = 1e-6
SCALE = HEAD_DIM ** -0.5
LOG2E = 1.4426950408889634
SCALE2 = SCALE * LOG2E

FOX_W = FOX_HEADS * HEAD_DIM
FOX_KV_W = FOX_KV_HEADS * HEAD_DIM
NSA_W = NSA_HEADS * HEAD_DIM
NSA_KV_W = NSA_KV_HEADS * HEAD_DIM
LANES = 128
VMEM_LIMIT = 56 * 1024 * 1024
PAGES_PER_STEP = 8
ATTN_PAGES_PER_STEP = 16
PAGE_SLABS = 8
WIN_SLABS = 2 * NSA_KV_HEADS

NT_DIMS = (((1,), (1,)), ((), ()))
FULL_K = dict(tm=832, tn=512, tk=4096)


def _params(*sem):
    return pltpu.CompilerParams(dimension_semantics=sem, vmem_limit_bytes=VMEM_LIMIT)


def _tile(n, target, align=8):
    if n <= target:
        return n
    best = None
    for t in range(align, target + 1, align):
        if n % t == 0:
            best = t
    assert best is not None, (n, target, align)
    return best


def _dot(a, b):
    return jnp.dot(a, b, preferred_element_type=F32)


def _dot_nt(a, b):
    return lax.dot_general(a, b, NT_DIMS, preferred_element_type=F32)


def _dot_tn(a, b):
    k = a.shape[0]
    kp = -(-k // LANES) * LANES
    if kp != k:
        a = jnp.concatenate([a, jnp.zeros((kp - k, a.shape[1]), a.dtype)], axis=0)
        b = jnp.concatenate([b, jnp.zeros((kp - k, b.shape[1]), b.dtype)], axis=0)
    return _dot(a.T.astype(BF16), b.astype(BF16))


def _iota(shape, dim, dtype=jnp.int32):
    return lax.broadcasted_iota(dtype, shape, dim)


def _rmsnorm_body(x_ref, g_ref, o_ref):
    x = x_ref[...]
    y = x * lax.rsqrt(jnp.mean(x * x, axis=-1, keepdims=True) + RMS_EPS)
    o_ref[...] = (y * g_ref[...]).astype(o_ref.dtype)


def _rmsnorm(x, g, out_dtype, row0=0, n=None):
    d = x.shape[1]
    n = x.shape[0] if n is None else n
    tr = _tile(n, 256)
    while row0 % tr:
        tr //= 2
    first = row0 // tr
    return pl.pallas_call(
        _rmsnorm_body,
        grid=(n // tr,),
        in_specs=[pl.BlockSpec((tr, d), lambda i: (first + i, 0)), pl.BlockSpec((1, d), lambda i: (0, 0))],
        out_specs=pl.BlockSpec((tr, d), lambda i: (i, 0)),
        out_shape=jax.ShapeDtypeStruct((n, d), out_dtype),
        compiler_params=_params("parallel"),
        name="rmsnorm",
    )(x, g.reshape(1, d))


def _gelu_tanh(x):
    return 0.5 * x * (1.0 + jnp.tanh(0.7978845608028654 * (x + 0.044715 * (x * x * x))))


def _mm_body(*refs, n_extra, epilogue, nt, nk, split_out, stacked_lhs):
    x_ref, w_ref = refs[0], refs[1]
    extra = refs[2:2 + n_extra]
    o_ref = refs[2 + n_extra]
    w = w_ref[...].astype(BF16)
    if stacked_lhs:
        x = jnp.concatenate([x_ref[j] for j in range(x_ref.shape[0])], axis=1)
    else:
        x = x_ref[...]
    part = _dot_nt(x, w) if nt else _dot(x, w)

    def finish(acc):
        res = epilogue(acc, *[e[...] for e in extra]).astype(o_ref.dtype)
        if split_out:
            for j in range(o_ref.shape[0]):
                o_ref[j] = res[:, j * LANES:(j + 1) * LANES]
        else:
            o_ref[...] = res

    if nk == 1:
        finish(part)
        return
    acc_ref = refs[3 + n_extra]
    k = pl.program_id(2)

    @pl.when(k == 0)
    def _():
        acc_ref[...] = part

    @pl.when(k > 0)
    def _():
        acc_ref[...] += part

    @pl.when(k == nk - 1)
    def _():
        finish(acc_ref[...])


def _matmul(x, w, *, tm, tn, tk, out_dtype, name, epilogue=None, extra=(), nt=False, split_out=False,
            stacked_lhs=False):
    if stacked_lhs:
        m, kdim = x.shape[1], x.shape[0] * LANES
    else:
        m, kdim = x.shape
    n = w.shape[0] if nt else w.shape[1]
    tm, tn, tk = _tile(m, tm), _tile(n, tn, LANES), _tile(kdim, tk, LANES)
    nk = kdim // tk
    if epilogue is None:
        epilogue = lambda acc: acc
    if stacked_lhs:
        x_spec = pl.BlockSpec((tk // LANES, tm, LANES), lambda i, j, k: (k, i, 0))
    else:
        x_spec = pl.BlockSpec((tm, tk), lambda i, j, k: (i, k))
    in_specs = [
        x_spec,
        pl.BlockSpec((tn, tk), lambda i, j, k: (j, k)) if nt else pl.BlockSpec((tk, tn), lambda i, j, k: (k, j)),
    ]
    args = [x, w]
    for arr, kind, off in extra:
        if kind == "tile":
            in_specs.append(pl.BlockSpec((tm, tn), lambda i, j, k, off=off: (i, j + off)))
        else:
            in_specs.append(pl.BlockSpec((1, tn), lambda i, j, k, off=off: (0, j + off)))
        args.append(arr)
    if split_out:
        out_shape = jax.ShapeDtypeStruct((n // LANES, m, LANES), out_dtype)
        out_spec = pl.BlockSpec((tn // LANES, tm, LANES), lambda i, j, k: (j, i, 0))
    else:
        out_shape = jax.ShapeDtypeStruct((m, n), out_dtype)
        out_spec = pl.BlockSpec((tm, tn), lambda i, j, k: (i, j))
    body = functools.partial(_mm_body, n_extra=len(extra), epilogue=epilogue, nt=nt, nk=nk,
                             split_out=split_out, stacked_lhs=stacked_lhs)
    return pl.pallas_call(
        body,
        grid=(m // tm, n // tn, nk),
        in_specs=in_specs,
        out_specs=out_spec,
        out_shape=out_shape,
        scratch_shapes=[pltpu.VMEM((tm, tn), F32)] if nk > 1 else [],
        compiler_params=_params("parallel", "parallel", "arbitrary"),
        name=name,
    )(*args)


def _rope(x, cos, sin_signed):
    return x * cos + pltpu.roll(x, HEAD_DIM // 2, 1) * sin_signed


def _log_sigmoid(z):
    return jnp.minimum(z, 0.0) - jnp.log1p(jnp.exp(-jnp.abs(z)))


def _post_body(p_ref, p2_ref, cos_ref, sin_ref, bf_ref,
               qf_ref, fkv_ref, qn_ref, qr_ref, nkv_ref, win_ref, logf_ref, gate_ref):
    cos = cos_ref[...]
    sin = sin_ref[...]
    o = 0
    qf_ref[...] = p_ref[:, o:o + FOX_W].astype(BF16)
    o += FOX_W
    fkv_ref[...] = p_ref[:, o:o + 2 * FOX_KV_W]
    o += 2 * FOX_KV_W
    for h in range(NSA_HEADS):
        q = p_ref[:, o + h * HEAD_DIM:o + (h + 1) * HEAD_DIM]
        qn_ref[:, h * HEAD_DIM:(h + 1) * HEAD_DIM] = q.astype(BF16)
        qr_ref[:, h * HEAD_DIM:(h + 1) * HEAD_DIM] = _rope(q, cos, sin).astype(BF16)
    o += NSA_W
    nkv_ref[:, 0:2 * NSA_KV_W] = p_ref[:, o:o + 2 * NSA_KV_W]
    o += 2 * NSA_KV_W
    for g in range(NSA_KV_HEADS):
        k = p_ref[:, o + g * HEAD_DIM:o + (g + 1) * HEAD_DIM]
        nkv_ref[:, 2 * NSA_KV_W + g * HEAD_DIM:2 * NSA_KV_W + (g + 1) * HEAD_DIM] = _rope(k, cos, sin)
    o += NSA_KV_W
    nkv_ref[:, 3 * NSA_KV_W:4 * NSA_KV_W] = p_ref[:, o:o + NSA_KV_W]
    o += NSA_KV_W
    for g in range(NSA_KV_HEADS):
        k = p_ref[:, o + g * HEAD_DIM:o + (g + 1) * HEAD_DIM]
        win_ref[:, g * HEAD_DIM:(g + 1) * HEAD_DIM] = _rope(k, cos, sin)
    o += NSA_KV_W
    win_ref[:, NSA_KV_W:2 * NSA_KV_W] = p_ref[:, o:o + NSA_KV_W]
    logf = _log_sigmoid(p2_ref[:, 0:LANES] + bf_ref[...])
    logf_ref[...] = logf[:, 0:FOX_HEADS]
    gate_ref[...] = jax.nn.sigmoid(p2_ref[:, LANES:3 * LANES])


def _post_projection(p, p2, cos, sin, bf_row):
    n = p.shape[0]
    tr = _tile(n, 256)
    row = lambda w: pl.BlockSpec((tr, w), lambda i: (i, 0))
    out_shapes = [
        jax.ShapeDtypeStruct((n, FOX_W), BF16),
        jax.ShapeDtypeStruct((n, 2 * FOX_KV_W), F32),
        jax.ShapeDtypeStruct((n, NSA_W), BF16),
        jax.ShapeDtypeStruct((n, NSA_W), BF16),
        jax.ShapeDtypeStruct((n, 4 * NSA_KV_W), F32),
        jax.ShapeDtypeStruct((n, 2 * NSA_KV_W), F32),
        jax.ShapeDtypeStruct((n, FOX_HEADS), F32),
        jax.ShapeDtypeStruct((n, 2 * LANES), F32),
    ]
    return pl.pallas_call(
        _post_body,
        grid=(n // tr,),
        in_specs=[row(p.shape[1]), row(p2.shape[1]), row(LANES), row(LANES),
                  pl.BlockSpec((1, LANES), lambda i: (0, 0))],
        out_specs=[row(s.shape[1]) for s in out_shapes],
        out_shape=out_shapes,
        compiler_params=_params("parallel"),
        name="post_projection",
    )(p, p2, cos, sin, bf_row)


def _split3(x):
    hi = x.astype(BF16)
    r1 = x - hi.astype(F32)
    mid = r1.astype(BF16)
    lo = (r1 - mid.astype(F32)).astype(BF16)
    return hi, mid, lo


def _row_cumsum(x, u):
    hi, mid, lo = _split3(x)
    return (_dot(hi, u) + _dot(mid, u)) + _dot(lo, u)


def _cumsum_prompt_body(lf_ref, u_ref, o_ref, carry_ref):
    @pl.when(pl.program_id(1) == 0)
    def _():
        carry_ref[...] = jnp.zeros_like(carry_ref)

    c = _row_cumsum(lf_ref[...], u_ref[...]) + carry_ref[...]
    o_ref[...] = c
    r = c.shape[1]
    carry_ref[...] = c[:, r - 1:r]


def _upper_ones(r):
    return (jnp.arange(r)[:, None] <= jnp.arange(r)[None, :]).astype(BF16)


def _cumsum_prompt(lf_t):
    b, h, t = lf_t.shape
    r = _tile(t, 512, LANES)
    return pl.pallas_call(
        _cumsum_prompt_body,
        grid=(b, t // r),
        in_specs=[pl.BlockSpec((None, h, r), lambda i, j: (i, 0, j)),
                  pl.BlockSpec((r, r), lambda i, j: (0, 0))],
        out_specs=pl.BlockSpec((None, h, r), lambda i, j: (i, 0, j)),
        out_shape=jax.ShapeDtypeStruct((b, h, t), F32),
        scratch_shapes=[pltpu.VMEM((h, 1), F32)],
        compiler_params=_params("parallel", "arbitrary"),
        name="cumsum_prompt",
    )(lf_t, _upper_ones(r))


def _page_cumsum_body(x_ref, u_ref, o_ref):
    o_ref[...] = _row_cumsum(x_ref[...], u_ref[...])


def _page_cumsum(x):
    rows = x.shape[0]
    tr = _tile(rows, 2048)
    return pl.pallas_call(
        _page_cumsum_body,
        grid=(rows // tr,),
        in_specs=[pl.BlockSpec((tr, PAGE_SIZE), lambda i: (i, 0)),
                  pl.BlockSpec((PAGE_SIZE, PAGE_SIZE), lambda i: (0, 0))],
        out_specs=pl.BlockSpec((tr, PAGE_SIZE), lambda i: (i, 0)),
        out_shape=jax.ShapeDtypeStruct((rows, PAGE_SIZE), F32),
        compiler_params=_params("parallel"),
        name="page_cumsum",
    )(x, _upper_ones(PAGE_SIZE))


def _online_update_t(s, vt, m_ref, l_ref, acc_ref, guard):
    m_old = m_ref[...]
    m_new = jnp.maximum(m_old, jnp.max(s, axis=0, keepdims=True))
    m_use = jnp.where(m_new == NEG_INF, 0.0, m_new) if guard else m_new
    alpha = jnp.exp2(m_old - m_use)
    p = jnp.exp2(s - m_use)
    l_ref[...] = alpha * l_ref[...] + jnp.sum(p, axis=0, keepdims=True)
    acc_ref[...] = alpha * acc_ref[...] + _dot(vt, p.astype(BF16))
    m_ref[...] = m_new


def _pipelined_chunks(lo, hi, last, qk, consume, s_a, s_b):
    s_a[...] = qk(lo)

    def step(j, src, dst):
        dst[...] = qk(jnp.minimum(j + 1, last))
        consume(j, src[...])

    def pair(t, carry):
        j = lo + 2 * t
        step(j, s_a, s_b)

        @pl.when(j + 1 < hi)
        def _():
            step(j + 1, s_b, s_a)

        return carry

    lax.fori_loop(0, (hi - lo + 1) // 2, pair, 0)


def _online_update(s, pv_fn, m_ref, l_ref, acc_ref, guard):
    m_old = m_ref[...]
    m_new = jnp.maximum(m_old, jnp.max(s, axis=-1, keepdims=True))
    m_use = jnp.where(m_new == NEG_INF, 0.0, m_new) if guard else m_new
    alpha = jnp.exp2(m_old - m_use)
    p = jnp.exp2(s - m_use)
    l_ref[...] = alpha * l_ref[...] + jnp.sum(p, axis=-1, keepdims=True)
    acc_ref[...] = alpha * acc_ref[...] + pv_fn(p.astype(BF16))
    m_ref[...] = m_new


def _masked_softmax(s, keep, axis):
    s = jnp.where(keep > 0.5, s, NEG_INF)
    m = jnp.max(s, axis=axis, keepdims=True)
    m = jnp.where(m == NEG_INF, 0.0, m)
    e = jnp.exp2(s - m)
    return e / jnp.maximum(jnp.sum(e, axis=axis, keepdims=True), 1e-30)


def _normalize(acc, l):
    return acc / jnp.maximum(l, 1e-30)


def _init_state(*triples):
    for m_ref, l_ref, acc_ref in triples:
        m_ref[...] = jnp.full_like(m_ref, NEG_INF)
        l_ref[...] = jnp.zeros_like(l_ref)
        acc_ref[...] = jnp.zeros_like(acc_ref)


def _mask_heads(s, keep, n_heads, width):
    return jnp.concatenate(
        [jnp.where(keep, s[:, r * width:(r + 1) * width], NEG_INF) for r in range(n_heads)], axis=1)


def _fox_prompt_body(q_ref, qx_ref, k_ref, kx_ref, v_ref, o_ref,
                     qa_ref, ka_ref, vt_ref, sa_ref, sb_ref, m_ref, l_ref, acc_ref, *, tq, nq):
    i = pl.program_id(2)
    rr = FOX_GROUP

    @pl.when(i == 0)
    def _():
        def prep(j, carry):
            ks = pl.multiple_of(j * tq, tq)
            ka_ref[j, :, 0:HEAD_DIM] = k_ref[pl.ds(ks, tq), :].astype(BF16)
            ka_ref[j, :, HEAD_DIM:2 * HEAD_DIM] = kx_ref[pl.ds(ks, tq), :]
            vt_ref[j] = v_ref[pl.ds(ks, tq), :].T.astype(BF16)
            return carry

        lax.fori_loop(0, nq, prep, 0)

    for r in range(rr):
        qa_ref[r * tq:(r + 1) * tq, 0:HEAD_DIM] = q_ref[:, r * HEAD_DIM:(r + 1) * HEAD_DIM]
        qa_ref[r * tq:(r + 1) * tq, HEAD_DIM:2 * HEAD_DIM] = qx_ref[r]
    _init_state((m_ref, l_ref, acc_ref))

    qk = lambda j: _dot_nt(ka_ref[j], qa_ref[...]) * SCALE2
    full = lambda j, s: _online_update_t(s, vt_ref[j], m_ref, l_ref, acc_ref, guard=False)
    _pipelined_chunks(0, i, i, qk, full, sa_ref, sb_ref)

    def diagonal(src):
        causal = _iota((tq, tq), 0) <= _iota((tq, tq), 1)
        _online_update_t(_mask_heads(src[...], causal, rr, tq), vt_ref[i], m_ref, l_ref, acc_ref, guard=False)

    pl.when(lax.rem(i, 2) == 0)(lambda: diagonal(sa_ref))
    pl.when(lax.rem(i, 2) == 1)(lambda: diagonal(sb_ref))

    o_t = _normalize(acc_ref[...], l_ref[...])
    for r in range(rr):
        o_ref[:, r * HEAD_DIM:(r + 1) * HEAD_DIM] = o_t[:, r * tq:(r + 1) * tq].T.astype(o_ref.dtype)


def _fox_bias_columns(c_t, b, t):
    g_, r_ = FOX_KV_HEADS, FOX_GROUP
    pieces = jnp.stack(_split3(c_t * (1.0 / SCALE)), axis=-1).reshape(b, g_, r_, t, 3)
    kx = -jnp.transpose(pieces, (0, 1, 3, 2, 4)).reshape(b, g_, t, 3 * r_)
    kx = jnp.concatenate([kx, jnp.ones((b, g_, t, 3), BF16)], axis=-1)
    kx = jnp.pad(kx, ((0, 0), (0, 0), (0, 0), (0, LANES - kx.shape[-1])))
    own = (jnp.arange(r_)[:, None] == jnp.arange(3 * r_)[None, :] // 3).astype(BF16)
    own = jnp.broadcast_to(own[None, None, :, None, :], (b, g_, r_, t, 3 * r_))
    qx = jnp.concatenate([own, pieces], axis=-1)
    qx = jnp.pad(qx, ((0, 0), (0, 0), (0, 0), (0, 0), (0, LANES - qx.shape[-1])))
    return qx, kx


def _fox_prompt(qf, fkv, c_t, b, t):
    tq = _tile(t, 256, LANES)
    nq = t // tq
    g_, r_ = FOX_KV_HEADS, FOX_GROUP
    rows = r_ * tq
    qx, kx = _fox_bias_columns(c_t, b, t)
    return pl.pallas_call(
        functools.partial(_fox_prompt_body, tq=tq, nq=nq),
        grid=(b, g_, nq),
        in_specs=[
            pl.BlockSpec((tq, r_ * HEAD_DIM), lambda bi, g, i: (bi * nq + i, g)),
            pl.BlockSpec((None, None, r_, tq, LANES), lambda bi, g, i: (bi, g, 0, i, 0)),
            pl.BlockSpec((t, HEAD_DIM), lambda bi, g, i: (bi, g)),
            pl.BlockSpec((None, None, t, LANES), lambda bi, g, i: (bi, g, 0, 0)),
            pl.BlockSpec((t, HEAD_DIM), lambda bi, g, i: (bi, g_ + g)),
        ],
        out_specs=pl.BlockSpec((tq, r_ * HEAD_DIM), lambda bi, g, i: (bi * nq + i, g)),
        out_shape=jax.ShapeDtypeStruct((b * t, FOX_W), BF16),
        scratch_shapes=[pltpu.VMEM((rows, 2 * HEAD_DIM), BF16),
                        pltpu.VMEM((nq, tq, 2 * HEAD_DIM), BF16),
                        pltpu.VMEM((nq, HEAD_DIM, tq), BF16),
                        pltpu.VMEM((tq, rows), F32), pltpu.VMEM((tq, rows), F32),
                        pltpu.VMEM((1, rows), F32), pltpu.VMEM((1, rows), F32),
                        pltpu.VMEM((HEAD_DIM, rows), F32)],
        compiler_params=_params("parallel", "parallel", "arbitrary"),
        name="fox_prompt",
    )(qf, qx, fkv, kx, fkv)


def _compress_body(x_ref, w_ref, o_ref):
    x = x_ref[...]
    nb = x.shape[0] // NSA_BLOCK
    xb = x.reshape(nb, NSA_BLOCK, x.shape[1]) * w_ref[...][None]
    o_ref[...] = jnp.sum(xb, axis=1) * (1.0 / NSA_BLOCK)


def _compress_rows(nkv, w_cmp, n_rows):
    width = 2 * NSA_KV_W
    tr = _tile(n_rows, 512, NSA_BLOCK * 8)
    return pl.pallas_call(
        _compress_body,
        grid=(n_rows // tr,),
        in_specs=[pl.BlockSpec((tr, width), lambda i: (i, 0)),
                  pl.BlockSpec((NSA_BLOCK, width), lambda i: (0, 0))],
        out_specs=pl.BlockSpec((tr // NSA_BLOCK, width), lambda i: (i, 0)),
        out_shape=jax.ShapeDtypeStruct((n_rows // NSA_BLOCK, width), F32),
        compiler_params=_params("parallel"),
        name="compress_rows",
    )(nkv, w_cmp)


def _compress_paged_body(pt_ref, *refs, pp):
    page_refs, w_ref, o_ref = refs[:pp], refs[pp], refs[pp + 1]
    nb = PAGE_SIZE // NSA_BLOCK
    for u in range(pp):
        for j in range(2 * NSA_KV_HEADS):
            x = _slab(page_refs[u], j, PAGE_SLABS, PAGE_SIZE)
            xb = x.reshape(nb, NSA_BLOCK, HEAD_DIM) * w_ref[:, j * HEAD_DIM:(j + 1) * HEAD_DIM][None]
            o_ref[u, :, j * HEAD_DIM:(j + 1) * HEAD_DIM] = jnp.sum(xb, axis=1) * (1.0 / NSA_BLOCK)


def _paged_specs(block, n_pages, pp):
    return [pl.BlockSpec((None,) + block, lambda b, p, pt, u=u: (pt[b * n_pages + p * pp + u], 0, 0))
            for u in range(pp)]


def _slab(ref, j, n_slabs, n_rows):
    return ref[pl.ds(j, n_rows, stride=n_slabs), :]


def _compress_paged(pt_flat, cache_nsa, w_cmp, db, n_pages, pp):
    width = 2 * NSA_KV_W
    nb = PAGE_SIZE // NSA_BLOCK
    grid_spec = pltpu.PrefetchScalarGridSpec(
        num_scalar_prefetch=1,
        grid=(db, n_pages // pp),
        in_specs=_paged_specs((PAGE_SIZE * PAGE_SLABS, HEAD_DIM), n_pages, pp)
        + [pl.BlockSpec((NSA_BLOCK, width), lambda b, p, pt: (0, 0))],
        out_specs=pl.BlockSpec((None, pp, nb, width), lambda b, p, pt: (b, p, 0, 0)),
    )
    out = pl.pallas_call(
        functools.partial(_compress_paged_body, pp=pp),
        grid_spec=grid_spec,
        out_shape=jax.ShapeDtypeStruct((db, n_pages, nb, width), F32),
        compiler_params=_params("parallel", "arbitrary"),
        name="compress_paged",
    )(pt_flat, *([cache_nsa] * pp), w_cmp)
    return out.reshape(db, n_pages * nb, width)


def _select_blocks(score, n_sel, axis):
    axis = axis % score.ndim
    nb = score.shape[axis]
    blk = _iota(score.shape, axis).astype(F32)
    sel = jnp.zeros(score.shape, F32)
    work = score
    for _ in range(n_sel):
        mx = jnp.max(work, axis=axis, keepdims=True)
        ix = jnp.min(jnp.where(work == mx, blk, float(nb)), axis=axis, keepdims=True)
        hit = blk == ix
        sel = jnp.where(hit, jnp.where(mx >= 0.0, 1.0, 0.0), sel)
        work = jnp.where(hit, NEG_INF, work)
    return sel


def _expand_blocks(sel_b, key_start, n_keys):
    nb = sel_b.shape[1]
    key_blk = lax.shift_right_logical(key_start + _iota((nb, n_keys), 1), BLOCK_SHIFT)
    e = jnp.where(_iota((nb, n_keys), 0) == key_blk, 1.0, 0.0).astype(BF16)
    return _dot(sel_b, e)


def _nsa_prompt_body(qn_ref, qr_ref, kc_ref, vc_ref, ks_ref, vs_ref, kw_ref, vw_ref, g_ref, o_ref,
                     qs_ref, ksb_ref, vst_ref, kwb_ref, vwt_ref, sel_ref, oc_ref, sa_ref, sb_ref,
                     m_s, l_s, acc_s, m_w, l_w, acc_w, *, tq, nq, nb):
    i = pl.program_id(2)
    rr = NSA_GROUP
    bpc = tq // NSA_BLOCK

    @pl.when(i == 0)
    def _():
        def prep(j, carry):
            ks = pl.multiple_of(j * tq, tq)
            ksb_ref[j] = ks_ref[pl.ds(ks, tq), :].astype(BF16)
            vst_ref[j] = vs_ref[pl.ds(ks, tq), :].T.astype(BF16)
            kwb_ref[j] = kw_ref[pl.ds(ks, tq), :].astype(BF16)
            vwt_ref[j] = vw_ref[pl.ds(ks, tq), :].T.astype(BF16)
            return carry

        lax.fori_loop(0, nq, prep, 0)

    stack = lambda ref: jnp.concatenate([ref[:, r * HEAD_DIM:(r + 1) * HEAD_DIM] for r in range(rr)], axis=0)
    lanes = lambda a: jnp.concatenate([a] * rr, axis=1)
    qn = stack(qn_ref)
    qs_ref[...] = stack(qr_ref)
    pos_q = i * tq + _iota((1, tq), 1)
    _init_state((m_s, l_s, acc_s), (m_w, l_w, acc_w))

    blk = _iota((nb, tq), 0)
    valid = (blk + 1) * NSA_BLOCK - 1 <= pos_q
    s_c = _dot_nt(kc_ref[...].astype(BF16), qn) * SCALE2
    p_c = _masked_softmax(s_c, lanes(jnp.where(valid, 1.0, 0.0)), axis=0)
    oc_ref[...] = _dot_tn(vc_ref[...], p_c)
    imp = p_c[:, 0:tq]
    for r in range(1, rr):
        imp = imp + p_c[:, r * tq:(r + 1) * tq]
    cur = lax.shift_right_logical(pos_q, BLOCK_SHIFT)
    score = jnp.where(blk == cur, SEL_FORCE, jnp.where(valid, imp, -1.0))
    sel_ref[...] = _select_blocks(score, min(NSA_TOP_N, nb), axis=0)

    def sel_keep(j):
        return jnp.concatenate(
            [jnp.broadcast_to(sel_ref[pl.ds(j * bpc + u, 1), :], (NSA_BLOCK, tq)) for u in range(bpc)], axis=0)

    def sel_chunk(j, s):
        _online_update_t(_mask_heads(s, sel_keep(j) > 0.5, rr, tq), vst_ref[j], m_s, l_s, acc_s, guard=True)

    _pipelined_chunks(0, i, i, lambda j: _dot_nt(ksb_ref[j], qs_ref[...]) * SCALE2, sel_chunk, sa_ref, sb_ref)

    def sel_diagonal(src):
        causal = _iota((tq, tq), 0) <= _iota((tq, tq), 1)
        keep = jnp.where(causal, sel_keep(i), 0.0) > 0.5
        _online_update_t(_mask_heads(src[...], keep, rr, tq), vst_ref[i], m_s, l_s, acc_s, guard=True)

    pl.when(lax.rem(i, 2) == 0)(lambda: sel_diagonal(sa_ref))
    pl.when(lax.rem(i, 2) == 1)(lambda: sel_diagonal(sb_ref))

    def win_chunk(j, s):
        dpos = pos_q - (j * tq + _iota((tq, 1), 0))
        keep = jnp.where(dpos >= 0, jnp.where(dpos < NSA_WINDOW, 1.0, 0.0), 0.0) > 0.5
        _online_update_t(_mask_heads(s, keep, rr, tq), vwt_ref[j], m_w, l_w, acc_w, guard=True)

    _pipelined_chunks(jnp.maximum(i - (NSA_WINDOW + tq - 1) // tq, 0), i + 1, i,
                      lambda j: _dot_nt(kwb_ref[j], qs_ref[...]) * SCALE2, win_chunk, sa_ref, sb_ref)

    g_t = g_ref[...].T
    o_s = _normalize(acc_s[...], l_s[...])
    o_w = _normalize(acc_w[...], l_w[...])
    for r in range(rr):
        cols = slice(r * tq, (r + 1) * tq)
        gate = lambda br: g_t[br * rr + r:br * rr + r + 1, :]
        o = gate(0) * oc_ref[:, cols] + gate(1) * o_s[:, cols] + gate(2) * o_w[:, cols]
        o_ref[:, r * HEAD_DIM:(r + 1) * HEAD_DIM] = o.T.astype(o_ref.dtype)


def _nsa_prompt(qn, qr, cmp_p, nkv, win, gates, b, t):
    tq = _tile(t, 256, LANES)
    nq = t // tq
    nb = t // NSA_BLOCK
    g_, rows = NSA_KV_HEADS, NSA_GROUP * tq
    qspec = pl.BlockSpec((tq, NSA_GROUP * HEAD_DIM), lambda bi, g, i: (bi * nq + i, g))
    col = lambda off: pl.BlockSpec((t, HEAD_DIM), lambda bi, g, i: (bi, off + g))
    state = [pltpu.VMEM((1, rows), F32), pltpu.VMEM((1, rows), F32), pltpu.VMEM((HEAD_DIM, rows), F32)]
    return pl.pallas_call(
        functools.partial(_nsa_prompt_body, tq=tq, nq=nq, nb=nb),
        grid=(b, g_, nq),
        in_specs=[
            qspec, qspec,
            pl.BlockSpec((nb, HEAD_DIM), lambda bi, g, i: (bi, g)),
            pl.BlockSpec((nb, HEAD_DIM), lambda bi, g, i: (bi, g_ + g)),
            col(2 * g_), col(3 * g_),
            col(0), col(g_),
            pl.BlockSpec((tq, LANES), lambda bi, g, i: (bi * nq + i, g)),
        ],
        out_specs=qspec,
        out_shape=jax.ShapeDtypeStruct((b * t, NSA_W), BF16),
        scratch_shapes=[pltpu.VMEM((rows, HEAD_DIM), BF16),
                        pltpu.VMEM((nq, tq, HEAD_DIM), BF16), pltpu.VMEM((nq, HEAD_DIM, tq), BF16),
                        pltpu.VMEM((nq, tq, HEAD_DIM), BF16), pltpu.VMEM((nq, HEAD_DIM, tq), BF16),
                        pltpu.VMEM((nb, tq), F32),
                        pltpu.VMEM((HEAD_DIM, rows), F32),
                        pltpu.VMEM((tq, rows), F32), pltpu.VMEM((tq, rows), F32)] + state + state,
        compiler_params=_params("parallel", "parallel", "arbitrary"),
        name="nsa_prompt",
    )(qn, qr, cmp_p, cmp_p, nkv, nkv, win, win, gates)


def _stack_heads(q_ref, col0, n_heads):
    return jnp.concatenate([q_ref[:, col0 + r * HEAD_DIM:col0 + (r + 1) * HEAD_DIM] for r in range(n_heads)], axis=0)


def _repeat_rows(a, times):
    return jnp.concatenate([jnp.broadcast_to(a[h:h + 1, :], (times, a.shape[1])) for h in range(a.shape[0])], axis=0)


def _fox_decode_body(pt_ref, q_ref, *refs, n_steps, pp, ts):
    kv_refs, lc_refs = refs[:pp], refs[pp:2 * pp]
    new_ref, lcn_ref, o_ref, qs_ref, carry_ref, m_ref, l_ref, acc_ref = refs[2 * pp:]
    p = pl.program_id(1)
    rr = FOX_GROUP
    rows = rr * ts

    @pl.when(p == 0)
    def _():
        _init_state((m_ref, l_ref, acc_ref))
        carry_ref[...] = jnp.zeros_like(carry_ref)
        for g in range(FOX_KV_HEADS):
            qs_ref[g] = _stack_heads(q_ref, g * rr * HEAD_DIM, rr).astype(BF16)

    def attend(pages, c_k, keep):
        bias = _repeat_rows(c_k * LOG2E, ts)
        s = jnp.concatenate(
            [jnp.concatenate([_dot_nt(qs_ref[g], _slab(pg, g, PAGE_SLABS, PAGE_SIZE).astype(BF16))
                              for pg in pages], axis=1)
             for g in range(FOX_KV_HEADS)], axis=0) * SCALE2 - bias
        if keep is not None:
            s = jnp.where(keep, s, NEG_INF)

        def pv(pb):
            out = []
            for g in range(FOX_KV_HEADS):
                acc = None
                for u, pg in enumerate(pages):
                    part = _dot(pb[g * rows:(g + 1) * rows, u * PAGE_SIZE:(u + 1) * PAGE_SIZE],
                                _slab(pg, FOX_KV_HEADS + g, PAGE_SLABS, PAGE_SIZE).astype(BF16))
                    acc = part if acc is None else acc + part
                out.append(acc)
            return jnp.concatenate(out, axis=0)

        _online_update(s, pv, m_ref, l_ref, acc_ref, guard=False)

    run = carry_ref[...]
    c_pages = []
    for u in range(pp):
        lc = lc_refs[u][...]
        c_pages.append(run + lc)
        run = run + lc[:, PAGE_SIZE - 1:PAGE_SIZE]
    carry_ref[...] = run
    attend(kv_refs, jnp.concatenate(c_pages, axis=1), None)

    @pl.when(p == n_steps - 1)
    def _():
        keep_i = jnp.where(_iota((ts, PAGE_SIZE), 1) <= _iota((ts, PAGE_SIZE), 0), 1.0, 0.0)
        keep = jnp.concatenate([keep_i] * FOX_HEADS, axis=0) > 0.5
        attend([new_ref], run + lcn_ref[...], keep)
        o = _normalize(acc_ref[...], l_ref[...])
        for h in range(FOX_HEADS):
            o_ref[:, h * HEAD_DIM:(h + 1) * HEAD_DIM] = o[h * ts:(h + 1) * ts, :]


def _fox_decode(pt_flat, q_s, cache_kv, lc_pool, new_pad, lc_new, db, ts, n_pages, pp):
    rows = FOX_HEADS * ts
    n_steps = n_pages // pp
    page = (PAGE_SIZE * PAGE_SLABS, HEAD_DIM)
    grid_spec = pltpu.PrefetchScalarGridSpec(
        num_scalar_prefetch=1,
        grid=(db, n_steps),
        in_specs=[pl.BlockSpec((None, ts, FOX_W), lambda b, p, pt: (b, 0, 0))]
        + _paged_specs(page, n_pages, pp)
        + _paged_specs((FOX_HEADS, PAGE_SIZE), n_pages, pp)
        + [pl.BlockSpec((None,) + page, lambda b, p, pt: (b, 0, 0)),
           pl.BlockSpec((None, FOX_HEADS, PAGE_SIZE), lambda b, p, pt: (b, 0, 0))],
        out_specs=pl.BlockSpec((None, ts, FOX_W), lambda b, p, pt: (b, 0, 0)),
        scratch_shapes=[pltpu.VMEM((FOX_KV_HEADS, FOX_GROUP * ts, HEAD_DIM), BF16),
                        pltpu.VMEM((FOX_HEADS, 1), F32),
                        pltpu.VMEM((rows, 1), F32), pltpu.VMEM((rows, 1), F32),
                        pltpu.VMEM((rows, HEAD_DIM), F32)],
    )
    return pl.pallas_call(
        functools.partial(_fox_decode_body, n_steps=n_steps, pp=pp, ts=ts),
        grid_spec=grid_spec,
        out_shape=jax.ShapeDtypeStruct((db, ts, FOX_W), F32),
        compiler_params=_params("parallel", "arbitrary"),
        name="fox_decode",
    )(pt_flat, q_s, *([cache_kv] * pp), *([lc_pool] * pp), new_pad, lc_new)


def _nsa_decode_body(pt_ref, qn_ref, qr_ref, cmp_ref, *refs, n_steps, pp, ts, past, nbp, wb):
    kv_refs = refs[:pp]
    (new_ref, win_ref, wnew_ref, g_ref, o_ref, nwin_ref,
     qrs_ref, oc_ref, ow_ref, sel_ref, m_ref, l_ref, acc_ref) = refs[pp:]
    p = pl.program_id(1)
    rr = NSA_GROUP
    rows = rr * ts
    gg = NSA_KV_HEADS
    tile_rows = lambda a, n: jnp.concatenate([a] * n, axis=0)
    qi_col = _iota((ts, 1), 0)

    @pl.when(p == 0)
    def _():
        _init_state((m_ref, l_ref, acc_ref))
        pos_q = past + qi_col
        blk = _iota((ts, nbp), 1)
        valid = (blk + 1) * NSA_BLOCK - 1 <= pos_q
        exists = blk * NSA_BLOCK < past + ts
        keep_c = tile_rows(jnp.where(valid, 1.0, 0.0), rr)
        cur = lax.shift_right_logical(pos_q, BLOCK_SHIFT)
        dpos = wb + qi_col - _iota((ts, wb), 1)
        keep_buf = jnp.where(dpos >= 0, jnp.where(dpos < NSA_WINDOW, 1.0, 0.0), 0.0)
        keep_new = jnp.where(_iota((ts, PAGE_SIZE), 1) <= qi_col, 1.0, 0.0)
        keep_w = tile_rows(jnp.concatenate([keep_buf, keep_new], axis=1), rr)
        for g in range(gg):
            qn = _stack_heads(qn_ref, g * rr * HEAD_DIM, rr).astype(BF16)
            qr = _stack_heads(qr_ref, g * rr * HEAD_DIM, rr).astype(BF16)
            qrs_ref[g] = qr
            kc = cmp_ref[:, g * HEAD_DIM:(g + 1) * HEAD_DIM].astype(BF16)
            vc = cmp_ref[:, NSA_KV_W + g * HEAD_DIM:NSA_KV_W + (g + 1) * HEAD_DIM].astype(BF16)
            p_c = _masked_softmax(_dot_nt(qn, kc) * SCALE2, keep_c, axis=-1)
            oc_ref[g * rows:(g + 1) * rows, :] = _dot(p_c.astype(BF16), vc)
            imp = p_c[0:ts]
            for r in range(1, rr):
                imp = imp + p_c[r * ts:(r + 1) * ts]
            score = jnp.where(blk == cur, SEL_FORCE, jnp.where(valid, imp, -1.0))
            score = jnp.where(exists, score, NEG_INF)
            sel = _select_blocks(score, min(NSA_TOP_N, (past + ts + NSA_BLOCK - 1) // NSA_BLOCK), axis=-1)
            sel_ref[g * rows:(g + 1) * rows, :] = tile_rows(sel, rr).astype(BF16)
            kw = _slab(win_ref, g, WIN_SLABS, wb).astype(BF16)
            vw = _slab(win_ref, gg + g, WIN_SLABS, wb).astype(BF16)
            kwn = _slab(wnew_ref, g, WIN_SLABS, PAGE_SIZE).astype(BF16)
            vwn = _slab(wnew_ref, gg + g, WIN_SLABS, PAGE_SIZE).astype(BF16)
            s_w = jnp.concatenate([_dot_nt(qr, kw), _dot_nt(qr, kwn)], axis=1) * SCALE2
            p_w = _masked_softmax(s_w, keep_w, axis=-1).astype(BF16)
            ow_ref[g * rows:(g + 1) * rows, :] = _dot(p_w[:, 0:wb], vw) + _dot(p_w[:, wb:wb + PAGE_SIZE], vwn)
        nwin_ref[0:(wb - ts) * WIN_SLABS, :] = win_ref[ts * WIN_SLABS:wb * WIN_SLABS, :]
        nwin_ref[(wb - ts) * WIN_SLABS:wb * WIN_SLABS, :] = wnew_ref[0:ts * WIN_SLABS, :]

    def attend(pages, key_start, keep_tok):
        n_keys = len(pages) * PAGE_SIZE
        keep = _expand_blocks(sel_ref[...], key_start, n_keys)
        if keep_tok is not None:
            keep = keep * keep_tok
        s = jnp.concatenate(
            [jnp.concatenate([_dot_nt(qrs_ref[g], _slab(pg, 2 * gg + g, PAGE_SLABS, PAGE_SIZE).astype(BF16))
                              for pg in pages], axis=1)
             for g in range(gg)], axis=0) * SCALE2
        s = jnp.where(keep > 0.5, s, NEG_INF)

        def pv(pb):
            out = []
            for g in range(gg):
                acc = None
                for u, pg in enumerate(pages):
                    part = _dot(pb[g * rows:(g + 1) * rows, u * PAGE_SIZE:(u + 1) * PAGE_SIZE],
                                _slab(pg, 3 * gg + g, PAGE_SLABS, PAGE_SIZE).astype(BF16))
                    acc = part if acc is None else acc + part
                out.append(acc)
            return jnp.concatenate(out, axis=0)

        _online_update(s, pv, m_ref, l_ref, acc_ref, guard=True)

    attend(kv_refs, p * (pp * PAGE_SIZE), None)

    @pl.when(p == n_steps - 1)
    def _():
        keep_tok = tile_rows(jnp.where(_iota((ts, PAGE_SIZE), 1) <= qi_col, 1.0, 0.0), gg * rr)
        attend([new_ref], past, keep_tok)
        o_s = _normalize(acc_ref[...], l_ref[...])
        gates = g_ref[...]

        def gate(br):
            return jnp.concatenate([gates[:, g * LANES + br * rr + r:g * LANES + br * rr + r + 1]
                                    for g in range(gg) for r in range(rr)], axis=0)

        o = gate(0) * oc_ref[...] + gate(1) * o_s + gate(2) * ow_ref[...]
        for h in range(NSA_HEADS):
            o_ref[:, h * HEAD_DIM:(h + 1) * HEAD_DIM] = o[h * ts:(h + 1) * ts, :]


def _nsa_decode(pt_flat, qn_s, qr_s, cmp_s, cache_nsa, new_pad, win_buf, wnew_pad, gates_s,
                db, ts, n_pages, pp):
    rows = NSA_HEADS * ts
    past = n_pages * PAGE_SIZE
    n_steps = n_pages // pp
    nbp = cmp_s.shape[1]
    wb = win_buf.shape[1] // WIN_SLABS
    page = (PAGE_SIZE * PAGE_SLABS, HEAD_DIM)
    whole = lambda shape: pl.BlockSpec((None,) + shape, lambda b, p, pt: (b,) + (0,) * len(shape))
    grid_spec = pltpu.PrefetchScalarGridSpec(
        num_scalar_prefetch=1,
        grid=(db, n_steps),
        in_specs=[whole((ts, NSA_W)), whole((ts, NSA_W)), whole((nbp, 2 * NSA_KV_W))]
        + _paged_specs(page, n_pages, pp)
        + [whole(page), whole((wb * WIN_SLABS, HEAD_DIM)), whole((PAGE_SIZE * WIN_SLABS, HEAD_DIM)),
           whole((ts, NSA_KV_HEADS * LANES))],
        out_specs=[whole((ts, NSA_W)), whole((wb * WIN_SLABS, HEAD_DIM))],
        scratch_shapes=[pltpu.VMEM((NSA_KV_HEADS, NSA_GROUP * ts, HEAD_DIM), BF16),
                        pltpu.VMEM((rows, HEAD_DIM), F32),
                        pltpu.VMEM((rows, HEAD_DIM), F32),
                        pltpu.VMEM((rows, nbp), BF16),
                        pltpu.VMEM((rows, 1), F32), pltpu.VMEM((rows, 1), F32),
                        pltpu.VMEM((rows, HEAD_DIM), F32)],
    )
    return pl.pallas_call(
        functools.partial(_nsa_decode_body, n_steps=n_steps, pp=pp, ts=ts, past=past, nbp=nbp, wb=wb),
        grid_spec=grid_spec,
        out_shape=[jax.ShapeDtypeStruct((db, ts, NSA_W), F32),
                   jax.ShapeDtypeStruct((db, wb * WIN_SLABS, HEAD_DIM), F32)],
        compiler_params=_params("parallel", "arbitrary"),
        name="nsa_decode",
    )(pt_flat, qn_s, qr_s, cmp_s, *([cache_nsa] * pp), new_pad, win_buf, wnew_pad, gates_s)


def _topk_rows(x, k):
    n = x.shape[0]
    row = _iota(x.shape, 0).astype(F32)
    vals, idxs = [], []
    for _ in range(k):
        mx = jnp.max(x, axis=0, keepdims=True)
        ix = jnp.min(jnp.where(x == mx, row, float(n)), axis=0, keepdims=True)
        x = jnp.where(row == ix, NEG_INF, x)
        vals.append(mx)
        idxs.append(ix)
    return jnp.concatenate(vals, axis=0), jnp.concatenate(idxs, axis=0)


def _peer_select_body(q_ref, keys_ref, a_ref, b_ref, g_ref):
    tm = q_ref.shape[0]
    kk = PEER_TOPK
    a_all, b_all, g_all = [], [], []
    for h in range(PEER_HEADS):
        sub = []
        for c in range(2):
            keys = keys_ref[h, c].astype(BF16)
            col0 = (h * 2 + c) * PEER_HALF
            sub.append(_topk_rows(_dot_nt(keys, q_ref[:, col0:col0 + PEER_HALF]), kk))
        (s1, i1), (s2, i2) = sub
        width = [kk // (i + 1) for i in range(kk)]
        n_cand = sum(width)
        pad = -(-n_cand // 8) * 8 - n_cand
        fill = lambda v: [jnp.full((pad, tm), v, F32)] if pad else []
        cand = jnp.concatenate([s1[i:i + 1] + s2[0:width[i]] for i in range(kk)] + fill(NEG_INF), axis=0)
        ca = jnp.concatenate([jnp.broadcast_to(i1[i:i + 1], (width[i], tm)) for i in range(kk)] + fill(0.0), axis=0)
        cb = jnp.concatenate([i2[0:width[i]] for i in range(kk)] + fill(0.0), axis=0)
        row = _iota(cand.shape, 0).astype(F32)
        tops, a_h, b_h = [], [], []
        for _ in range(kk):
            mx = jnp.max(cand, axis=0, keepdims=True)
            ix = jnp.min(jnp.where(cand == mx, row, float(cand.shape[0])), axis=0, keepdims=True)
            hit = row == ix
            a_h.append(jnp.sum(jnp.where(hit, ca, 0.0), axis=0, keepdims=True))
            b_h.append(jnp.sum(jnp.where(hit, cb, 0.0), axis=0, keepdims=True))
            cand = jnp.where(hit, NEG_INF, cand)
            tops.append(mx)
        top = jnp.concatenate(tops, axis=0)
        e = jnp.exp(top - top[0:1])
        g_all.append(e / jnp.sum(e, axis=0, keepdims=True))
        a_all.append(jnp.concatenate(a_h, axis=0))
        b_all.append(jnp.concatenate(b_h, axis=0))
    a_ref[...] = jnp.concatenate(a_all, axis=0).T
    b_ref[...] = jnp.concatenate(b_all, axis=0).T
    g_ref[...] = jnp.concatenate(g_all, axis=0).T


def _peer_select(q, peer_keys):
    n = q.shape[0]
    tm = _tile(n, LANES, LANES)
    slots = PEER_HEADS * PEER_TOPK
    out = jax.ShapeDtypeStruct((n, slots), F32)
    spec = pl.BlockSpec((tm, slots), lambda i: (i, 0))
    return pl.pallas_call(
        _peer_select_body,
        grid=(n // tm,),
        in_specs=[pl.BlockSpec((tm, q.shape[1]), lambda i: (i, 0)),
                  pl.BlockSpec(peer_keys.shape, lambda i: (0, 0, 0, 0))],
        out_specs=[spec, spec, spec],
        out_shape=[out, out, out],
        compiler_params=_params("parallel"),
        name="peer_select",
    )(q, peer_keys)


def _peer_coef_body(a_ref, b_ref, g_ref, act_ref, o_ref, w_ref):
    tt = a_ref.shape[0]
    slots = a_ref.shape[1]
    sub = _iota((PEER_KEYS, slots), 0).astype(F32).astype(BF16)
    one, zero = jnp.ones((), BF16), jnp.zeros((), BF16)
    for t in range(tt):
        g_row = g_ref[t:t + 1, :]
        g_hi = g_row.astype(BF16)
        g_lo = (g_row - g_hi.astype(F32)).astype(BF16)
        a_hot = jnp.where(sub == a_ref[t:t + 1, :].astype(BF16), one, zero)
        hit = sub == b_ref[t:t + 1, :].astype(BF16)
        b_hi = jnp.where(hit, g_hi, zero)
        b_lo = jnp.where(hit, g_lo, zero)
        w_ref[:, t, :] = _dot_nt(jnp.concatenate([a_hot, a_hot], axis=1),
                                 jnp.concatenate([b_hi, b_lo], axis=1))
    o_ref[...] = (w_ref[...] * act_ref[...]).astype(o_ref.dtype)


def _peer_coef(a_sel, b_sel, g_sel, act):
    n, slots = a_sel.shape
    tt = _tile(n, 16)
    spec = pl.BlockSpec((tt, slots), lambda i: (i, 0))
    cube = pl.BlockSpec((PEER_KEYS, tt, PEER_KEYS), lambda i: (0, i, 0))
    return pl.pallas_call(
        _peer_coef_body,
        grid=(n // tt,),
        in_specs=[spec, spec, spec, cube],
        out_specs=cube,
        out_shape=jax.ShapeDtypeStruct((PEER_KEYS, n, PEER_KEYS), BF16),
        scratch_shapes=[pltpu.VMEM((PEER_KEYS, tt, PEER_KEYS), F32)],
        compiler_params=_params("parallel"),
        name="peer_coef",
    )(a_sel, b_sel, g_sel, act)


def _rope_tables(pos):
    half = HEAD_DIM // 2
    inv = ROPE_THETA ** (-jnp.arange(half, dtype=F32) / half)
    ang = pos.astype(F32)[:, None] * inv[None, :]
    cos, sin = jnp.cos(ang), jnp.sin(ang)
    return jnp.concatenate([cos, cos], axis=-1), jnp.concatenate([-sin, sin], axis=-1)


def _split_w_in(w_in):
    d = w_in.shape[0]
    o_f = FOX_W + 2 * FOX_KV_W
    o_q = o_f + FOX_HEADS
    o_g = o_q + NSA_W + 6 * NSA_KV_W
    main = jnp.concatenate([w_in[:, :o_f], w_in[:, o_q:o_g]], axis=1)
    forget = jnp.pad(w_in[:, o_f:o_q], ((0, 0), (0, LANES - FOX_HEADS)))
    gates = w_in[:, o_g:].reshape(d, NSA_KV_HEADS, NSA_GROUP, NSA_BRANCHES)
    gates = jnp.transpose(gates, (0, 1, 3, 2)).reshape(d, NSA_KV_HEADS, NSA_BRANCHES * NSA_GROUP)
    gates = jnp.pad(gates, ((0, 0), (0, 0), (0, LANES - NSA_BRANCHES * NSA_GROUP)))
    return main, jnp.concatenate([forget, gates.reshape(d, NSA_KV_HEADS * LANES)], axis=1)


def _pad_axis1(a, size):
    return jnp.pad(a, ((0, 0), (0, size - a.shape[1])) + ((0, 0),) * (a.ndim - 2))


def _layer(x, p_ple, cache_fox_kv, cache_fox_logf, cache_nsa_kv, state_win, pt_flat, dims,
           g_mix, w_in, b_f, w_ck, w_cv, w_a, w_b, w_merge, b_merge, w_o,
           g_ffn, w_pq, peer_keys, peer_u, peer_v, g_ple, w_ple_gate, w_ple):
    b, t, db, ts, n_pages = dims
    n, d = x.shape
    n_p = b * t
    past = n_pages * PAGE_SIZE
    pp = _tile(n_pages, PAGES_PER_STEP, 1)
    pp_attn = _tile(n_pages, ATTN_PAGES_PER_STEP, 1)

    h = _rmsnorm(x, g_mix, BF16)
    w_main, w_small = _split_w_in(w_in)
    proj = _matmul(h, w_main, out_dtype=F32, name="in_proj", **FULL_K)
    proj2 = _matmul(h, w_small, tm=1664, tn=384, tk=2048, out_dtype=F32, name="in_proj_small")
    pos = jnp.concatenate([jnp.tile(jnp.arange(t), b), jnp.tile(past + jnp.arange(ts), db)])
    cos, sin = _rope_tables(pos)
    bf_row = jnp.pad(b_f, (0, LANES - FOX_HEADS)).reshape(1, LANES)
    qf, fkv, qn, qr, nkv, win, logf, gates = _post_projection(proj, proj2, cos, sin, bf_row)

    lf_p_t = jnp.transpose(logf[:n_p].reshape(b, t, FOX_HEADS), (0, 2, 1))
    o_f_p = _fox_prompt(qf, fkv, _cumsum_prompt(lf_p_t), b, t)
    w_cmp = jnp.concatenate([jnp.tile(w_ck, (1, NSA_KV_HEADS)), jnp.tile(w_cv, (1, NSA_KV_HEADS))], axis=1)
    cmp_p = _compress_rows(nkv, w_cmp, n_p)
    o_n_p = _nsa_prompt(qn, qr, cmp_p, nkv, win, gates, b, t)

    sample = lambda a: a[n_p:].reshape(db, ts, a.shape[1])
    n_pool = cache_fox_logf.shape[0]
    lc_pool = _page_cumsum(jnp.transpose(cache_fox_logf, (0, 2, 1)).reshape(n_pool * FOX_HEADS, PAGE_SIZE))
    lf_new_t = jnp.pad(jnp.transpose(sample(logf), (0, 2, 1)), ((0, 0), (0, 0), (0, PAGE_SIZE - ts)))
    lc_new = _page_cumsum(lf_new_t.reshape(db * FOX_HEADS, PAGE_SIZE))
    slab_rows = lambda a: a.reshape(a.shape[0], -1, HEAD_DIM)
    fkv_new = slab_rows(_pad_axis1(sample(fkv), PAGE_SIZE))
    o_f_s = _fox_decode(pt_flat, sample(qf).astype(F32), slab_rows(cache_fox_kv),
                        lc_pool.reshape(n_pool, FOX_HEADS, PAGE_SIZE), fkv_new,
                        lc_new.reshape(db, FOX_HEADS, PAGE_SIZE), db, ts, n_pages, pp_attn)

    cache_nsa = slab_rows(cache_nsa_kv)
    nkv_new = _pad_axis1(sample(nkv), PAGE_SIZE)
    cmp_new = _compress_rows(nkv_new.reshape(db * PAGE_SIZE, 4 * NSA_KV_W), w_cmp, db * PAGE_SIZE)
    cmp_s = jnp.concatenate([_compress_paged(pt_flat, cache_nsa, w_cmp, db, n_pages, pp),
                             cmp_new.reshape(db, PAGE_SIZE // NSA_BLOCK, 2 * NSA_KV_W)], axis=1)
    cmp_s = _pad_axis1(cmp_s, -(-cmp_s.shape[1] // LANES) * LANES)
    wb = state_win.shape[1]
    o_n_s, new_win = _nsa_decode(
        pt_flat, sample(qn).astype(F32), sample(qr).astype(F32), cmp_s, cache_nsa, slab_rows(nkv_new),
        slab_rows(state_win), slab_rows(_pad_axis1(sample(win), PAGE_SIZE)),
        sample(gates), db, ts, n_pages, pp_attn)

    o_f = jnp.concatenate([o_f_p, o_f_s.reshape(db * ts, FOX_W).astype(BF16)], axis=0)
    o_n = jnp.concatenate([o_n_p, o_n_s.reshape(db * ts, NSA_W).astype(BF16)], axis=0)

    gate = _matmul(h, w_merge, out_dtype=F32, name="merge_gate", **FULL_K,
                   epilogue=lambda acc, bias: jax.nn.sigmoid(acc + bias), extra=[(b_merge.reshape(1, -1), "row", 0)])
    m_a = _matmul(o_f, w_a, tm=1664, tn=512, tk=2048, out_dtype=F32, name="merge_fox",
                  epilogue=lambda acc, ga: ga * acc, extra=[(gate, "tile", 0)])
    tn_b = _tile(d, 512, LANES)
    mixed = _matmul(o_n, w_b, tm=1664, tn=512, tk=2048, out_dtype=BF16, name="merge_nsa",
                    epilogue=lambda acc, gb, ma: ma + gb * acc,
                    extra=[(gate, "tile", d // tn_b), (m_a, "tile", 0)])
    x1 = _matmul(mixed, w_o, out_dtype=F32, name="out_proj", **FULL_K,
                 epilogue=lambda acc, res: res + acc, extra=[(x, "tile", 0)])

    h2 = _rmsnorm(x1, g_ffn, BF16)
    pq = _matmul(h2, w_pq, out_dtype=BF16, name="peer_query", **FULL_K)
    a_sel, b_sel, g_sel = _peer_select(pq, peer_keys)
    act = _matmul(h2, peer_u, out_dtype=F32, nt=True, split_out=True, epilogue=_gelu_tanh, name="peer_act",
                  **FULL_K)
    coef = _peer_coef(a_sel, b_sel, g_sel, act)
    x2 = _matmul(coef, peer_v, tm=1664, tn=512, tk=2048, out_dtype=F32, stacked_lhs=True, name="peer_out",
                 epilogue=lambda acc, res: res + acc, extra=[(x1, "tile", 0)])

    h3 = _rmsnorm(x2, g_ple, BF16)
    pw = _matmul(p_ple.astype(BF16), w_ple, tm=1664, tn=512, tk=2048, out_dtype=F32, name="ple_embed")
    x3 = _matmul(h3, w_ple_gate, out_dtype=F32, name="ple_gate", **FULL_K,
                 epilogue=lambda acc, res, pe: res + jax.nn.sigmoid(acc) * pe,
                 extra=[(x2, "tile", 0), (pw, "tile", 0)])

    caches = dict(
        fox_kv_p=fkv[:n_p].reshape(b, t, 2, FOX_KV_HEADS, HEAD_DIM),
        fox_lf_p=logf[:n_p].reshape(b, t, FOX_HEADS),
        nsa_kv_p=nkv[:n_p].reshape(b, t, 4, NSA_KV_HEADS, HEAD_DIM),
        nsa_win_p=win[:n_p].reshape(b, t, 2, NSA_KV_HEADS, HEAD_DIM)[:, t - min(NSA_WINDOW, t):],
        fox_kv_s=fkv[n_p:].reshape(db, ts, 2, FOX_KV_HEADS, HEAD_DIM),
        fox_lf_s=logf[n_p:].reshape(db, ts, FOX_HEADS),
        nsa_kv_s=nkv[n_p:].reshape(db, ts, 4, NSA_KV_HEADS, HEAD_DIM),
        nsa_win_s=new_win.reshape(db, wb, 2, NSA_KV_HEADS, HEAD_DIM),
    )
    return x3, caches


def kernel(x_prompt, x_sample, cache_fox_kv, cache_fox_logf, cache_nsa_kv, state_nsa_win, page_table, p_prompt, p_sample, g_mix, w_in, b_f, w_ck, w_cv, w_a, w_b, w_merge, b_merge, w_o, g_ffn, w_pq, peer_keys, peer_u, peer_v, g_ple, w_ple_gate, w_ple, g_final):
    b, t, d = x_prompt.shape
    db, ts, _ = x_sample.shape
    depth = g_mix.shape[0]
    n_pages = page_table.shape[1]
    n_p = b * t
    dims = (b, t, db, ts, n_pages)
    pt_flat = page_table.reshape(-1).astype(jnp.int32)
    x = jnp.concatenate([x_prompt.reshape(n_p, d), x_sample.reshape(db * ts, d)], axis=0)
    per_layer = []
    for i in range(depth):
        p_ple = jnp.concatenate([p_prompt[i].reshape(n_p, -1), p_sample[i].reshape(db * ts, -1)], axis=0)
        x, caches = _layer(x, p_ple, cache_fox_kv[i], cache_fox_logf[i], cache_nsa_kv[i], state_nsa_win[i],
                           pt_flat, dims, g_mix[i], w_in[i], b_f[i], w_ck[i], w_cv[i], w_a[i], w_b[i],
                           w_merge[i], b_merge[i], w_o[i], g_ffn[i], w_pq[i], peer_keys[i], peer_u[i],
                           peer_v[i], g_ple[i], w_ple_gate[i], w_ple[i])
        per_layer.append(caches)
    y_p = _rmsnorm(x, g_final, F32, 0, n_p)
    y_s = _rmsnorm(x, g_final, F32, n_p, db * ts)
    stack = lambda name: jnp.stack([c[name] for c in per_layer])
    return (y_p.reshape(b, t, d), y_s.reshape(db, ts, d),
            stack("fox_kv_p"), stack("fox_lf_p"), stack("nsa_kv_p"), stack("nsa_win_p"),
            stack("fox_kv_s"), stack("fox_lf_s"), stack("nsa_kv_s"), stack("nsa_win_s"))
```

```python
import functools

import jax
import jax.numpy as jnp
from jax import lax
from jax.experimental import pallas as pl
from jax.experimental.pallas import tpu as pltpu

F32 = jnp.float32
BF16 = jnp.bfloat16
NEG_INF = float("-inf")

HEAD_DIM = 128
FOX_HEADS = 16
FOX_KV_HEADS = 4
FOX_GROUP = FOX_HEADS // FOX_KV_HEADS
NSA_HEADS = 16
NSA_KV_HEADS = 2
NSA_GROUP = NSA_HEADS // NSA_KV_HEADS
NSA_BLOCK = 64
BLOCK_SHIFT = NSA_BLOCK.bit_length() - 1
NSA_TOP_N = 16
NSA_WINDOW = 512
NSA_BRANCHES = 3
SEL_FORCE = 1e4
ROPE_THETA = 10000.0
PAGE_SIZE = 128
PEER_HEADS = 8
PEER_KEYS = 128
PEER_TOPK = 16
PEER_HALF = 128
RMS_EPS = 1e-6
SCALE = HEAD_DIM ** -0.5
LOG2E = 1.4426950408889634
SCALE2 = SCALE * LOG2E

FOX_W = FOX_HEADS * HEAD_DIM
FOX_KV_W = FOX_KV_HEADS * HEAD_DIM
NSA_W = NSA_HEADS * HEAD_DIM
NSA_KV_W = NSA_KV_HEADS * HEAD_DIM
LANES = 128
VMEM_LIMIT = 56 * 1024 * 1024
PAGES_PER_STEP = 8
ATTN_PAGES_PER_STEP = 16
PAGE_SLABS = 8
WIN_SLABS = 2 * NSA_KV_HEADS

NT_DIMS = (((1,), (1,)), ((), ()))
FULL_K = dict(tm=832, tn=512, tk=4096)


def _params(*sem):
    return pltpu.CompilerParams(dimension_semantics=sem, vmem_limit_bytes=VMEM_LIMIT)


def _tile(n, target, align=8):
    if n <= target:
        return n
    best = None
    for t in range(align, target + 1, align):
        if n % t == 0:
            best = t
    assert best is not None, (n, target, align)
    return best


def _dot(a, b):
    return jnp.dot(a, b, preferred_element_type=F32)


def _dot_nt(a, b):
    return lax.dot_general(a, b, NT_DIMS, preferred_element_type=F32)


def _dot_tn(a, b):
    k = a.shape[0]
    kp = -(-k // LANES) * LANES
    if kp != k:
        a = jnp.concatenate([a, jnp.zeros((kp - k, a.shape[1]), a.dtype)], axis=0)
        b = jnp.concatenate([b, jnp.zeros((kp - k, b.shape[1]), b.dtype)], axis=0)
    return _dot(a.T.astype(BF16), b.astype(BF16))


def _iota(shape, dim, dtype=jnp.int32):
    return lax.broadcasted_iota(dtype, shape, dim)


def _rmsnorm_body(x_ref, g_ref, o_ref):
    x = x_ref[...]
    y = x * lax.rsqrt(jnp.mean(x * x, axis=-1, keepdims=True) + RMS_EPS)
    o_ref[...] = (y * g_ref[...]).astype(o_ref.dtype)


def _rmsnorm(x, g, out_dtype, row0=0, n=None):
    d = x.shape[1]
    n = x.shape[0] if n is None else n
    tr = _tile(n, 256)
    while row0 % tr:
        tr //= 2
    first = row0 // tr
    return pl.pallas_call(
        _rmsnorm_body,
        grid=(n // tr,),
        in_specs=[pl.BlockSpec((tr, d), lambda i: (first + i, 0)), pl.BlockSpec((1, d), lambda i: (0, 0))],
        out_specs=pl.BlockSpec((tr, d), lambda i: (i, 0)),
        out_shape=jax.ShapeDtypeStruct((n, d), out_dtype),
        compiler_params=_params("parallel"),
        name="rmsnorm",
    )(x, g.reshape(1, d))


def _gelu_tanh(x):
    return 0.5 * x * (1.0 + jnp.tanh(0.7978845608028654 * (x + 0.044715 * (x * x * x))))


def _mm_body(*refs, n_extra, epilogue, nt, nk, split_out, stacked_lhs):
    x_ref, w_ref = refs[0], refs[1]
    extra = refs[2:2 + n_extra]
    o_ref = refs[2 + n_extra]
    w = w_ref[...].astype(BF16)
    if stacked_lhs:
        x = jnp.concatenate([x_ref[j] for j in range(x_ref.shape[0])], axis=1)
    else:
        x = x_ref[...]
    part = _dot_nt(x, w) if nt else _dot(x, w)

    def finish(acc):
        res = epilogue(acc, *[e[...] for e in extra]).astype(o_ref.dtype)
        if split_out:
            for j in range(o_ref.shape[0]):
                o_ref[j] = res[:, j * LANES:(j + 1) * LANES]
        else:
            o_ref[...] = res

    if nk == 1:
        finish(part)
        return
    acc_ref = refs[3 + n_extra]
    k = pl.program_id(2)

    @pl.when(k == 0)
    def _():
        acc_ref[...] = part

    @pl.when(k > 0)
    def _():
        acc_ref[...] += part

    @pl.when(k == nk - 1)
    def _():
        finish(acc_ref[...])


def _matmul(x, w, *, tm, tn, tk, out_dtype, name, epilogue=None, extra=(), nt=False, split_out=False,
            stacked_lhs=False):
    if stacked_lhs:
        m, kdim = x.shape[1], x.shape[0] * LANES
    else:
        m, kdim = x.shape
    n = w.shape[0] if nt else w.shape[1]
    tm, tn, tk = _tile(m, tm), _tile(n, tn, LANES), _tile(kdim, tk, LANES)
    nk = kdim // tk
    if epilogue is None:
        epilogue = lambda acc: acc
    if stacked_lhs:
        x_spec = pl.BlockSpec((tk // LANES, tm, LANES), lambda i, j, k: (k, i, 0))
    else:
        x_spec = pl.BlockSpec((tm, tk), lambda i, j, k: (i, k))
    in_specs = [
        x_spec,
        pl.BlockSpec((tn, tk), lambda i, j, k: (j, k)) if nt else pl.BlockSpec((tk, tn), lambda i, j, k: (k, j)),
    ]
    args = [x, w]
    for arr, kind, off in extra:
        if kind == "tile":
            in_specs.append(pl.BlockSpec((tm, tn), lambda i, j, k, off=off: (i, j + off)))
        elif kind == "row":
            in_specs.append(pl.BlockSpec((1, tn), lambda i, j, k, off=off: (0, j + off)))
        elif kind == "lhs":
            in_specs.append(pl.BlockSpec((tm, arr.shape[1]), lambda i, j, k: (i, 0)))
        else:
            assert kind == "rhs", kind
            in_specs.append(pl.BlockSpec((arr.shape[0], tn), lambda i, j, k: (0, j)))
        args.append(arr)
    if split_out:
        out_shape = jax.ShapeDtypeStruct((n // LANES, m, LANES), out_dtype)
        out_spec = pl.BlockSpec((tn // LANES, tm, LANES), lambda i, j, k: (j, i, 0))
    else:
        out_shape = jax.ShapeDtypeStruct((m, n), out_dtype)
        out_spec = pl.BlockSpec((tm, tn), lambda i, j, k: (i, j))
    body = functools.partial(_mm_body, n_extra=len(extra), epilogue=epilogue, nt=nt, nk=nk,
                             split_out=split_out, stacked_lhs=stacked_lhs)
    return pl.pallas_call(
        body,
        grid=(m // tm, n // tn, nk),
        in_specs=in_specs,
        out_specs=out_spec,
        out_shape=out_shape,
        scratch_shapes=[pltpu.VMEM((tm, tn), F32)] if nk > 1 else [],
        compiler_params=_params("parallel", "parallel", "arbitrary"),
        name=name,
    )(*args)


def _rope(x, cos, sin_signed):
    return x * cos + pltpu.roll(x, HEAD_DIM // 2, 1) * sin_signed


def _log_sigmoid(z):
    return jnp.minimum(z, 0.0) - jnp.log1p(jnp.exp(-jnp.abs(z)))


def _post_body(p_ref, cos_ref, sin_ref, bf_ref,
               qf_ref, fkv_ref, qn_ref, qr_ref, nkv_ref, win_ref, logf_ref, gate_ref):
    cos = cos_ref[...]
    sin = sin_ref[...]
    o = 0
    qf_ref[...] = p_ref[:, o:o + FOX_W].astype(BF16)
    o += FOX_W
    fkv_ref[...] = p_ref[:, o:o + 2 * FOX_KV_W]
    o += 2 * FOX_KV_W
    for h in range(NSA_HEADS):
        q = p_ref[:, o + h * HEAD_DIM:o + (h + 1) * HEAD_DIM]
        qn_ref[:, h * HEAD_DIM:(h + 1) * HEAD_DIM] = q.astype(BF16)
        qr_ref[:, h * HEAD_DIM:(h + 1) * HEAD_DIM] = _rope(q, cos, sin).astype(BF16)
    o += NSA_W
    nkv_ref[:, 0:2 * NSA_KV_W] = p_ref[:, o:o + 2 * NSA_KV_W]
    o += 2 * NSA_KV_W
    for g in range(NSA_KV_HEADS):
        k = p_ref[:, o + g * HEAD_DIM:o + (g + 1) * HEAD_DIM]
        nkv_ref[:, 2 * NSA_KV_W + g * HEAD_DIM:2 * NSA_KV_W + (g + 1) * HEAD_DIM] = _rope(k, cos, sin)
    o += NSA_KV_W
    nkv_ref[:, 3 * NSA_KV_W:4 * NSA_KV_W] = p_ref[:, o:o + NSA_KV_W]
    o += NSA_KV_W
    for g in range(NSA_KV_HEADS):
        k = p_ref[:, o + g * HEAD_DIM:o + (g + 1) * HEAD_DIM]
        win_ref[:, g * HEAD_DIM:(g + 1) * HEAD_DIM] = _rope(k, cos, sin)
    o += NSA_KV_W
    win_ref[:, NSA_KV_W:2 * NSA_KV_W] = p_ref[:, o:o + NSA_KV_W]
    o += NSA_KV_W
    logf = _log_sigmoid(p_ref[:, o:o + LANES] + bf_ref[...])
    logf_ref[...] = logf[:, 0:FOX_HEADS]
    gate_ref[...] = jax.nn.sigmoid(p_ref[:, o + LANES:o + 3 * LANES])


def _post_projection(p, cos, sin, bf_row, row0, n):
    tr = _tile(n, 256)
    while row0 % tr:
        tr //= 2
    first = row0 // tr
    row = lambda w: pl.BlockSpec((tr, w), lambda i: (i, 0))
    row_in = lambda w: pl.BlockSpec((tr, w), lambda i: (first + i, 0))
    out_shapes = [
        jax.ShapeDtypeStruct((n, FOX_W), BF16),
        jax.ShapeDtypeStruct((n, 2 * FOX_KV_W), F32),
        jax.ShapeDtypeStruct((n, NSA_W), BF16),
        jax.ShapeDtypeStruct((n, NSA_W), BF16),
        jax.ShapeDtypeStruct((n, 4 * NSA_KV_W), F32),
        jax.ShapeDtypeStruct((n, 2 * NSA_KV_W), F32),
        jax.ShapeDtypeStruct((n, FOX_HEADS), F32),
        jax.ShapeDtypeStruct((n, 2 * LANES), F32),
    ]
    return pl.pallas_call(
        _post_body,
        grid=(n // tr,),
        in_specs=[row_in(p.shape[1]), row_in(LANES), row_in(LANES),
                  pl.BlockSpec((1, LANES), lambda i: (0, 0))],
        out_specs=[row(s.shape[1]) for s in out_shapes],
        out_shape=out_shapes,
        compiler_params=_params("parallel"),
        name="post_projection",
    )(p, cos, sin, bf_row)


def _split3(x):
    hi = x.astype(BF16)
    r1 = x - hi.astype(F32)
    mid = r1.astype(BF16)
    lo = (r1 - mid.astype(F32)).astype(BF16)
    return hi, mid, lo


def _row_cumsum(x, u):
    hi, mid, lo = _split3(x)
    return (_dot(hi, u) + _dot(mid, u)) + _dot(lo, u)


def _cumsum_prompt_body(lf_ref, u_ref, o_ref, carry_ref):
    @pl.when(pl.program_id(1) == 0)
    def _():
        carry_ref[...] = jnp.zeros_like(carry_ref)

    c = _row_cumsum(lf_ref[...], u_ref[...]) + carry_ref[...]
    o_ref[...] = c
    r = c.shape[1]
    carry_ref[...] = c[:, r - 1:r]


def _upper_ones(r):
    return (jnp.arange(r)[:, None] <= jnp.arange(r)[None, :]).astype(BF16)


def _cumsum_prompt(lf_t):
    b, h, t = lf_t.shape
    r = _tile(t, 512, LANES)
    return pl.pallas_call(
        _cumsum_prompt_body,
        grid=(b, t // r),
        in_specs=[pl.BlockSpec((None, h, r), lambda i, j: (i, 0, j)),
                  pl.BlockSpec((r, r), lambda i, j: (0, 0))],
        out_specs=pl.BlockSpec((None, h, r), lambda i, j: (i, 0, j)),
        out_shape=jax.ShapeDtypeStruct((b, h, t), F32),
        scratch_shapes=[pltpu.VMEM((h, 1), F32)],
        compiler_params=_params("parallel", "arbitrary"),
        name="cumsum_prompt",
    )(lf_t, _upper_ones(r))


def _page_cumsum_body(x_ref, u_ref, o_ref):
    o_ref[...] = _row_cumsum(x_ref[...], u_ref[...])


def _page_cumsum(x):
    rows = x.shape[0]
    tr = _tile(rows, 2048)
    return pl.pallas_call(
        _page_cumsum_body,
        grid=(rows // tr,),
        in_specs=[pl.BlockSpec((tr, PAGE_SIZE), lambda i: (i, 0)),
                  pl.BlockSpec((PAGE_SIZE, PAGE_SIZE), lambda i: (0, 0))],
        out_specs=pl.BlockSpec((tr, PAGE_SIZE), lambda i: (i, 0)),
        out_shape=jax.ShapeDtypeStruct((rows, PAGE_SIZE), F32),
        compiler_params=_params("parallel"),
        name="page_cumsum",
    )(x, _upper_ones(PAGE_SIZE))


def _online_update_t(s, vt, m_ref, l_ref, acc_ref, guard):
    m_old = m_ref[...]
    m_new = jnp.maximum(m_old, jnp.max(s, axis=0, keepdims=True))
    m_use = jnp.where(m_new == NEG_INF, 0.0, m_new) if guard else m_new
    alpha = jnp.exp2(m_old - m_use)
    p = jnp.exp2(s - m_use)
    l_ref[...] = alpha * l_ref[...] + jnp.sum(p, axis=0, keepdims=True)
    acc_ref[...] = alpha * acc_ref[...] + _dot(vt, p.astype(BF16))
    m_ref[...] = m_new


def _pipelined_chunks(lo, hi, last, qk, consume, s_a, s_b):
    s_a[...] = qk(lo)

    def step(j, src, dst):
        dst[...] = qk(jnp.minimum(j + 1, last))
        consume(j, src[...])

    def pair(t, carry):
        j = lo + 2 * t
        step(j, s_a, s_b)

        @pl.when(j + 1 < hi)
        def _():
            step(j + 1, s_b, s_a)

        return carry

    lax.fori_loop(0, (hi - lo + 1) // 2, pair, 0)


def _online_update(s, pv_fn, m_ref, l_ref, acc_ref, guard):
    m_old = m_ref[...]
    m_new = jnp.maximum(m_old, jnp.max(s, axis=-1, keepdims=True))
    m_use = jnp.where(m_new == NEG_INF, 0.0, m_new) if guard else m_new
    alpha = jnp.exp2(m_old - m_use)
    p = jnp.exp2(s - m_use)
    l_ref[...] = alpha * l_ref[...] + jnp.sum(p, axis=-1, keepdims=True)
    acc_ref[...] = alpha * acc_ref[...] + pv_fn(p.astype(BF16))
    m_ref[...] = m_new


def _masked_softmax(s, keep, axis):
    s = jnp.where(keep > 0.5, s, NEG_INF)
    m = jnp.max(s, axis=axis, keepdims=True)
    m = jnp.where(m == NEG_INF, 0.0, m)
    e = jnp.exp2(s - m)
    return e / jnp.maximum(jnp.sum(e, axis=axis, keepdims=True), 1e-30)


def _normalize(acc, l):
    return acc / jnp.maximum(l, 1e-30)


def _init_state(*triples):
    for m_ref, l_ref, acc_ref in triples:
        m_ref[...] = jnp.full_like(m_ref, NEG_INF)
        l_ref[...] = jnp.zeros_like(l_ref)
        acc_ref[...] = jnp.zeros_like(acc_ref)


def _mask_heads(s, keep, n_heads, width):
    return jnp.concatenate(
        [jnp.where(keep, s[:, r * width:(r + 1) * width], NEG_INF) for r in range(n_heads)], axis=1)


def _fox_prompt_body(q_ref, qx_ref, k_ref, kx_ref, v_ref, o_ref,
                     qa_ref, ka_ref, vt_ref, sa_ref, sb_ref, m_ref, l_ref, acc_ref, *, tq, nq):
    i = pl.program_id(2)
    rr = FOX_GROUP

    @pl.when(i == 0)
    def _():
        def prep(j, carry):
            ks = pl.multiple_of(j * tq, tq)
            ka_ref[j, :, 0:HEAD_DIM] = k_ref[pl.ds(ks, tq), :].astype(BF16)
            ka_ref[j, :, HEAD_DIM:2 * HEAD_DIM] = kx_ref[pl.ds(ks, tq), :]
            vt_ref[j] = v_ref[pl.ds(ks, tq), :].T.astype(BF16)
            return carry

        lax.fori_loop(0, nq, prep, 0)

    for r in range(rr):
        qa_ref[r * tq:(r + 1) * tq, 0:HEAD_DIM] = q_ref[:, r * HEAD_DIM:(r + 1) * HEAD_DIM]
        qa_ref[r * tq:(r + 1) * tq, HEAD_DIM:2 * HEAD_DIM] = qx_ref[r]
    _init_state((m_ref, l_ref, acc_ref))

    qk = lambda j: _dot_nt(ka_ref[j], qa_ref[...]) * SCALE2
    full = lambda j, s: _online_update_t(s, vt_ref[j], m_ref, l_ref, acc_ref, guard=False)
    _pipelined_chunks(0, i, i, qk, full, sa_ref, sb_ref)

    def diagonal(src):
        causal = _iota((tq, tq), 0) <= _iota((tq, tq), 1)
        _online_update_t(_mask_heads(src[...], causal, rr, tq), vt_ref[i], m_ref, l_ref, acc_ref, guard=False)

    pl.when(lax.rem(i, 2) == 0)(lambda: diagonal(sa_ref))
    pl.when(lax.rem(i, 2) == 1)(lambda: diagonal(sb_ref))

    o_t = _normalize(acc_ref[...], l_ref[...])
    for r in range(rr):
        o_ref[:, r * HEAD_DIM:(r + 1) * HEAD_DIM] = o_t[:, r * tq:(r + 1) * tq].T.astype(o_ref.dtype)


def _fox_bias_columns(c_t, b, t):
    g_, r_ = FOX_KV_HEADS, FOX_GROUP
    pieces = jnp.stack(_split3(c_t * (1.0 / SCALE)), axis=-1).reshape(b, g_, r_, t, 3)
    kx = -jnp.transpose(pieces, (0, 1, 3, 2, 4)).reshape(b, g_, t, 3 * r_)
    kx = jnp.concatenate([kx, jnp.ones((b, g_, t, 3), BF16)], axis=-1)
    kx = jnp.pad(kx, ((0, 0), (0, 0), (0, 0), (0, LANES - kx.shape[-1])))
    own = (jnp.arange(r_)[:, None] == jnp.arange(3 * r_)[None, :] // 3).astype(BF16)
    own = jnp.broadcast_to(own[None, None, :, None, :], (b, g_, r_, t, 3 * r_))
    qx = jnp.concatenate([own, pieces], axis=-1)
    qx = jnp.pad(qx, ((0, 0), (0, 0), (0, 0), (0, 0), (0, LANES - qx.shape[-1])))
    return qx, kx


def _fox_prompt(qf, fkv, c_t, b, t):
    tq = _tile(t, 256, LANES)
    nq = t // tq
    g_, r_ = FOX_KV_HEADS, FOX_GROUP
    rows = r_ * tq
    qx, kx = _fox_bias_columns(c_t, b, t)
    return pl.pallas_call(
        functools.partial(_fox_prompt_body, tq=tq, nq=nq),
        grid=(b, g_, nq),
        in_specs=[
            pl.BlockSpec((tq, r_ * HEAD_DIM), lambda bi, g, i: (bi * nq + i, g)),
            pl.BlockSpec((None, None, r_, tq, LANES), lambda bi, g, i: (bi, g, 0, i, 0)),
            pl.BlockSpec((t, HEAD_DIM), lambda bi, g, i: (bi, g)),
            pl.BlockSpec((None, None, t, LANES), lambda bi, g, i: (bi, g, 0, 0)),
            pl.BlockSpec((t, HEAD_DIM), lambda bi, g, i: (bi, g_ + g)),
        ],
        out_specs=pl.BlockSpec((tq, r_ * HEAD_DIM), lambda bi, g, i: (bi * nq + i, g)),
        out_shape=jax.ShapeDtypeStruct((b * t, FOX_W), BF16),
        scratch_shapes=[pltpu.VMEM((rows, 2 * HEAD_DIM), BF16),
                        pltpu.VMEM((nq, tq, 2 * HEAD_DIM), BF16),
                        pltpu.VMEM((nq, HEAD_DIM, tq), BF16),
                        pltpu.VMEM((tq, rows), F32), pltpu.VMEM((tq, rows), F32),
                        pltpu.VMEM((1, rows), F32), pltpu.VMEM((1, rows), F32),
                        pltpu.VMEM((HEAD_DIM, rows), F32)],
        compiler_params=_params("parallel", "parallel", "arbitrary"),
        name="fox_prompt",
    )(qf, qx, fkv, kx, fkv)


def _compress_body(x_ref, w_ref, o_ref):
    x = x_ref[...]
    nb = x.shape[0] // NSA_BLOCK
    xb = x.reshape(nb, NSA_BLOCK, x.shape[1]) * w_ref[...][None]
    o_ref[...] = jnp.sum(xb, axis=1) * (1.0 / NSA_BLOCK)


def _compress_rows(nkv, w_cmp, n_rows):
    width = 2 * NSA_KV_W
    tr = _tile(n_rows, 512, NSA_BLOCK * 8)
    return pl.pallas_call(
        _compress_body,
        grid=(n_rows // tr,),
        in_specs=[pl.BlockSpec((tr, width), lambda i: (i, 0)),
                  pl.BlockSpec((NSA_BLOCK, width), lambda i: (0, 0))],
        out_specs=pl.BlockSpec((tr // NSA_BLOCK, width), lambda i: (i, 0)),
        out_shape=jax.ShapeDtypeStruct((n_rows // NSA_BLOCK, width), F32),
        compiler_params=_params("parallel"),
        name="compress_rows",
    )(nkv, w_cmp)


def _compress_paged_body(pt_ref, *refs, pp):
    page_refs, w_ref, o_ref = refs[:pp], refs[pp], refs[pp + 1]
    nb = PAGE_SIZE // NSA_BLOCK
    for u in range(pp):
        for j in range(2 * NSA_KV_HEADS):
            x = _slab(page_refs[u], j, PAGE_SLABS, PAGE_SIZE)
            xb = x.reshape(nb, NSA_BLOCK, HEAD_DIM) * w_ref[:, j * HEAD_DIM:(j + 1) * HEAD_DIM][None]
            o_ref[u, :, j * HEAD_DIM:(j + 1) * HEAD_DIM] = jnp.sum(xb, axis=1) * (1.0 / NSA_BLOCK)


def _paged_specs(block, n_pages, pp):
    return [pl.BlockSpec((None,) + block, lambda b, p, pt, u=u: (pt[b * n_pages + p * pp + u], 0, 0))
            for u in range(pp)]


def _slab(ref, j, n_slabs, n_rows):
    return ref[pl.ds(j, n_rows, stride=n_slabs), :]


def _compress_paged(pt_flat, cache_nsa, w_cmp, db, n_pages, pp):
    width = 2 * NSA_KV_W
    nb = PAGE_SIZE // NSA_BLOCK
    grid_spec = pltpu.PrefetchScalarGridSpec(
        num_scalar_prefetch=1,
        grid=(db, n_pages // pp),
        in_specs=_paged_specs((PAGE_SIZE * PAGE_SLABS, HEAD_DIM), n_pages, pp)
        + [pl.BlockSpec((NSA_BLOCK, width), lambda b, p, pt: (0, 0))],
        out_specs=pl.BlockSpec((None, pp, nb, width), lambda b, p, pt: (b, p, 0, 0)),
    )
    out = pl.pallas_call(
        functools.partial(_compress_paged_body, pp=pp),
        grid_spec=grid_spec,
        out_shape=jax.ShapeDtypeStruct((db, n_pages, nb, width), F32),
        compiler_params=_params("parallel", "arbitrary"),
        name="compress_paged",
    )(pt_flat, *([cache_nsa] * pp), w_cmp)
    return out.reshape(db, n_pages * nb, width)


def _select_blocks(score, n_sel, axis):
    axis = axis % score.ndim
    nb = score.shape[axis]
    blk = _iota(score.shape, axis).astype(F32)
    sel = jnp.zeros(score.shape, F32)
    work = score
    for _ in range(n_sel):
        mx = jnp.max(work, axis=axis, keepdims=True)
        ix = jnp.min(jnp.where(work == mx, blk, float(nb)), axis=axis, keepdims=True)
        hit = blk == ix
        sel = jnp.where(hit, jnp.where(mx >= 0.0, 1.0, 0.0), sel)
        work = jnp.where(hit, NEG_INF, work)
    return sel


def _expand_blocks(sel_b, key_start, n_keys):
    nb = sel_b.shape[1]
    key_blk = lax.shift_right_logical(key_start + _iota((nb, n_keys), 1), BLOCK_SHIFT)
    e = jnp.where(_iota((nb, n_keys), 0) == key_blk, 1.0, 0.0).astype(BF16)
    return _dot(sel_b, e)


def _nsa_prompt_body(qn_ref, qr_ref, kc_ref, vc_ref, ks_ref, vs_ref, kw_ref, vw_ref, g_ref, o_ref,
                     qs_ref, ksb_ref, vst_ref, kwb_ref, vwt_ref, sel_ref, oc_ref,
                     m_s, l_s, acc_s, m_w, l_w, acc_w, *, tq, nq, nb):
    i = pl.program_id(2)
    rr = NSA_GROUP
    bpc = tq // NSA_BLOCK

    @pl.when(i == 0)
    def _():
        def prep(j, carry):
            ks = pl.multiple_of(j * tq, tq)
            ksb_ref[j] = ks_ref[pl.ds(ks, tq), :].astype(BF16)
            vst_ref[j] = vs_ref[pl.ds(ks, tq), :].T.astype(BF16)
            kwb_ref[j] = kw_ref[pl.ds(ks, tq), :].astype(BF16)
            vwt_ref[j] = vw_ref[pl.ds(ks, tq), :].T.astype(BF16)
            return carry

        lax.fori_loop(0, nq, prep, 0)

    stack = lambda ref: jnp.concatenate([ref[:, r * HEAD_DIM:(r + 1) * HEAD_DIM] for r in range(rr)], axis=0)
    lanes = lambda a: jnp.concatenate([a] * rr, axis=1)
    qn = stack(qn_ref)
    qs_ref[...] = stack(qr_ref)
    pos_q = i * tq + _iota((1, tq), 1)
    _init_state((m_s, l_s, acc_s), (m_w, l_w, acc_w))

    blk = _iota((nb, tq), 0)
    valid = (blk + 1) * NSA_BLOCK - 1 <= pos_q
    s_c = _dot_nt(kc_ref[...].astype(BF16), qn) * SCALE2
    p_c = _masked_softmax(s_c, lanes(jnp.where(valid, 1.0, 0.0)), axis=0)
    oc_ref[...] = _dot_tn(vc_ref[...], p_c)
    imp = p_c[:, 0:tq]
    for r in range(1, rr):
        imp = imp + p_c[:, r * tq:(r + 1) * tq]
    cur = lax.shift_right_logical(pos_q, BLOCK_SHIFT)
    score = jnp.where(blk == cur, SEL_FORCE, jnp.where(valid, imp, -1.0))
    sel_ref[...] = _select_blocks(score, min(NSA_TOP_N, nb), axis=0)

    def sel_keep(j):
        return jnp.concatenate(
            [jnp.broadcast_to(sel_ref[pl.ds(j * bpc + u, 1), :], (NSA_BLOCK, tq)) for u in range(bpc)], axis=0)

    def sel_chunk(j, carry):
        s = _mask_heads(_dot_nt(ksb_ref[j], qs_ref[...]) * SCALE2, sel_keep(j) > 0.5, rr, tq)
        _online_update_t(s, vst_ref[j], m_s, l_s, acc_s, guard=True)
        return carry

    lax.fori_loop(0, i, sel_chunk, 0)
    causal = _iota((tq, tq), 0) <= _iota((tq, tq), 1)
    keep = jnp.where(causal, sel_keep(i), 0.0) > 0.5
    s = _mask_heads(_dot_nt(ksb_ref[i], qs_ref[...]) * SCALE2, keep, rr, tq)
    _online_update_t(s, vst_ref[i], m_s, l_s, acc_s, guard=True)

    def win_chunk(j, carry):
        dpos = pos_q - (j * tq + _iota((tq, 1), 0))
        keep = jnp.where(dpos >= 0, jnp.where(dpos < NSA_WINDOW, 1.0, 0.0), 0.0) > 0.5
        s = _mask_heads(_dot_nt(kwb_ref[j], qs_ref[...]) * SCALE2, keep, rr, tq)
        _online_update_t(s, vwt_ref[j], m_w, l_w, acc_w, guard=True)
        return carry

    lax.fori_loop(jnp.maximum(i - (NSA_WINDOW + tq - 1) // tq, 0), i + 1, win_chunk, 0)

    g_t = g_ref[...].T
    o_s = _normalize(acc_s[...], l_s[...])
    o_w = _normalize(acc_w[...], l_w[...])
    for r in range(rr):
        cols = slice(r * tq, (r + 1) * tq)
        gate = lambda br: g_t[br * rr + r:br * rr + r + 1, :]
        o = gate(0) * oc_ref[:, cols] + gate(1) * o_s[:, cols] + gate(2) * o_w[:, cols]
        o_ref[:, r * HEAD_DIM:(r + 1) * HEAD_DIM] = o.T.astype(o_ref.dtype)


def _nsa_prompt(qn, qr, cmp_p, nkv, win, gates, b, t):
    tq = _tile(t, 256, LANES)
    nq = t // tq
    nb = t // NSA_BLOCK
    g_, rows = NSA_KV_HEADS, NSA_GROUP * tq
    qspec = pl.BlockSpec((tq, NSA_GROUP * HEAD_DIM), lambda bi, g, i: (bi * nq + i, g))
    col = lambda off: pl.BlockSpec((t, HEAD_DIM), lambda bi, g, i: (bi, off + g))
    state = [pltpu.VMEM((1, rows), F32), pltpu.VMEM((1, rows), F32), pltpu.VMEM((HEAD_DIM, rows), F32)]
    return pl.pallas_call(
        functools.partial(_nsa_prompt_body, tq=tq, nq=nq, nb=nb),
        grid=(b, g_, nq),
        in_specs=[
            qspec, qspec,
            pl.BlockSpec((nb, HEAD_DIM), lambda bi, g, i: (bi, g)),
            pl.BlockSpec((nb, HEAD_DIM), lambda bi, g, i: (bi, g_ + g)),
            col(2 * g_), col(3 * g_),
            col(0), col(g_),
            pl.BlockSpec((tq, LANES), lambda bi, g, i: (bi * nq + i, g)),
        ],
        out_specs=qspec,
        out_shape=jax.ShapeDtypeStruct((b * t, NSA_W), BF16),
        scratch_shapes=[pltpu.VMEM((rows, HEAD_DIM), BF16),
                        pltpu.VMEM((nq, tq, HEAD_DIM), BF16), pltpu.VMEM((nq, HEAD_DIM, tq), BF16),
                        pltpu.VMEM((nq, tq, HEAD_DIM), BF16), pltpu.VMEM((nq, HEAD_DIM, tq), BF16),
                        pltpu.VMEM((nb, tq), F32),
                        pltpu.VMEM((HEAD_DIM, rows), F32)] + state + state,
        compiler_params=_params("parallel", "parallel", "arbitrary"),
        name="nsa_prompt",
    )(qn, qr, cmp_p, cmp_p, nkv, nkv, win, win, gates)


def _stack_heads(q_ref, col0, n_heads):
    return jnp.concatenate([q_ref[:, col0 + r * HEAD_DIM:col0 + (r + 1) * HEAD_DIM] for r in range(n_heads)], axis=0)


def _repeat_rows(a, times):
    return jnp.concatenate([jnp.broadcast_to(a[h:h + 1, :], (times, a.shape[1])) for h in range(a.shape[0])], axis=0)


def _fox_decode_body(pt_ref, q_ref, *refs, n_steps, pp, ts):
    kv_refs, lc_refs = refs[:pp], refs[pp:2 * pp]
    new_ref, lcn_ref, o_ref, qs_ref, carry_ref, m_ref, l_ref, acc_ref = refs[2 * pp:]
    p = pl.program_id(1)
    rr = FOX_GROUP
    rows = rr * ts

    @pl.when(p == 0)
    def _():
        _init_state((m_ref, l_ref, acc_ref))
        carry_ref[...] = jnp.zeros_like(carry_ref)
        for g in range(FOX_KV_HEADS):
            qs_ref[g] = _stack_heads(q_ref, g * rr * HEAD_DIM, rr).astype(BF16)

    def attend(pages, c_k, keep):
        bias = _repeat_rows(c_k * LOG2E, ts)
        s = jnp.concatenate(
            [jnp.concatenate([_dot_nt(qs_ref[g], _slab(pg, g, PAGE_SLABS, PAGE_SIZE).astype(BF16))
                              for pg in pages], axis=1)
             for g in range(FOX_KV_HEADS)], axis=0) * SCALE2 - bias
        if keep is not None:
            s = jnp.where(keep, s, NEG_INF)

        def pv(pb):
            out = []
            for g in range(FOX_KV_HEADS):
                acc = None
                for u, pg in enumerate(pages):
                    part = _dot(pb[g * rows:(g + 1) * rows, u * PAGE_SIZE:(u + 1) * PAGE_SIZE],
                                _slab(pg, FOX_KV_HEADS + g, PAGE_SLABS, PAGE_SIZE).astype(BF16))
                    acc = part if acc is None else acc + part
                out.append(acc)
            return jnp.concatenate(out, axis=0)

        _online_update(s, pv, m_ref, l_ref, acc_ref, guard=False)

    run = carry_ref[...]
    c_pages = []
    for u in range(pp):
        lc = lc_refs[u][...]
        c_pages.append(run + lc)
        run = run + lc[:, PAGE_SIZE - 1:PAGE_SIZE]
    carry_ref[...] = run
    attend(kv_refs, jnp.concatenate(c_pages, axis=1), None)

    @pl.when(p == n_steps - 1)
    def _():
        keep_i = jnp.where(_iota((ts, PAGE_SIZE), 1) <= _iota((ts, PAGE_SIZE), 0), 1.0, 0.0)
        keep = jnp.concatenate([keep_i] * FOX_HEADS, axis=0) > 0.5
        attend([new_ref], run + lcn_ref[...], keep)
        o = _normalize(acc_ref[...], l_ref[...])
        for h in range(FOX_HEADS):
            o_ref[:, h * HEAD_DIM:(h + 1) * HEAD_DIM] = o[h * ts:(h + 1) * ts, :]


def _fox_decode(pt_flat, q_s, cache_kv, lc_pool, new_pad, lc_new, db, ts, n_pages, pp):
    rows = FOX_HEADS * ts
    n_steps = n_pages // pp
    page = (PAGE_SIZE * PAGE_SLABS, HEAD_DIM)
    grid_spec = pltpu.PrefetchScalarGridSpec(
        num_scalar_prefetch=1,
        grid=(db, n_steps),
        in_specs=[pl.BlockSpec((None, ts, FOX_W), lambda b, p, pt: (b, 0, 0))]
        + _paged_specs(page, n_pages, pp)
        + _paged_specs((FOX_HEADS, PAGE_SIZE), n_pages, pp)
        + [pl.BlockSpec((None,) + page, lambda b, p, pt: (b, 0, 0)),
           pl.BlockSpec((None, FOX_HEADS, PAGE_SIZE), lambda b, p, pt: (b, 0, 0))],
        out_specs=pl.BlockSpec((None, ts, FOX_W), lambda b, p, pt: (b, 0, 0)),
        scratch_shapes=[pltpu.VMEM((FOX_KV_HEADS, FOX_GROUP * ts, HEAD_DIM), BF16),
                        pltpu.VMEM((FOX_HEADS, 1), F32),
                        pltpu.VMEM((rows, 1), F32), pltpu.VMEM((rows, 1), F32),
                        pltpu.VMEM((rows, HEAD_DIM), F32)],
    )
    return pl.pallas_call(
        functools.partial(_fox_decode_body, n_steps=n_steps, pp=pp, ts=ts),
        grid_spec=grid_spec,
        out_shape=jax.ShapeDtypeStruct((db, ts, FOX_W), F32),
        compiler_params=_params("parallel", "arbitrary"),
        name="fox_decode",
    )(pt_flat, q_s, *([cache_kv] * pp), *([lc_pool] * pp), new_pad, lc_new)


def _nsa_decode_body(pt_ref, qn_ref, qr_ref, cmp_ref, *refs, n_steps, pp, ts, past, nbp, wb):
    kv_refs = refs[:pp]
    (new_ref, win_ref, wnew_ref, g_ref, o_ref, nwin_ref,
     qrs_ref, oc_ref, ow_ref, sel_ref, m_ref, l_ref, acc_ref) = refs[pp:]
    p = pl.program_id(1)
    rr = NSA_GROUP
    rows = rr * ts
    gg = NSA_KV_HEADS
    tile_rows = lambda a, n: jnp.concatenate([a] * n, axis=0)
    qi_col = _iota((ts, 1), 0)

    @pl.when(p == 0)
    def _():
        _init_state((m_ref, l_ref, acc_ref))
        pos_q = past + qi_col
        blk = _iota((ts, nbp), 1)
        valid = (blk + 1) * NSA_BLOCK - 1 <= pos_q
        exists = blk * NSA_BLOCK < past + ts
        keep_c = tile_rows(jnp.where(valid, 1.0, 0.0), rr)
        cur = lax.shift_right_logical(pos_q, BLOCK_SHIFT)
        dpos = wb + qi_col - _iota((ts, wb), 1)
        keep_buf = jnp.where(dpos >= 0, jnp.where(dpos < NSA_WINDOW, 1.0, 0.0), 0.0)
        keep_new = jnp.where(_iota((ts, PAGE_SIZE), 1) <= qi_col, 1.0, 0.0)
        keep_w = tile_rows(jnp.concatenate([keep_buf, keep_new], axis=1), rr)
        for g in range(gg):
            qn = _stack_heads(qn_ref, g * rr * HEAD_DIM, rr).astype(BF16)
            qr = _stack_heads(qr_ref, g * rr * HEAD_DIM, rr).astype(BF16)
            qrs_ref[g] = qr
            kc = cmp_ref[:, g * HEAD_DIM:(g + 1) * HEAD_DIM].astype(BF16)
            vc = cmp_ref[:, NSA_KV_W + g * HEAD_DIM:NSA_KV_W + (g + 1) * HEAD_DIM].astype(BF16)
            p_c = _masked_softmax(_dot_nt(qn, kc) * SCALE2, keep_c, axis=-1)
            oc_ref[g * rows:(g + 1) * rows, :] = _dot(p_c.astype(BF16), vc)
            imp = p_c[0:ts]
            for r in range(1, rr):
                imp = imp + p_c[r * ts:(r + 1) * ts]
            score = jnp.where(blk == cur, SEL_FORCE, jnp.where(valid, imp, -1.0))
            score = jnp.where(exists, score, NEG_INF)
            sel = _select_blocks(score, min(NSA_TOP_N, (past + ts + NSA_BLOCK - 1) // NSA_BLOCK), axis=-1)
            sel_ref[g * rows:(g + 1) * rows, :] = tile_rows(sel, rr).astype(BF16)
            kw = _slab(win_ref, g, WIN_SLABS, wb).astype(BF16)
            vw = _slab(win_ref, gg + g, WIN_SLABS, wb).astype(BF16)
            kwn = _slab(wnew_ref, g, WIN_SLABS, PAGE_SIZE).astype(BF16)
            vwn = _slab(wnew_ref, gg + g, WIN_SLABS, PAGE_SIZE).astype(BF16)
            s_w = jnp.concatenate([_dot_nt(qr, kw), _dot_nt(qr, kwn)], axis=1) * SCALE2
            p_w = _masked_softmax(s_w, keep_w, axis=-1).astype(BF16)
            ow_ref[g * rows:(g + 1) * rows, :] = _dot(p_w[:, 0:wb], vw) + _dot(p_w[:, wb:wb + PAGE_SIZE], vwn)
        nwin_ref[0:(wb - ts) * WIN_SLABS, :] = win_ref[ts * WIN_SLABS:wb * WIN_SLABS, :]
        nwin_ref[(wb - ts) * WIN_SLABS:wb * WIN_SLABS, :] = wnew_ref[0:ts * WIN_SLABS, :]

    def attend(pages, key_start, keep_tok):
        n_keys = len(pages) * PAGE_SIZE
        keep = _expand_blocks(sel_ref[...], key_start, n_keys)
        if keep_tok is not None:
            keep = keep * keep_tok
        s = jnp.concatenate(
            [jnp.concatenate([_dot_nt(qrs_ref[g], _slab(pg, 2 * gg + g, PAGE_SLABS, PAGE_SIZE).astype(BF16))
                              for pg in pages], axis=1)
             for g in range(gg)], axis=0) * SCALE2
        s = jnp.where(keep > 0.5, s, NEG_INF)

        def pv(pb):
            out = []
            for g in range(gg):
                acc = None
                for u, pg in enumerate(pages):
                    part = _dot(pb[g * rows:(g + 1) * rows, u * PAGE_SIZE:(u + 1) * PAGE_SIZE],
                                _slab(pg, 3 * gg + g, PAGE_SLABS, PAGE_SIZE).astype(BF16))
                    acc = part if acc is None else acc + part
                out.append(acc)
            return jnp.concatenate(out, axis=0)

        _online_update(s, pv, m_ref, l_ref, acc_ref, guard=True)

    attend(kv_refs, p * (pp * PAGE_SIZE), None)

    @pl.when(p == n_steps - 1)
    def _():
        keep_tok = tile_rows(jnp.where(_iota((ts, PAGE_SIZE), 1) <= qi_col, 1.0, 0.0), gg * rr)
        attend([new_ref], past, keep_tok)
        o_s = _normalize(acc_ref[...], l_ref[...])
        gates = g_ref[...]

        def gate(br):
            return jnp.concatenate([gates[:, g * LANES + br * rr + r:g * LANES + br * rr + r + 1]
                                    for g in range(gg) for r in range(rr)], axis=0)

        o = gate(0) * oc_ref[...] + gate(1) * o_s + gate(2) * ow_ref[...]
        for h in range(NSA_HEADS):
            o_ref[:, h * HEAD_DIM:(h + 1) * HEAD_DIM] = o[h * ts:(h + 1) * ts, :]


def _nsa_decode(pt_flat, qn_s, qr_s, cmp_s, cache_nsa, new_pad, win_buf, wnew_pad, gates_s,
                db, ts, n_pages, pp):
    rows = NSA_HEADS * ts
    past = n_pages * PAGE_SIZE
    n_steps = n_pages // pp
    nbp = cmp_s.shape[1]
    wb = win_buf.shape[1] // WIN_SLABS
    page = (PAGE_SIZE * PAGE_SLABS, HEAD_DIM)
    whole = lambda shape: pl.BlockSpec((None,) + shape, lambda b, p, pt: (b,) + (0,) * len(shape))
    grid_spec = pltpu.PrefetchScalarGridSpec(
        num_scalar_prefetch=1,
        grid=(db, n_steps),
        in_specs=[whole((ts, NSA_W)), whole((ts, NSA_W)), whole((nbp, 2 * NSA_KV_W))]
        + _paged_specs(page, n_pages, pp)
        + [whole(page), whole((wb * WIN_SLABS, HEAD_DIM)), whole((PAGE_SIZE * WIN_SLABS, HEAD_DIM)),
           whole((ts, NSA_KV_HEADS * LANES))],
        out_specs=[whole((ts, NSA_W)), whole((wb * WIN_SLABS, HEAD_DIM))],
        scratch_shapes=[pltpu.VMEM((NSA_KV_HEADS, NSA_GROUP * ts, HEAD_DIM), BF16),
                        pltpu.VMEM((rows, HEAD_DIM), F32),
                        pltpu.VMEM((rows, HEAD_DIM), F32),
                        pltpu.VMEM((rows, nbp), BF16),
                        pltpu.VMEM((rows, 1), F32), pltpu.VMEM((rows, 1), F32),
                        pltpu.VMEM((rows, HEAD_DIM), F32)],
    )
    return pl.pallas_call(
        functools.partial(_nsa_decode_body, n_steps=n_steps, pp=pp, ts=ts, past=past, nbp=nbp, wb=wb),
        grid_spec=grid_spec,
        out_shape=[jax.ShapeDtypeStruct((db, ts, NSA_W), F32),
                   jax.ShapeDtypeStruct((db, wb * WIN_SLABS, HEAD_DIM), F32)],
        compiler_params=_params("parallel", "arbitrary"),
        name="nsa_decode",
    )(pt_flat, qn_s, qr_s, cmp_s, *([cache_nsa] * pp), new_pad, win_buf, wnew_pad, gates_s)


def _topk_rows(x, k):
    n = x.shape[0]
    row = _iota(x.shape, 0).astype(F32)
    vals, idxs = [], []
    for _ in range(k):
        mx = jnp.max(x, axis=0, keepdims=True)
        ix = jnp.min(jnp.where(x == mx, row, float(n)), axis=0, keepdims=True)
        x = jnp.where(row == ix, NEG_INF, x)
        vals.append(mx)
        idxs.append(ix)
    return jnp.concatenate(vals, axis=0), jnp.concatenate(idxs, axis=0)


def _peer_select_body(q_ref, keys_ref, a_ref, b_ref, g_ref):
    tm = q_ref.shape[0]
    kk = PEER_TOPK
    a_all, b_all, g_all = [], [], []
    for h in range(PEER_HEADS):
        sub = []
        for c in range(2):
            keys = keys_ref[h, c].astype(BF16)
            col0 = (h * 2 + c) * PEER_HALF
            sub.append(_topk_rows(_dot_nt(keys, q_ref[:, col0:col0 + PEER_HALF]), kk))
        (s1, i1), (s2, i2) = sub
        width = [kk // (i + 1) for i in range(kk)]
        n_cand = sum(width)
        pad = -(-n_cand // 8) * 8 - n_cand
        fill = lambda v: [jnp.full((pad, tm), v, F32)] if pad else []
        cand = jnp.concatenate([s1[i:i + 1] + s2[0:width[i]] for i in range(kk)] + fill(NEG_INF), axis=0)
        ca = jnp.concatenate([jnp.broadcast_to(i1[i:i + 1], (width[i], tm)) for i in range(kk)] + fill(0.0), axis=0)
        cb = jnp.concatenate([i2[0:width[i]] for i in range(kk)] + fill(0.0), axis=0)
        row = _iota(cand.shape, 0).astype(F32)
        tops, a_h, b_h = [], [], []
        for _ in range(kk):
            mx = jnp.max(cand, axis=0, keepdims=True)
            ix = jnp.min(jnp.where(cand == mx, row, float(cand.shape[0])), axis=0, keepdims=True)
            hit = row == ix
            a_h.append(jnp.sum(jnp.where(hit, ca, 0.0), axis=0, keepdims=True))
            b_h.append(jnp.sum(jnp.where(hit, cb, 0.0), axis=0, keepdims=True))
            cand = jnp.where(hit, NEG_INF, cand)
            tops.append(mx)
        top = jnp.concatenate(tops, axis=0)
        e = jnp.exp(top - top[0:1])
        g_all.append(e / jnp.sum(e, axis=0, keepdims=True))
        a_all.append(jnp.concatenate(a_h, axis=0))
        b_all.append(jnp.concatenate(b_h, axis=0))
    a_ref[...] = jnp.concatenate(a_all, axis=0).T
    b_ref[...] = jnp.concatenate(b_all, axis=0).T
    g_ref[...] = jnp.concatenate(g_all, axis=0).T


def _peer_select(q, peer_keys):
    n = q.shape[0]
    tm = _tile(n, LANES, LANES)
    slots = PEER_HEADS * PEER_TOPK
    out = jax.ShapeDtypeStruct((n, slots), F32)
    spec = pl.BlockSpec((tm, slots), lambda i: (i, 0))
    return pl.pallas_call(
        _peer_select_body,
        grid=(n // tm,),
        in_specs=[pl.BlockSpec((tm, q.shape[1]), lambda i: (i, 0)),
                  pl.BlockSpec(peer_keys.shape, lambda i: (0, 0, 0, 0))],
        out_specs=[spec, spec, spec],
        out_shape=[out, out, out],
        compiler_params=_params("parallel"),
        name="peer_select",
    )(q, peer_keys)


def _peer_coef_body(a_ref, b_ref, g_ref, act_ref, o_ref, w_ref):
    tt = a_ref.shape[0]
    slots = a_ref.shape[1]
    sub = _iota((PEER_KEYS, slots), 0).astype(F32).astype(BF16)
    one, zero = jnp.ones((), BF16), jnp.zeros((), BF16)
    for t in range(tt):
        g_row = g_ref[t:t + 1, :]
        g_hi = g_row.astype(BF16)
        g_lo = (g_row - g_hi.astype(F32)).astype(BF16)
        a_hot = jnp.where(sub == a_ref[t:t + 1, :].astype(BF16), one, zero)
        hit = sub == b_ref[t:t + 1, :].astype(BF16)
        b_hi = jnp.where(hit, g_hi, zero)
        b_lo = jnp.where(hit, g_lo, zero)
        w_ref[:, t, :] = _dot_nt(jnp.concatenate([a_hot, a_hot], axis=1),
                                 jnp.concatenate([b_hi, b_lo], axis=1))
    o_ref[...] = (w_ref[...] * act_ref[...]).astype(o_ref.dtype)


def _peer_coef(a_sel, b_sel, g_sel, act):
    n, slots = a_sel.shape
    tt = _tile(n, 16)
    spec = pl.BlockSpec((tt, slots), lambda i: (i, 0))
    cube = pl.BlockSpec((PEER_KEYS, tt, PEER_KEYS), lambda i: (0, i, 0))
    return pl.pallas_call(
        _peer_coef_body,
        grid=(n // tt,),
        in_specs=[spec, spec, spec, cube],
        out_specs=cube,
        out_shape=jax.ShapeDtypeStruct((PEER_KEYS, n, PEER_KEYS), BF16),
        scratch_shapes=[pltpu.VMEM((PEER_KEYS, tt, PEER_KEYS), F32)],
        compiler_params=_params("parallel"),
        name="peer_coef",
    )(a_sel, b_sel, g_sel, act)


def _rope_tables(pos):
    half = HEAD_DIM // 2
    inv = ROPE_THETA ** (-jnp.arange(half, dtype=F32) / half)
    ang = pos.astype(F32)[:, None] * inv[None, :]
    cos, sin = jnp.cos(ang), jnp.sin(ang)
    return jnp.concatenate([cos, cos], axis=-1), jnp.concatenate([-sin, sin], axis=-1)


def _align_w_in(w_in):
    d = w_in.shape[0]
    o_f = FOX_W + 2 * FOX_KV_W
    o_q = o_f + FOX_HEADS
    o_g = o_q + NSA_W + 6 * NSA_KV_W
    main = jnp.concatenate([w_in[:, :o_f], w_in[:, o_q:o_g]], axis=1)
    forget = jnp.pad(w_in[:, o_f:o_q], ((0, 0), (0, LANES - FOX_HEADS)))
    gates = w_in[:, o_g:].reshape(d, NSA_KV_HEADS, NSA_GROUP, NSA_BRANCHES)
    gates = jnp.transpose(gates, (0, 1, 3, 2)).reshape(d, NSA_KV_HEADS, NSA_BRANCHES * NSA_GROUP)
    gates = jnp.pad(gates, ((0, 0), (0, 0), (0, LANES - NSA_BRANCHES * NSA_GROUP)))
    return jnp.concatenate([main, forget, gates.reshape(d, NSA_KV_HEADS * LANES)], axis=1)


def _pad_axis1(a, size):
    return jnp.pad(a, ((0, 0), (0, size - a.shape[1])) + ((0, 0),) * (a.ndim - 2))


def _layer(x, p_ple, cache_fox_kv, cache_fox_logf, cache_nsa_kv, state_win, pt_flat, dims,
           g_mix, w_in, b_f, w_ck, w_cv, w_a, w_b, w_merge, b_merge, w_o,
           g_ffn, w_pq, peer_keys, peer_u, peer_v, g_ple, w_ple_gate, w_ple):
    b, t, db, ts, n_pages = dims
    n, d = x.shape
    n_p = b * t
    past = n_pages * PAGE_SIZE
    pp = _tile(n_pages, PAGES_PER_STEP, 1)
    pp_attn = _tile(n_pages, ATTN_PAGES_PER_STEP, 1)

    h = _rmsnorm(x, g_mix, BF16)
    proj = _matmul(h, _align_w_in(w_in), tm=832, tn=640, tk=d, out_dtype=F32, name="in_proj")
    pos = jnp.concatenate([jnp.tile(jnp.arange(t), b), jnp.tile(past + jnp.arange(ts), db)])
    cos, sin = _rope_tables(pos)
    bf_row = jnp.pad(b_f, (0, LANES - FOX_HEADS)).reshape(1, LANES)
    qf, fkv, qn, qr, nkv, win, logf, gates = _post_projection(proj, cos, sin, bf_row, 0, n_p)
    sample_rows = _post_projection(proj, cos, sin, bf_row, n_p, n - n_p)
    qf_s, fkv_s, qn_s, qr_s, nkv_s, win_s, logf_s, gates_s = [a.reshape(db, ts, -1) for a in sample_rows]

    lf_p_t = jnp.transpose(logf.reshape(b, t, FOX_HEADS), (0, 2, 1))
    o_f_p = _fox_prompt(qf, fkv, _cumsum_prompt(lf_p_t), b, t)
    w_cmp = jnp.concatenate([jnp.tile(w_ck, (1, NSA_KV_HEADS)), jnp.tile(w_cv, (1, NSA_KV_HEADS))], axis=1)
    cmp_p = _compress_rows(nkv, w_cmp, n_p)
    o_n_p = _nsa_prompt(qn, qr, cmp_p, nkv, win, gates, b, t)

    n_pool = cache_fox_logf.shape[0]
    lc_pool = _page_cumsum(jnp.transpose(cache_fox_logf, (0, 2, 1)).reshape(n_pool * FOX_HEADS, PAGE_SIZE))
    lf_new_t = jnp.pad(jnp.transpose(logf_s, (0, 2, 1)), ((0, 0), (0, 0), (0, PAGE_SIZE - ts)))
    lc_new = _page_cumsum(lf_new_t.reshape(db * FOX_HEADS, PAGE_SIZE))
    slab_rows = lambda a: a.reshape(a.shape[0], -1, HEAD_DIM)
    fkv_new = slab_rows(_pad_axis1(fkv_s, PAGE_SIZE))
    o_f_s = _fox_decode(pt_flat, qf_s.astype(F32), slab_rows(cache_fox_kv),
                        lc_pool.reshape(n_pool, FOX_HEADS, PAGE_SIZE), fkv_new,
                        lc_new.reshape(db, FOX_HEADS, PAGE_SIZE), db, ts, n_pages, pp_attn)

    cache_nsa = slab_rows(cache_nsa_kv)
    nkv_new = _pad_axis1(nkv_s, PAGE_SIZE)
    cmp_new = _compress_rows(nkv_new.reshape(db * PAGE_SIZE, 4 * NSA_KV_W), w_cmp, db * PAGE_SIZE)
    cmp_s = jnp.concatenate([_compress_paged(pt_flat, cache_nsa, w_cmp, db, n_pages, pp),
                             cmp_new.reshape(db, PAGE_SIZE // NSA_BLOCK, 2 * NSA_KV_W)], axis=1)
    cmp_s = _pad_axis1(cmp_s, -(-cmp_s.shape[1] // LANES) * LANES)
    wb = state_win.shape[1]
    o_n_s, new_win = _nsa_decode(
        pt_flat, qn_s.astype(F32), qr_s.astype(F32), cmp_s, cache_nsa, slab_rows(nkv_new),
        slab_rows(state_win), slab_rows(_pad_axis1(win_s, PAGE_SIZE)),
        gates_s, db, ts, n_pages, pp_attn)

    o_f = jnp.concatenate([o_f_p, o_f_s.reshape(db * ts, FOX_W).astype(BF16)], axis=0)
    o_n = jnp.concatenate([o_n_p, o_n_s.reshape(db * ts, NSA_W).astype(BF16)], axis=0)

    gate = _matmul(h, w_merge, out_dtype=F32, name="merge_gate", **FULL_K,
                   epilogue=lambda acc, bias: jax.nn.sigmoid(acc + bias), extra=[(b_merge.reshape(1, -1), "row", 0)])
    m_a = _matmul(o_f, w_a, tm=1664, tn=512, tk=2048, out_dtype=F32, name="merge_fox",
                  epilogue=lambda acc, ga: ga * acc, extra=[(gate, "tile", 0)])
    tn_b = _tile(d, 512, LANES)
    mixed = _matmul(o_n, w_b, tm=1664, tn=512, tk=2048, out_dtype=BF16, name="merge_nsa",
                    epilogue=lambda acc, gb, ma: ma + gb * acc,
                    extra=[(gate, "tile", d // tn_b), (m_a, "tile", 0)])
    x1 = _matmul(mixed, w_o, out_dtype=F32, name="out_proj", **FULL_K,
                 epilogue=lambda acc, res: res + acc, extra=[(x, "tile", 0)])

    h2 = _rmsnorm(x1, g_ffn, BF16)
    pq = _matmul(h2, w_pq, out_dtype=BF16, name="peer_query", **FULL_K)
    a_sel, b_sel, g_sel = _peer_select(pq, peer_keys)
    act = _matmul(h2, peer_u, out_dtype=F32, nt=True, split_out=True, epilogue=_gelu_tanh, name="peer_act",
                  **FULL_K)
    coef = _peer_coef(a_sel, b_sel, g_sel, act)
    x2 = _matmul(coef, peer_v, tm=1664, tn=512, tk=2048, out_dtype=F32, stacked_lhs=True, name="peer_out",
                 epilogue=lambda acc, res: res + acc, extra=[(x1, "tile", 0)])

    h3 = _rmsnorm(x2, g_ple, BF16)
    x3 = _matmul(h3, w_ple_gate, out_dtype=F32, name="ple_gate", **FULL_K,
                 epilogue=lambda acc, res, p_rows, w_cols: (
                     res + jax.nn.sigmoid(acc) * _dot(p_rows, w_cols.astype(BF16))),
                 extra=[(x2, "tile", 0), (p_ple.astype(BF16), "lhs", 0), (w_ple, "rhs", 0)])

    caches = dict(
        fox_kv_p=fkv.reshape(b, t, 2, FOX_KV_HEADS, HEAD_DIM),
        fox_lf_p=logf.reshape(b, t, FOX_HEADS),
        nsa_kv_p=nkv.reshape(b, t, 4, NSA_KV_HEADS, HEAD_DIM),
        nsa_win_p=win.reshape(b, t, 2, NSA_KV_HEADS, HEAD_DIM)[:, t - min(NSA_WINDOW, t):],
        fox_kv_s=fkv_s.reshape(db, ts, 2, FOX_KV_HEADS, HEAD_DIM),
        fox_lf_s=logf_s,
        nsa_kv_s=nkv_s.reshape(db, ts, 4, NSA_KV_HEADS, HEAD_DIM),
        nsa_win_s=new_win.reshape(db, wb, 2, NSA_KV_HEADS, HEAD_DIM),
    )
    return x3, caches


def kernel(x_prompt, x_sample, cache_fox_kv, cache_fox_logf, cache_nsa_kv, state_nsa_win, page_table, p_prompt, p_sample, g_mix, w_in, b_f, w_ck, w_cv, w_a, w_b, w_merge, b_merge, w_o, g_ffn, w_pq, peer_keys, peer_u, peer_v, g_ple, w_ple_gate, w_ple, g_final):
    b, t, d = x_prompt.shape
    db, ts, _ = x_sample.shape
    depth = g_mix.shape[0]
    n_pages = page_table.shape[1]
    n_p = b * t
    dims = (b, t, db, ts, n_pages)
    pt_flat = page_table.reshape(-1).astype(jnp.int32)
    x = jnp.concatenate([x_prompt.reshape(n_p, d), x_sample.reshape(db * ts, d)], axis=0)
    per_layer = []
    for i in range(depth):
        p_ple = jnp.concatenate([p_prompt[i].reshape(n_p, -1), p_sample[i].reshape(db * ts, -1)], axis=0)
        x, caches = _layer(x, p_ple, cache_fox_kv[i], cache_fox_logf[i], cache_nsa_kv[i], state_nsa_win[i],
                           pt_flat, dims, g_mix[i], w_in[i], b_f[i], w_ck[i], w_cv[i], w_a[i], w_b[i],
                           w_merge[i], b_merge[i], w_o[i], g_ffn[i], w_pq[i], peer_keys[i], peer_u[i],
                           peer_v[i], g_ple[i], w_ple_gate[i], w_ple[i])
        per_layer.append(caches)
    y_p = _rmsnorm(x, g_final, F32, 0, n_p)
    y_s = _rmsnorm(x, g_final, F32, n_p, db * ts)
    stack = lambda name: jnp.stack([c[name] for c in per_layer])
    return (y_p.reshape(b, t, d), y_s.reshape(db, ts, d),
            stack("fox_kv_p"), stack("fox_lf_p"), stack("nsa_kv_p"), stack("nsa_win_p"),
            stack("fox_kv_s"), stack("fox_lf_s"), stack("nsa_kv_s"), stack("nsa_win_s"))
```

```python
import functools

import jax
import jax.numpy as jnp
from jax import lax
from jax.experimental import pallas as pl
from jax.experimental.pallas import tpu as pltpu

F32 = jnp.float32
BF16 = jnp.bfloat16
NEG_INF = float("-inf")

HEAD_DIM = 128
FOX_HEADS = 16
FOX_KV_HEADS = 4
FOX_GROUP = FOX_HEADS // FOX_KV_HEADS
NSA_HEADS = 16
NSA_KV_HEADS = 2
NSA_GROUP = NSA_HEADS // NSA_KV_HEADS
NSA_BLOCK = 64
BLOCK_SHIFT = NSA_BLOCK.bit_length() - 1
NSA_TOP_N = 16
NSA_WINDOW = 512
NSA_BRANCHES = 3
SEL_FORCE = 1e4
ROPE_THETA = 10000.0
PAGE_SIZE = 128
PEER_HEADS = 8
PEER_KEYS = 128
PEER_TOPK = 16
PEER_HALF = 128
RMS_EPS = 1e-6
SCALE = HEAD_DIM ** -0.5
LOG2E = 1.4426950408889634
SCALE2 = SCALE * LOG2E

FOX_W = FOX_HEADS * HEAD_DIM
FOX_KV_W = FOX_KV_HEADS * HEAD_DIM
NSA_W = NSA_HEADS * HEAD_DIM
NSA_KV_W = NSA_KV_HEADS * HEAD_DIM
LANES = 128
VMEM_LIMIT = 56 * 1024 * 1024
PAGES_PER_STEP = 8
ATTN_PAGES_PER_STEP = 16
PAGE_SLABS = 8
WIN_SLABS = 2 * NSA_KV_HEADS

NT_DIMS = (((1,), (1,)), ((), ()))
FULL_K = dict(tm=832, tn=512, tk=4096)
IN_PROJ_TN = FULL_K["tn"]


def _params(*sem):
    return pltpu.CompilerParams(dimension_semantics=sem, vmem_limit_bytes=VMEM_LIMIT)


def _tile(n, target, align=8):
    if n <= target:
        return n
    best = None
    for t in range(align, target + 1, align):
        if n % t == 0:
            best = t
    assert best is not None, (n, target, align)
    return best


def _dot(a, b):
    return jnp.dot(a, b, preferred_element_type=F32)


def _dot_nt(a, b):
    return lax.dot_general(a, b, NT_DIMS, preferred_element_type=F32)


def _dot_tn(a, b):
    k = a.shape[0]
    kp = -(-k // LANES) * LANES
    if kp != k:
        a = jnp.concatenate([a, jnp.zeros((kp - k, a.shape[1]), a.dtype)], axis=0)
        b = jnp.concatenate([b, jnp.zeros((kp - k, b.shape[1]), b.dtype)], axis=0)
    return _dot(a.T.astype(BF16), b.astype(BF16))


def _iota(shape, dim, dtype=jnp.int32):
    return lax.broadcasted_iota(dtype, shape, dim)


def _rmsnorm_body(x_ref, g_ref, o_ref):
    x = x_ref[...]
    y = x * lax.rsqrt(jnp.mean(x * x, axis=-1, keepdims=True) + RMS_EPS)
    o_ref[...] = (y * g_ref[...]).astype(o_ref.dtype)


def _rmsnorm(x, g, out_dtype, row0=0, n=None):
    d = x.shape[1]
    n = x.shape[0] if n is None else n
    tr = _tile(n, 256)
    while row0 % tr:
        tr //= 2
    first = row0 // tr
    return pl.pallas_call(
        _rmsnorm_body,
        grid=(n // tr,),
        in_specs=[pl.BlockSpec((tr, d), lambda i: (first + i, 0)), pl.BlockSpec((1, d), lambda i: (0, 0))],
        out_specs=pl.BlockSpec((tr, d), lambda i: (i, 0)),
        out_shape=jax.ShapeDtypeStruct((n, d), out_dtype),
        compiler_params=_params("parallel"),
        name="rmsnorm",
    )(x, g.reshape(1, d))


def _gelu_tanh(x):
    return 0.5 * x * (1.0 + jnp.tanh(0.7978845608028654 * (x + 0.044715 * (x * x * x))))


def _mm_body(*refs, n_extra, epilogue, nt, nk, split_out):
    x_ref, w_ref = refs[0], refs[1]
    extra = refs[2:2 + n_extra]
    o_ref = refs[2 + n_extra]
    def product():
        w = w_ref[...].astype(BF16)
        return _dot_nt(x_ref[...], w) if nt else _dot(x_ref[...], w)

    def finish(acc):
        res = epilogue(acc, *[e[...] for e in extra]).astype(o_ref.dtype)
        if split_out:
            for j in range(o_ref.shape[0]):
                o_ref[j] = res[:, j * LANES:(j + 1) * LANES]
        else:
            o_ref[...] = res

    if nk == 1:
        finish(product())
        return
    acc_ref = refs[3 + n_extra]
    k = pl.program_id(2)

    @pl.when(k == 0)
    def _():
        acc_ref[...] = jnp.zeros_like(acc_ref)

    acc_ref[...] += product()

    @pl.when(k == nk - 1)
    def _():
        finish(acc_ref[...])


def _matmul(x, w, *, tm, tn, tk, out_dtype, name, epilogue=None, extra=(), nt=False, split_out=False):
    m, kdim = x.shape
    n = w.shape[0] if nt else w.shape[1]
    tm, tn, tk = _tile(m, tm), _tile(n, tn, LANES), _tile(kdim, tk, LANES)
    nk = kdim // tk
    if epilogue is None:
        epilogue = lambda acc: acc
    in_specs = [
        pl.BlockSpec((tm, tk), lambda i, j, k: (i, k)),
        pl.BlockSpec((tn, tk), lambda i, j, k: (j, k)) if nt else pl.BlockSpec((tk, tn), lambda i, j, k: (k, j)),
    ]
    args = [x, w]
    for arr, kind, off in extra:
        if kind == "tile":
            in_specs.append(pl.BlockSpec((tm, tn), lambda i, j, k, off=off: (i, j + off)))
        elif kind == "row":
            in_specs.append(pl.BlockSpec((1, tn), lambda i, j, k, off=off: (0, j + off)))
        elif kind == "lhs":
            in_specs.append(pl.BlockSpec((tm, arr.shape[1]), lambda i, j, k: (i, 0)))
        else:
            assert kind == "rhs", kind
            in_specs.append(pl.BlockSpec((arr.shape[0], tn), lambda i, j, k: (0, j)))
        args.append(arr)
    if split_out:
        out_shape = jax.ShapeDtypeStruct((n // LANES, m, LANES), out_dtype)
        out_spec = pl.BlockSpec((tn // LANES, tm, LANES), lambda i, j, k: (j, i, 0))
    else:
        out_shape = jax.ShapeDtypeStruct((m, n), out_dtype)
        out_spec = pl.BlockSpec((tm, tn), lambda i, j, k: (i, j))
    body = functools.partial(_mm_body, n_extra=len(extra), epilogue=epilogue, nt=nt, nk=nk,
                             split_out=split_out)
    return pl.pallas_call(
        body,
        grid=(m // tm, n // tn, nk),
        in_specs=in_specs,
        out_specs=out_spec,
        out_shape=out_shape,
        scratch_shapes=[pltpu.VMEM((tm, tn), F32)] if nk > 1 else [],
        compiler_params=_params("parallel", "parallel", "arbitrary"),
        name=name,
    )(*args)


def _rope(x, cos, sin_signed):
    return x * cos + pltpu.roll(x, HEAD_DIM // 2, 1) * sin_signed


def _log_sigmoid(z):
    return jnp.minimum(z, 0.0) - jnp.log1p(jnp.exp(-jnp.abs(z)))


def _post_body(p_ref, cos_ref, sin_ref, bf_ref,
               qf_ref, fkv_ref, qn_ref, qr_ref, nkv_ref, win_ref, logf_ref, gate_ref):
    cos = cos_ref[...]
    sin = sin_ref[...]
    o = 0
    qf_ref[...] = p_ref[:, o:o + FOX_W].astype(BF16)
    o += FOX_W
    fkv_ref[...] = p_ref[:, o:o + 2 * FOX_KV_W]
    o += 2 * FOX_KV_W
    for h in range(NSA_HEADS):
        q = p_ref[:, o + h * HEAD_DIM:o + (h + 1) * HEAD_DIM]
        qn_ref[:, h * HEAD_DIM:(h + 1) * HEAD_DIM] = q.astype(BF16)
        qr_ref[:, h * HEAD_DIM:(h + 1) * HEAD_DIM] = _rope(q, cos, sin).astype(BF16)
    o += NSA_W
    nkv_ref[:, 0:2 * NSA_KV_W] = p_ref[:, o:o + 2 * NSA_KV_W]
    o += 2 * NSA_KV_W
    for g in range(NSA_KV_HEADS):
        k = p_ref[:, o + g * HEAD_DIM:o + (g + 1) * HEAD_DIM]
        nkv_ref[:, 2 * NSA_KV_W + g * HEAD_DIM:2 * NSA_KV_W + (g + 1) * HEAD_DIM] = _rope(k, cos, sin)
    o += NSA_KV_W
    nkv_ref[:, 3 * NSA_KV_W:4 * NSA_KV_W] = p_ref[:, o:o + NSA_KV_W]
    o += NSA_KV_W
    for g in range(NSA_KV_HEADS):
        k = p_ref[:, o + g * HEAD_DIM:o + (g + 1) * HEAD_DIM]
        win_ref[:, g * HEAD_DIM:(g + 1) * HEAD_DIM] = _rope(k, cos, sin)
    o += NSA_KV_W
    win_ref[:, NSA_KV_W:2 * NSA_KV_W] = p_ref[:, o:o + NSA_KV_W]
    o += NSA_KV_W
    logf = _log_sigmoid(p_ref[:, o:o + LANES] + bf_ref[...])
    logf_ref[...] = logf[:, 0:FOX_HEADS]
    gate_ref[...] = jax.nn.sigmoid(p_ref[:, o + LANES:o + 3 * LANES])


def _post_projection(p, cos, sin, bf_row, row0, n):
    tr = _tile(n, 256)
    while row0 % tr:
        tr //= 2
    first = row0 // tr
    row = lambda w: pl.BlockSpec((tr, w), lambda i: (i, 0))
    row_in = lambda w: pl.BlockSpec((tr, w), lambda i: (first + i, 0))
    out_shapes = [
        jax.ShapeDtypeStruct((n, FOX_W), BF16),
        jax.ShapeDtypeStruct((n, 2 * FOX_KV_W), F32),
        jax.ShapeDtypeStruct((n, NSA_W), BF16),
        jax.ShapeDtypeStruct((n, NSA_W), BF16),
        jax.ShapeDtypeStruct((n, 4 * NSA_KV_W), F32),
        jax.ShapeDtypeStruct((n, 2 * NSA_KV_W), F32),
        jax.ShapeDtypeStruct((n, FOX_HEADS), F32),
        jax.ShapeDtypeStruct((n, 2 * LANES), F32),
    ]
    return pl.pallas_call(
        _post_body,
        grid=(n // tr,),
        in_specs=[row_in(p.shape[1]), row_in(LANES), row_in(LANES),
                  pl.BlockSpec((1, LANES), lambda i: (0, 0))],
        out_specs=[row(s.shape[1]) for s in out_shapes],
        out_shape=out_shapes,
        compiler_params=_params("parallel"),
        name="post_projection",
    )(p, cos, sin, bf_row)


def _split3(x):
    hi = x.astype(BF16)
    r1 = x - hi.astype(F32)
    mid = r1.astype(BF16)
    lo = (r1 - mid.astype(F32)).astype(BF16)
    return hi, mid, lo


def _row_cumsum(x, u):
    hi, mid, lo = _split3(x)
    return (_dot(hi, u) + _dot(mid, u)) + _dot(lo, u)


def _cumsum_prompt_body(lf_ref, u_ref, o_ref, carry_ref):
    @pl.when(pl.program_id(1) == 0)
    def _():
        carry_ref[...] = jnp.zeros_like(carry_ref)

    c = _row_cumsum(lf_ref[...], u_ref[...]) + carry_ref[...]
    o_ref[...] = c
    r = c.shape[1]
    carry_ref[...] = c[:, r - 1:r]


def _upper_ones(r):
    return (jnp.arange(r)[:, None] <= jnp.arange(r)[None, :]).astype(BF16)


def _cumsum_prompt(lf_t):
    b, h, t = lf_t.shape
    r = _tile(t, 512, LANES)
    return pl.pallas_call(
        _cumsum_prompt_body,
        grid=(b, t // r),
        in_specs=[pl.BlockSpec((None, h, r), lambda i, j: (i, 0, j)),
                  pl.BlockSpec((r, r), lambda i, j: (0, 0))],
        out_specs=pl.BlockSpec((None, h, r), lambda i, j: (i, 0, j)),
        out_shape=jax.ShapeDtypeStruct((b, h, t), F32),
        scratch_shapes=[pltpu.VMEM((h, 1), F32)],
        compiler_params=_params("parallel", "arbitrary"),
        name="cumsum_prompt",
    )(lf_t, _upper_ones(r))


def _page_cumsum_body(x_ref, u_ref, o_ref):
    o_ref[...] = _row_cumsum(x_ref[...], u_ref[...])


def _page_cumsum(x):
    rows = x.shape[0]
    tr = _tile(rows, 2048)
    return pl.pallas_call(
        _page_cumsum_body,
        grid=(rows // tr,),
        in_specs=[pl.BlockSpec((tr, PAGE_SIZE), lambda i: (i, 0)),
                  pl.BlockSpec((PAGE_SIZE, PAGE_SIZE), lambda i: (0, 0))],
        out_specs=pl.BlockSpec((tr, PAGE_SIZE), lambda i: (i, 0)),
        out_shape=jax.ShapeDtypeStruct((rows, PAGE_SIZE), F32),
        compiler_params=_params("parallel"),
        name="page_cumsum",
    )(x, _upper_ones(PAGE_SIZE))


def _online_update_t(s, vt, m_ref, l_ref, acc_ref, guard):
    m_old = m_ref[...]
    m_new = jnp.maximum(m_old, jnp.max(s, axis=0, keepdims=True))
    m_use = jnp.where(m_new == NEG_INF, 0.0, m_new) if guard else m_new
    alpha = jnp.exp2(m_old - m_use)
    p = jnp.exp2(s - m_use)
    l_ref[...] = alpha * l_ref[...] + jnp.sum(p, axis=0, keepdims=True)
    acc_ref[...] = alpha * acc_ref[...] + _dot(vt, p.astype(BF16))
    m_ref[...] = m_new


def _pipelined_chunks(lo, hi, last, qk, consume, s_a, s_b):
    s_a[...] = qk(lo)

    def step(j, src, dst):
        dst[...] = qk(jnp.minimum(j + 1, last))
        consume(j, src[...])

    def pair(t, carry):
        j = lo + 2 * t
        step(j, s_a, s_b)

        @pl.when(j + 1 < hi)
        def _():
            step(j + 1, s_b, s_a)

        return carry

    lax.fori_loop(0, (hi - lo + 1) // 2, pair, 0)


def _online_update(s, pv_fn, m_ref, l_ref, acc_ref, guard):
    m_old = m_ref[...]
    m_new = jnp.maximum(m_old, jnp.max(s, axis=-1, keepdims=True))
    m_use = jnp.where(m_new == NEG_INF, 0.0, m_new) if guard else m_new
    alpha = jnp.exp2(m_old - m_use)
    p = jnp.exp2(s - m_use)
    l_ref[...] = alpha * l_ref[...] + jnp.sum(p, axis=-1, keepdims=True)
    acc_ref[...] = alpha * acc_ref[...] + pv_fn(p.astype(BF16))
    m_ref[...] = m_new


def _masked_softmax(s, keep, axis):
    s = jnp.where(keep > 0.5, s, NEG_INF)
    m = jnp.max(s, axis=axis, keepdims=True)
    m = jnp.where(m == NEG_INF, 0.0, m)
    e = jnp.exp2(s - m)
    return e / jnp.maximum(jnp.sum(e, axis=axis, keepdims=True), 1e-30)


def _normalize(acc, l):
    return acc / jnp.maximum(l, 1e-30)


def _init_state(*triples):
    for m_ref, l_ref, acc_ref in triples:
        m_ref[...] = jnp.full_like(m_ref, NEG_INF)
        l_ref[...] = jnp.zeros_like(l_ref)
        acc_ref[...] = jnp.zeros_like(acc_ref)


def _mask_heads(s, keep, n_heads, width):
    return jnp.concatenate(
        [jnp.where(keep, s[:, r * width:(r + 1) * width], NEG_INF) for r in range(n_heads)], axis=1)


def _fox_prompt_body(q_ref, qx_ref, k_ref, kx_ref, v_ref, o_ref,
                     qa_ref, ka_ref, vt_ref, sa_ref, sb_ref, m_ref, l_ref, acc_ref, *, tq, nq):
    i = pl.program_id(2)
    rr = FOX_GROUP

    @pl.when(i == 0)
    def _():
        def prep(j, carry):
            ks = pl.multiple_of(j * tq, tq)
            ka_ref[j, :, 0:HEAD_DIM] = k_ref[pl.ds(ks, tq), :].astype(BF16)
            ka_ref[j, :, HEAD_DIM:2 * HEAD_DIM] = kx_ref[pl.ds(ks, tq), :]
            vt_ref[j] = v_ref[pl.ds(ks, tq), :].T.astype(BF16)
            return carry

        lax.fori_loop(0, nq, prep, 0)

    for r in range(rr):
        qa_ref[r * tq:(r + 1) * tq, 0:HEAD_DIM] = q_ref[:, r * HEAD_DIM:(r + 1) * HEAD_DIM]
        qa_ref[r * tq:(r + 1) * tq, HEAD_DIM:2 * HEAD_DIM] = qx_ref[r]
    _init_state((m_ref, l_ref, acc_ref))

    qk = lambda j: _dot_nt(ka_ref[j], qa_ref[...]) * SCALE2
    full = lambda j, s: _online_update_t(s, vt_ref[j], m_ref, l_ref, acc_ref, guard=False)
    _pipelined_chunks(0, i, i, qk, full, sa_ref, sb_ref)

    def diagonal(src):
        causal = _iota((tq, tq), 0) <= _iota((tq, tq), 1)
        _online_update_t(_mask_heads(src[...], causal, rr, tq), vt_ref[i], m_ref, l_ref, acc_ref, guard=False)

    pl.when(lax.rem(i, 2) == 0)(lambda: diagonal(sa_ref))
    pl.when(lax.rem(i, 2) == 1)(lambda: diagonal(sb_ref))

    o_t = _normalize(acc_ref[...], l_ref[...])
    for r in range(rr):
        o_ref[:, r * HEAD_DIM:(r + 1) * HEAD_DIM] = o_t[:, r * tq:(r + 1) * tq].T.astype(o_ref.dtype)


def _fox_bias_columns(c_t, b, t):
    g_, r_ = FOX_KV_HEADS, FOX_GROUP
    pieces = jnp.stack(_split3(c_t * (1.0 / SCALE)), axis=-1).reshape(b, g_, r_, t, 3)
    kx = -jnp.transpose(pieces, (0, 1, 3, 2, 4)).reshape(b, g_, t, 3 * r_)
    kx = jnp.concatenate([kx, jnp.ones((b, g_, t, 3), BF16)], axis=-1)
    kx = jnp.pad(kx, ((0, 0), (0, 0), (0, 0), (0, LANES - kx.shape[-1])))
    own = (jnp.arange(r_)[:, None] == jnp.arange(3 * r_)[None, :] // 3).astype(BF16)
    own = jnp.broadcast_to(own[None, None, :, None, :], (b, g_, r_, t, 3 * r_))
    qx = jnp.concatenate([own, pieces], axis=-1)
    qx = jnp.pad(qx, ((0, 0), (0, 0), (0, 0), (0, 0), (0, LANES - qx.shape[-1])))
    return qx, kx


def _fox_prompt(qf, fkv, c_t, b, t):
    tq = _tile(t, 256, LANES)
    nq = t // tq
    g_, r_ = FOX_KV_HEADS, FOX_GROUP
    rows = r_ * tq
    qx, kx = _fox_bias_columns(c_t, b, t)
    return pl.pallas_call(
        functools.partial(_fox_prompt_body, tq=tq, nq=nq),
        grid=(b, g_, nq),
        in_specs=[
            pl.BlockSpec((tq, r_ * HEAD_DIM), lambda bi, g, i: (bi * nq + i, g)),
            pl.BlockSpec((None, None, r_, tq, LANES), lambda bi, g, i: (bi, g, 0, i, 0)),
            pl.BlockSpec((t, HEAD_DIM), lambda bi, g, i: (bi, g)),
            pl.BlockSpec((None, None, t, LANES), lambda bi, g, i: (bi, g, 0, 0)),
            pl.BlockSpec((t, HEAD_DIM), lambda bi, g, i: (bi, g_ + g)),
        ],
        out_specs=pl.BlockSpec((tq, r_ * HEAD_DIM), lambda bi, g, i: (bi * nq + i, g)),
        out_shape=jax.ShapeDtypeStruct((b * t, FOX_W), BF16),
        scratch_shapes=[pltpu.VMEM((rows, 2 * HEAD_DIM), BF16),
                        pltpu.VMEM((nq, tq, 2 * HEAD_DIM), BF16),
                        pltpu.VMEM((nq, HEAD_DIM, tq), BF16),
                        pltpu.VMEM((tq, rows), F32), pltpu.VMEM((tq, rows), F32),
                        pltpu.VMEM((1, rows), F32), pltpu.VMEM((1, rows), F32),
                        pltpu.VMEM((HEAD_DIM, rows), F32)],
        compiler_params=_params("parallel", "parallel", "arbitrary"),
        name="fox_prompt",
    )(qf, qx, fkv, kx, fkv)


def _compress_body(x_ref, w_ref, o_ref):
    x = x_ref[...]
    nb = x.shape[0] // NSA_BLOCK
    xb = x.reshape(nb, NSA_BLOCK, x.shape[1]) * w_ref[...][None]
    o_ref[...] = jnp.sum(xb, axis=1) * (1.0 / NSA_BLOCK)


def _compress_rows(nkv, w_cmp, n_rows):
    width = 2 * NSA_KV_W
    tr = _tile(n_rows, 512, NSA_BLOCK * 8)
    return pl.pallas_call(
        _compress_body,
        grid=(n_rows // tr,),
        in_specs=[pl.BlockSpec((tr, width), lambda i: (i, 0)),
                  pl.BlockSpec((NSA_BLOCK, width), lambda i: (0, 0))],
        out_specs=pl.BlockSpec((tr // NSA_BLOCK, width), lambda i: (i, 0)),
        out_shape=jax.ShapeDtypeStruct((n_rows // NSA_BLOCK, width), F32),
        compiler_params=_params("parallel"),
        name="compress_rows",
    )(nkv, w_cmp)


def _compress_paged_body(pt_ref, *refs, pp):
    page_refs, w_ref, o_ref = refs[:pp], refs[pp], refs[pp + 1]
    nb = PAGE_SIZE // NSA_BLOCK
    for u in range(pp):
        for j in range(2 * NSA_KV_HEADS):
            x = _slab(page_refs[u], j, PAGE_SLABS, PAGE_SIZE)
            xb = x.reshape(nb, NSA_BLOCK, HEAD_DIM) * w_ref[:, j * HEAD_DIM:(j + 1) * HEAD_DIM][None]
            o_ref[u, :, j * HEAD_DIM:(j + 1) * HEAD_DIM] = jnp.sum(xb, axis=1) * (1.0 / NSA_BLOCK)


def _paged_specs(block, n_pages, pp):
    return [pl.BlockSpec((None,) + block, lambda b, p, pt, u=u: (pt[b * n_pages + p * pp + u], 0, 0))
            for u in range(pp)]


def _slab(ref, j, n_slabs, n_rows):
    return ref[pl.ds(j, n_rows, stride=n_slabs), :]


def _compress_paged(pt_flat, cache_nsa, w_cmp, db, n_pages, pp):
    width = 2 * NSA_KV_W
    nb = PAGE_SIZE // NSA_BLOCK
    grid_spec = pltpu.PrefetchScalarGridSpec(
        num_scalar_prefetch=1,
        grid=(db, n_pages // pp),
        in_specs=_paged_specs((PAGE_SIZE * PAGE_SLABS, HEAD_DIM), n_pages, pp)
        + [pl.BlockSpec((NSA_BLOCK, width), lambda b, p, pt: (0, 0))],
        out_specs=pl.BlockSpec((None, pp, nb, width), lambda b, p, pt: (b, p, 0, 0)),
    )
    out = pl.pallas_call(
        functools.partial(_compress_paged_body, pp=pp),
        grid_spec=grid_spec,
        out_shape=jax.ShapeDtypeStruct((db, n_pages, nb, width), F32),
        compiler_params=_params("parallel", "arbitrary"),
        name="compress_paged",
    )(pt_flat, *([cache_nsa] * pp), w_cmp)
    return out.reshape(db, n_pages * nb, width)


def _select_blocks(score, n_sel, axis):
    axis = axis % score.ndim
    nb = score.shape[axis]
    blk = _iota(score.shape, axis).astype(F32)
    sel = jnp.zeros(score.shape, F32)
    work = score
    for _ in range(n_sel):
        mx = jnp.max(work, axis=axis, keepdims=True)
        ix = jnp.min(jnp.where(work == mx, blk, float(nb)), axis=axis, keepdims=True)
        hit = blk == ix
        sel = jnp.where(hit, jnp.where(mx >= 0.0, 1.0, 0.0), sel)
        work = jnp.where(hit, NEG_INF, work)
    return sel


def _expand_blocks(sel_b, key_start, n_keys):
    nb = sel_b.shape[1]
    key_blk = lax.shift_right_logical(key_start + _iota((nb, n_keys), 1), BLOCK_SHIFT)
    e = jnp.where(_iota((nb, n_keys), 0) == key_blk, 1.0, 0.0).astype(BF16)
    return _dot(sel_b, e)


def _nsa_prompt_body(qn_ref, qr_ref, kc_ref, vc_ref, ks_ref, vs_ref, kw_ref, vw_ref, g_ref, o_ref,
                     qs_ref, ksb_ref, vst_ref, kwb_ref, vwt_ref, sel_ref, oc_ref,
                     m_s, l_s, acc_s, m_w, l_w, acc_w, *, tq, nq, nb):
    i = pl.program_id(2)
    rr = NSA_GROUP
    bpc = tq // NSA_BLOCK

    @pl.when(i == 0)
    def _():
        def prep(j, carry):
            ks = pl.multiple_of(j * tq, tq)
            ksb_ref[j] = ks_ref[pl.ds(ks, tq), :].astype(BF16)
            vst_ref[j] = vs_ref[pl.ds(ks, tq), :].T.astype(BF16)
            kwb_ref[j] = kw_ref[pl.ds(ks, tq), :].astype(BF16)
            vwt_ref[j] = vw_ref[pl.ds(ks, tq), :].T.astype(BF16)
            return carry

        lax.fori_loop(0, nq, prep, 0)

    stack = lambda ref: jnp.concatenate([ref[:, r * HEAD_DIM:(r + 1) * HEAD_DIM] for r in range(rr)], axis=0)
    lanes = lambda a: jnp.concatenate([a] * rr, axis=1)
    qn = stack(qn_ref)
    qs_ref[...] = stack(qr_ref)
    pos_q = i * tq + _iota((1, tq), 1)
    _init_state((m_s, l_s, acc_s), (m_w, l_w, acc_w))

    blk = _iota((nb, tq), 0)
    valid = (blk + 1) * NSA_BLOCK - 1 <= pos_q
    s_c = _dot_nt(kc_ref[...].astype(BF16), qn) * SCALE2
    p_c = _masked_softmax(s_c, lanes(jnp.where(valid, 1.0, 0.0)), axis=0)
    oc_ref[...] = _dot_tn(vc_ref[...], p_c)
    imp = p_c[:, 0:tq]
    for r in range(1, rr):
        imp = imp + p_c[:, r * tq:(r + 1) * tq]
    cur = lax.shift_right_logical(pos_q, BLOCK_SHIFT)
    score = jnp.where(blk == cur, SEL_FORCE, jnp.where(valid, imp, -1.0))
    sel_ref[...] = _select_blocks(score, min(NSA_TOP_N, nb), axis=0)

    def sel_keep(j):
        return jnp.concatenate(
            [jnp.broadcast_to(sel_ref[pl.ds(j * bpc + u, 1), :], (NSA_BLOCK, tq)) for u in range(bpc)], axis=0)

    def sel_chunk(j, carry):
        s = _mask_heads(_dot_nt(ksb_ref[j], qs_ref[...]) * SCALE2, sel_keep(j) > 0.5, rr, tq)
        _online_update_t(s, vst_ref[j], m_s, l_s, acc_s, guard=True)
        return carry

    lax.fori_loop(0, i, sel_chunk, 0)
    causal = _iota((tq, tq), 0) <= _iota((tq, tq), 1)
    keep = jnp.where(causal, sel_keep(i), 0.0) > 0.5
    s = _mask_heads(_dot_nt(ksb_ref[i], qs_ref[...]) * SCALE2, keep, rr, tq)
    _online_update_t(s, vst_ref[i], m_s, l_s, acc_s, guard=True)

    def win_chunk(j, carry):
        dpos = pos_q - (j * tq + _iota((tq, 1), 0))
        keep = jnp.where(dpos >= 0, jnp.where(dpos < NSA_WINDOW, 1.0, 0.0), 0.0) > 0.5
        s = _mask_heads(_dot_nt(kwb_ref[j], qs_ref[...]) * SCALE2, keep, rr, tq)
        _online_update_t(s, vwt_ref[j], m_w, l_w, acc_w, guard=True)
        return carry

    lax.fori_loop(jnp.maximum(i - (NSA_WINDOW + tq - 1) // tq, 0), i + 1, win_chunk, 0)

    g_t = g_ref[...].T
    o_s = _normalize(acc_s[...], l_s[...])
    o_w = _normalize(acc_w[...], l_w[...])
    for r in range(rr):
        cols = slice(r * tq, (r + 1) * tq)
        gate = lambda br: g_t[br * rr + r:br * rr + r + 1, :]
        o = gate(0) * oc_ref[:, cols] + gate(1) * o_s[:, cols] + gate(2) * o_w[:, cols]
        o_ref[:, r * HEAD_DIM:(r + 1) * HEAD_DIM] = o.T.astype(o_ref.dtype)


def _nsa_prompt(qn, qr, cmp_p, nkv, win, gates, b, t):
    tq = _tile(t, 256, LANES)
    nq = t // tq
    nb = t // NSA_BLOCK
    g_, rows = NSA_KV_HEADS, NSA_GROUP * tq
    qspec = pl.BlockSpec((tq, NSA_GROUP * HEAD_DIM), lambda bi, g, i: (bi * nq + i, g))
    col = lambda off: pl.BlockSpec((t, HEAD_DIM), lambda bi, g, i: (bi, off + g))
    state = [pltpu.VMEM((1, rows), F32), pltpu.VMEM((1, rows), F32), pltpu.VMEM((HEAD_DIM, rows), F32)]
    return pl.pallas_call(
        functools.partial(_nsa_prompt_body, tq=tq, nq=nq, nb=nb),
        grid=(b, g_, nq),
        in_specs=[
            qspec, qspec,
            pl.BlockSpec((nb, HEAD_DIM), lambda bi, g, i: (bi, g)),
            pl.BlockSpec((nb, HEAD_DIM), lambda bi, g, i: (bi, g_ + g)),
            col(2 * g_), col(3 * g_),
            col(0), col(g_),
            pl.BlockSpec((tq, LANES), lambda bi, g, i: (bi * nq + i, g)),
        ],
        out_specs=qspec,
        out_shape=jax.ShapeDtypeStruct((b * t, NSA_W), BF16),
        scratch_shapes=[pltpu.VMEM((rows, HEAD_DIM), BF16),
                        pltpu.VMEM((nq, tq, HEAD_DIM), BF16), pltpu.VMEM((nq, HEAD_DIM, tq), BF16),
                        pltpu.VMEM((nq, tq, HEAD_DIM), BF16), pltpu.VMEM((nq, HEAD_DIM, tq), BF16),
                        pltpu.VMEM((nb, tq), F32),
                        pltpu.VMEM((HEAD_DIM, rows), F32)] + state + state,
        compiler_params=_params("parallel", "parallel", "arbitrary"),
        name="nsa_prompt",
    )(qn, qr, cmp_p, cmp_p, nkv, nkv, win, win, gates)


def _stack_heads(q_ref, col0, n_heads):
    return jnp.concatenate([q_ref[:, col0 + r * HEAD_DIM:col0 + (r + 1) * HEAD_DIM] for r in range(n_heads)], axis=0)


def _repeat_rows(a, times):
    return jnp.concatenate([jnp.broadcast_to(a[h:h + 1, :], (times, a.shape[1])) for h in range(a.shape[0])], axis=0)


def _fox_decode_body(pt_ref, q_ref, *refs, n_steps, pp, ts):
    kv_refs, lc_refs = refs[:pp], refs[pp:2 * pp]
    new_ref, lcn_ref, o_ref, qs_ref, carry_ref, m_ref, l_ref, acc_ref = refs[2 * pp:]
    p = pl.program_id(1)
    rr = FOX_GROUP
    rows = rr * ts

    @pl.when(p == 0)
    def _():
        _init_state((m_ref, l_ref, acc_ref))
        carry_ref[...] = jnp.zeros_like(carry_ref)
        for g in range(FOX_KV_HEADS):
            qs_ref[g] = _stack_heads(q_ref, g * rr * HEAD_DIM, rr).astype(BF16)

    def attend(pages, c_k, keep):
        bias = _repeat_rows(c_k * LOG2E, ts)
        s = jnp.concatenate(
            [jnp.concatenate([_dot_nt(qs_ref[g], _slab(pg, g, PAGE_SLABS, PAGE_SIZE).astype(BF16))
                              for pg in pages], axis=1)
             for g in range(FOX_KV_HEADS)], axis=0) * SCALE2 - bias
        if keep is not None:
            s = jnp.where(keep, s, NEG_INF)

        def pv(pb):
            out = []
            for g in range(FOX_KV_HEADS):
                acc = None
                for u, pg in enumerate(pages):
                    part = _dot(pb[g * rows:(g + 1) * rows, u * PAGE_SIZE:(u + 1) * PAGE_SIZE],
                                _slab(pg, FOX_KV_HEADS + g, PAGE_SLABS, PAGE_SIZE).astype(BF16))
                    acc = part if acc is None else acc + part
                out.append(acc)
            return jnp.concatenate(out, axis=0)

        _online_update(s, pv, m_ref, l_ref, acc_ref, guard=False)

    run = carry_ref[...]
    c_pages = []
    for u in range(pp):
        lc = lc_refs[u][...]
        c_pages.append(run + lc)
        run = run + lc[:, PAGE_SIZE - 1:PAGE_SIZE]
    carry_ref[...] = run
    attend(kv_refs, jnp.concatenate(c_pages, axis=1), None)

    @pl.when(p == n_steps - 1)
    def _():
        keep_i = jnp.where(_iota((ts, PAGE_SIZE), 1) <= _iota((ts, PAGE_SIZE), 0), 1.0, 0.0)
        keep = jnp.concatenate([keep_i] * FOX_HEADS, axis=0) > 0.5
        attend([new_ref], run + lcn_ref[...], keep)
        o = _normalize(acc_ref[...], l_ref[...])
        for h in range(FOX_HEADS):
            o_ref[:, h * HEAD_DIM:(h + 1) * HEAD_DIM] = o[h * ts:(h + 1) * ts, :]


def _fox_decode(pt_flat, q_s, cache_kv, lc_pool, new_pad, lc_new, db, ts, n_pages, pp):
    rows = FOX_HEADS * ts
    n_steps = n_pages // pp
    page = (PAGE_SIZE * PAGE_SLABS, HEAD_DIM)
    grid_spec = pltpu.PrefetchScalarGridSpec(
        num_scalar_prefetch=1,
        grid=(db, n_steps),
        in_specs=[pl.BlockSpec((None, ts, FOX_W), lambda b, p, pt: (b, 0, 0))]
        + _paged_specs(page, n_pages, pp)
        + _paged_specs((FOX_HEADS, PAGE_SIZE), n_pages, pp)
        + [pl.BlockSpec((None,) + page, lambda b, p, pt: (b, 0, 0)),
           pl.BlockSpec((None, FOX_HEADS, PAGE_SIZE), lambda b, p, pt: (b, 0, 0))],
        out_specs=pl.BlockSpec((None, ts, FOX_W), lambda b, p, pt: (b, 0, 0)),
        scratch_shapes=[pltpu.VMEM((FOX_KV_HEADS, FOX_GROUP * ts, HEAD_DIM), BF16),
                        pltpu.VMEM((FOX_HEADS, 1), F32),
                        pltpu.VMEM((rows, 1), F32), pltpu.VMEM((rows, 1), F32),
                        pltpu.VMEM((rows, HEAD_DIM), F32)],
    )
    return pl.pallas_call(
        functools.partial(_fox_decode_body, n_steps=n_steps, pp=pp, ts=ts),
        grid_spec=grid_spec,
        out_shape=jax.ShapeDtypeStruct((db, ts, FOX_W), F32),
        compiler_params=_params("parallel", "arbitrary"),
        name="fox_decode",
    )(pt_flat, q_s, *([cache_kv] * pp), *([lc_pool] * pp), new_pad, lc_new)


def _nsa_decode_body(pt_ref, qn_ref, qr_ref, cmp_ref, *refs, n_steps, pp, ts, past, nbp, wb):
    kv_refs = refs[:pp]
    (new_ref, win_ref, wnew_ref, g_ref, o_ref, nwin_ref,
     qrs_ref, oc_ref, ow_ref, sel_ref, m_ref, l_ref, acc_ref) = refs[pp:]
    p = pl.program_id(1)
    rr = NSA_GROUP
    rows = rr * ts
    gg = NSA_KV_HEADS
    tile_rows = lambda a, n: jnp.concatenate([a] * n, axis=0)
    qi_col = _iota((ts, 1), 0)

    @pl.when(p == 0)
    def _():
        _init_state((m_ref, l_ref, acc_ref))
        pos_q = past + qi_col
        blk = _iota((ts, nbp), 1)
        valid = (blk + 1) * NSA_BLOCK - 1 <= pos_q
        exists = blk * NSA_BLOCK < past + ts
        keep_c = tile_rows(jnp.where(valid, 1.0, 0.0), rr)
        cur = lax.shift_right_logical(pos_q, BLOCK_SHIFT)
        dpos = wb + qi_col - _iota((ts, wb), 1)
        keep_buf = jnp.where(dpos >= 0, jnp.where(dpos < NSA_WINDOW, 1.0, 0.0), 0.0)
        keep_new = jnp.where(_iota((ts, PAGE_SIZE), 1) <= qi_col, 1.0, 0.0)
        keep_w = tile_rows(jnp.concatenate([keep_buf, keep_new], axis=1), rr)
        for g in range(gg):
            qn = _stack_heads(qn_ref, g * rr * HEAD_DIM, rr).astype(BF16)
            qr = _stack_heads(qr_ref, g * rr * HEAD_DIM, rr).astype(BF16)
            qrs_ref[g] = qr
            kc = cmp_ref[:, g * HEAD_DIM:(g + 1) * HEAD_DIM].astype(BF16)
            vc = cmp_ref[:, NSA_KV_W + g * HEAD_DIM:NSA_KV_W + (g + 1) * HEAD_DIM].astype(BF16)
            p_c = _masked_softmax(_dot_nt(qn, kc) * SCALE2, keep_c, axis=-1)
            oc_ref[g * rows:(g + 1) * rows, :] = _dot(p_c.astype(BF16), vc)
            imp = p_c[0:ts]
            for r in range(1, rr):
                imp = imp + p_c[r * ts:(r + 1) * ts]
            score = jnp.where(blk == cur, SEL_FORCE, jnp.where(valid, imp, -1.0))
            score = jnp.where(exists, score, NEG_INF)
            sel = _select_blocks(score, min(NSA_TOP_N, (past + ts + NSA_BLOCK - 1) // NSA_BLOCK), axis=-1)
            sel_ref[g * rows:(g + 1) * rows, :] = tile_rows(sel, rr).astype(BF16)
            kw = _slab(win_ref, g, WIN_SLABS, wb).astype(BF16)
            vw = _slab(win_ref, gg + g, WIN_SLABS, wb).astype(BF16)
            kwn = _slab(wnew_ref, g, WIN_SLABS, PAGE_SIZE).astype(BF16)
            vwn = _slab(wnew_ref, gg + g, WIN_SLABS, PAGE_SIZE).astype(BF16)
            s_w = jnp.concatenate([_dot_nt(qr, kw), _dot_nt(qr, kwn)], axis=1) * SCALE2
            p_w = _masked_softmax(s_w, keep_w, axis=-1).astype(BF16)
            ow_ref[g * rows:(g + 1) * rows, :] = _dot(p_w[:, 0:wb], vw) + _dot(p_w[:, wb:wb + PAGE_SIZE], vwn)
        nwin_ref[0:(wb - ts) * WIN_SLABS, :] = win_ref[ts * WIN_SLABS:wb * WIN_SLABS, :]
        nwin_ref[(wb - ts) * WIN_SLABS:wb * WIN_SLABS, :] = wnew_ref[0:ts * WIN_SLABS, :]

    def attend(pages, key_start, keep_tok):
        n_keys = len(pages) * PAGE_SIZE
        keep = _expand_blocks(sel_ref[...], key_start, n_keys)
        if keep_tok is not None:
            keep = keep * keep_tok
        s = jnp.concatenate(
            [jnp.concatenate([_dot_nt(qrs_ref[g], _slab(pg, 2 * gg + g, PAGE_SLABS, PAGE_SIZE).astype(BF16))
                              for pg in pages], axis=1)
             for g in range(gg)], axis=0) * SCALE2
        s = jnp.where(keep > 0.5, s, NEG_INF)

        def pv(pb):
            out = []
            for g in range(gg):
                acc = None
                for u, pg in enumerate(pages):
                    part = _dot(pb[g * rows:(g + 1) * rows, u * PAGE_SIZE:(u + 1) * PAGE_SIZE],
                                _slab(pg, 3 * gg + g, PAGE_SLABS, PAGE_SIZE).astype(BF16))
                    acc = part if acc is None else acc + part
                out.append(acc)
            return jnp.concatenate(out, axis=0)

        _online_update(s, pv, m_ref, l_ref, acc_ref, guard=True)

    attend(kv_refs, p * (pp * PAGE_SIZE), None)

    @pl.when(p == n_steps - 1)
    def _():
        keep_tok = tile_rows(jnp.where(_iota((ts, PAGE_SIZE), 1) <= qi_col, 1.0, 0.0), gg * rr)
        attend([new_ref], past, keep_tok)
        o_s = _normalize(acc_ref[...], l_ref[...])
        gates = g_ref[...]

        def gate(br):
            return jnp.concatenate([gates[:, g * LANES + br * rr + r:g * LANES + br * rr + r + 1]
                                    for g in range(gg) for r in range(rr)], axis=0)

        o = gate(0) * oc_ref[...] + gate(1) * o_s + gate(2) * ow_ref[...]
        for h in range(NSA_HEADS):
            o_ref[:, h * HEAD_DIM:(h + 1) * HEAD_DIM] = o[h * ts:(h + 1) * ts, :]


def _nsa_decode(pt_flat, qn_s, qr_s, cmp_s, cache_nsa, new_pad, win_buf, wnew_pad, gates_s,
                db, ts, n_pages, pp):
    rows = NSA_HEADS * ts
    past = n_pages * PAGE_SIZE
    n_steps = n_pages // pp
    nbp = cmp_s.shape[1]
    wb = win_buf.shape[1] // WIN_SLABS
    page = (PAGE_SIZE * PAGE_SLABS, HEAD_DIM)
    whole = lambda shape: pl.BlockSpec((None,) + shape, lambda b, p, pt: (b,) + (0,) * len(shape))
    grid_spec = pltpu.PrefetchScalarGridSpec(
        num_scalar_prefetch=1,
        grid=(db, n_steps),
        in_specs=[whole((ts, NSA_W)), whole((ts, NSA_W)), whole((nbp, 2 * NSA_KV_W))]
        + _paged_specs(page, n_pages, pp)
        + [whole(page), whole((wb * WIN_SLABS, HEAD_DIM)), whole((PAGE_SIZE * WIN_SLABS, HEAD_DIM)),
           whole((ts, NSA_KV_HEADS * LANES))],
        out_specs=[whole((ts, NSA_W)), whole((wb * WIN_SLABS, HEAD_DIM))],
        scratch_shapes=[pltpu.VMEM((NSA_KV_HEADS, NSA_GROUP * ts, HEAD_DIM), BF16),
                        pltpu.VMEM((rows, HEAD_DIM), F32),
                        pltpu.VMEM((rows, HEAD_DIM), F32),
                        pltpu.VMEM((rows, nbp), BF16),
                        pltpu.VMEM((rows, 1), F32), pltpu.VMEM((rows, 1), F32),
                        pltpu.VMEM((rows, HEAD_DIM), F32)],
    )
    return pl.pallas_call(
        functools.partial(_nsa_decode_body, n_steps=n_steps, pp=pp, ts=ts, past=past, nbp=nbp, wb=wb),
        grid_spec=grid_spec,
        out_shape=[jax.ShapeDtypeStruct((db, ts, NSA_W), F32),
                   jax.ShapeDtypeStruct((db, wb * WIN_SLABS, HEAD_DIM), F32)],
        compiler_params=_params("parallel", "arbitrary"),
        name="nsa_decode",
    )(pt_flat, qn_s, qr_s, cmp_s, *([cache_nsa] * pp), new_pad, win_buf, wnew_pad, gates_s)


def _topk_rows(x, k):
    n = x.shape[0]
    row = _iota(x.shape, 0).astype(F32)
    vals, idxs = [], []
    for _ in range(k):
        mx = jnp.max(x, axis=0, keepdims=True)
        ix = jnp.min(jnp.where(x == mx, row, float(n)), axis=0, keepdims=True)
        x = jnp.where(row == ix, NEG_INF, x)
        vals.append(mx)
        idxs.append(ix)
    return jnp.concatenate(vals, axis=0), jnp.concatenate(idxs, axis=0)


def _peer_select_body(q_ref, keys_ref, a_ref, b_ref, g_ref):
    tm = q_ref.shape[0]
    kk = PEER_TOPK
    a_all, b_all, g_all = [], [], []
    for h in range(PEER_HEADS):
        sub = []
        for c in range(2):
            keys = keys_ref[h, c].astype(BF16)
            col0 = (h * 2 + c) * PEER_HALF
            sub.append(_topk_rows(_dot_nt(keys, q_ref[:, col0:col0 + PEER_HALF]), kk))
        (s1, i1), (s2, i2) = sub
        width = [kk // (i + 1) for i in range(kk)]
        n_cand = sum(width)
        pad = -(-n_cand // 8) * 8 - n_cand
        fill = lambda v: [jnp.full((pad, tm), v, F32)] if pad else []
        cand = jnp.concatenate([s1[i:i + 1] + s2[0:width[i]] for i in range(kk)] + fill(NEG_INF), axis=0)
        ca = jnp.concatenate([jnp.broadcast_to(i1[i:i + 1], (width[i], tm)) for i in range(kk)] + fill(0.0), axis=0)
        cb = jnp.concatenate([i2[0:width[i]] for i in range(kk)] + fill(0.0), axis=0)
        row = _iota(cand.shape, 0).astype(F32)
        tops, a_h, b_h = [], [], []
        for _ in range(kk):
            mx = jnp.max(cand, axis=0, keepdims=True)
            ix = jnp.min(jnp.where(cand == mx, row, float(cand.shape[0])), axis=0, keepdims=True)
            hit = row == ix
            a_h.append(jnp.sum(jnp.where(hit, ca, 0.0), axis=0, keepdims=True))
            b_h.append(jnp.sum(jnp.where(hit, cb, 0.0), axis=0, keepdims=True))
            cand = jnp.where(hit, NEG_INF, cand)
            tops.append(mx)
        top = jnp.concatenate(tops, axis=0)
        e = jnp.exp(top - top[0:1])
        g_all.append(e / jnp.sum(e, axis=0, keepdims=True))
        a_all.append(jnp.concatenate(a_h, axis=0))
        b_all.append(jnp.concatenate(b_h, axis=0))
    a_ref[...] = jnp.concatenate(a_all, axis=0).T
    b_ref[...] = jnp.concatenate(b_all, axis=0).T
    g_ref[...] = jnp.concatenate(g_all, axis=0).T


def _peer_select(q, peer_keys):
    n = q.shape[0]
    tm = _tile(n, LANES, LANES)
    slots = PEER_HEADS * PEER_TOPK
    out = jax.ShapeDtypeStruct((n, slots), F32)
    spec = pl.BlockSpec((tm, slots), lambda i: (i, 0))
    return pl.pallas_call(
        _peer_select_body,
        grid=(n // tm,),
        in_specs=[pl.BlockSpec((tm, q.shape[1]), lambda i: (i, 0)),
                  pl.BlockSpec(peer_keys.shape, lambda i: (0, 0, 0, 0))],
        out_specs=[spec, spec, spec],
        out_shape=[out, out, out],
        compiler_params=_params("parallel"),
        name="peer_select",
    )(q, peer_keys)


def _peer_coef_body(a_ref, b_ref, g_ref, act_ref, o_ref, w_ref):
    tt = a_ref.shape[0]
    slots = a_ref.shape[1]
    sub = _iota((PEER_KEYS, slots), 0).astype(F32).astype(BF16)
    one, zero = jnp.ones((), BF16), jnp.zeros((), BF16)
    for t in range(tt):
        g_row = g_ref[t:t + 1, :]
        g_hi = g_row.astype(BF16)
        g_lo = (g_row - g_hi.astype(F32)).astype(BF16)
        a_hot = jnp.where(sub == a_ref[t:t + 1, :].astype(BF16), one, zero)
        hit = sub == b_ref[t:t + 1, :].astype(BF16)
        b_hi = jnp.where(hit, g_hi, zero)
        b_lo = jnp.where(hit, g_lo, zero)
        w_ref[:, t, :] = _dot_nt(jnp.concatenate([a_hot, a_hot], axis=1),
                                 jnp.concatenate([b_hi, b_lo], axis=1))
    for a in range(PEER_KEYS):
        o_ref[:, a * PEER_KEYS:(a + 1) * PEER_KEYS] = (w_ref[a] * act_ref[a]).astype(o_ref.dtype)


def _peer_coef(a_sel, b_sel, g_sel, act):
    n, slots = a_sel.shape
    tt = _tile(n, 16)
    spec = pl.BlockSpec((tt, slots), lambda i: (i, 0))
    cube = pl.BlockSpec((PEER_KEYS, tt, PEER_KEYS), lambda i: (0, i, 0))
    return pl.pallas_call(
        _peer_coef_body,
        grid=(n // tt,),
        in_specs=[spec, spec, spec, cube],
        out_specs=pl.BlockSpec((tt, PEER_KEYS * PEER_KEYS), lambda i: (i, 0)),
        out_shape=jax.ShapeDtypeStruct((n, PEER_KEYS * PEER_KEYS), BF16),
        scratch_shapes=[pltpu.VMEM((PEER_KEYS, tt, PEER_KEYS), F32)],
        compiler_params=_params("parallel"),
        name="peer_coef",
    )(a_sel, b_sel, g_sel, act)


def _rope_tables(pos):
    half = HEAD_DIM // 2
    inv = ROPE_THETA ** (-jnp.arange(half, dtype=F32) / half)
    ang = pos.astype(F32)[:, None] * inv[None, :]
    cos, sin = jnp.cos(ang), jnp.sin(ang)
    return jnp.concatenate([cos, cos], axis=-1), jnp.concatenate([-sin, sin], axis=-1)


def _align_w_in(w_in):
    d = w_in.shape[0]
    o_f = FOX_W + 2 * FOX_KV_W
    o_q = o_f + FOX_HEADS
    o_g = o_q + NSA_W + 6 * NSA_KV_W
    main = jnp.concatenate([w_in[:, :o_f], w_in[:, o_q:o_g]], axis=1)
    forget = jnp.pad(w_in[:, o_f:o_q], ((0, 0), (0, LANES - FOX_HEADS)))
    gates = w_in[:, o_g:].reshape(d, NSA_KV_HEADS, NSA_GROUP, NSA_BRANCHES)
    gates = jnp.transpose(gates, (0, 1, 3, 2)).reshape(d, NSA_KV_HEADS, NSA_BRANCHES * NSA_GROUP)
    gates = jnp.pad(gates, ((0, 0), (0, 0), (0, LANES - NSA_BRANCHES * NSA_GROUP)))
    cols = main.shape[1] + (1 + NSA_KV_HEADS) * LANES
    fill = jnp.zeros((d, -cols % IN_PROJ_TN), w_in.dtype)
    return jnp.concatenate([main, forget, gates.reshape(d, NSA_KV_HEADS * LANES), fill], axis=1)


def _pad_axis1(a, size):
    return jnp.pad(a, ((0, 0), (0, size - a.shape[1])) + ((0, 0),) * (a.ndim - 2))


def _layer(x, p_ple, cache_fox_kv, cache_fox_logf, cache_nsa_kv, state_win, pt_flat, dims,
           g_mix, w_in, b_f, w_ck, w_cv, w_a, w_b, w_merge, b_merge, w_o,
           g_ffn, w_pq, peer_keys, peer_u, peer_v, g_ple, w_ple_gate, w_ple):
    b, t, db, ts, n_pages = dims
    n, d = x.shape
    n_p = b * t
    past = n_pages * PAGE_SIZE
    pp = _tile(n_pages, PAGES_PER_STEP, 1)
    pp_attn = _tile(n_pages, ATTN_PAGES_PER_STEP, 1)

    h = _rmsnorm(x, g_mix, BF16)
    proj = _matmul(h, _align_w_in(w_in), out_dtype=F32, name="in_proj", **FULL_K)
    pos = jnp.concatenate([jnp.tile(jnp.arange(t), b), jnp.tile(past + jnp.arange(ts), db)])
    cos, sin = _rope_tables(pos)
    bf_row = jnp.pad(b_f, (0, LANES - FOX_HEADS)).reshape(1, LANES)
    qf, fkv, qn, qr, nkv, win, logf, gates = _post_projection(proj, cos, sin, bf_row, 0, n_p)
    sample_rows = _post_projection(proj, cos, sin, bf_row, n_p, n - n_p)
    qf_s, fkv_s, qn_s, qr_s, nkv_s, win_s, logf_s, gates_s = [a.reshape(db, ts, -1) for a in sample_rows]

    lf_p_t = jnp.transpose(logf.reshape(b, t, FOX_HEADS), (0, 2, 1))
    o_f_p = _fox_prompt(qf, fkv, _cumsum_prompt(lf_p_t), b, t)
    w_cmp = jnp.concatenate([jnp.tile(w_ck, (1, NSA_KV_HEADS)), jnp.tile(w_cv, (1, NSA_KV_HEADS))], axis=1)
    cmp_p = _compress_rows(nkv, w_cmp, n_p)
    o_n_p = _nsa_prompt(qn, qr, cmp_p, nkv, win, gates, b, t)

    n_pool = cache_fox_logf.shape[0]
    lc_pool = _page_cumsum(jnp.transpose(cache_fox_logf, (0, 2, 1)).reshape(n_pool * FOX_HEADS, PAGE_SIZE))
    lf_new_t = jnp.pad(jnp.transpose(logf_s, (0, 2, 1)), ((0, 0), (0, 0), (0, PAGE_SIZE - ts)))
    lc_new = _page_cumsum(lf_new_t.reshape(db * FOX_HEADS, PAGE_SIZE))
    slab_rows = lambda a: a.reshape(a.shape[0], -1, HEAD_DIM)
    fkv_new = slab_rows(_pad_axis1(fkv_s, PAGE_SIZE))
    o_f_s = _fox_decode(pt_flat, qf_s.astype(F32), slab_rows(cache_fox_kv),
                        lc_pool.reshape(n_pool, FOX_HEADS, PAGE_SIZE), fkv_new,
                        lc_new.reshape(db, FOX_HEADS, PAGE_SIZE), db, ts, n_pages, pp_attn)

    cache_nsa = slab_rows(cache_nsa_kv)
    nkv_new = _pad_axis1(nkv_s, PAGE_SIZE)
    cmp_new = _compress_rows(nkv_new.reshape(db * PAGE_SIZE, 4 * NSA_KV_W), w_cmp, db * PAGE_SIZE)
    cmp_s = jnp.concatenate([_compress_paged(pt_flat, cache_nsa, w_cmp, db, n_pages, pp),
                             cmp_new.reshape(db, PAGE_SIZE // NSA_BLOCK, 2 * NSA_KV_W)], axis=1)
    cmp_s = _pad_axis1(cmp_s, -(-cmp_s.shape[1] // LANES) * LANES)
    wb = state_win.shape[1]
    o_n_s, new_win = _nsa_decode(
        pt_flat, qn_s.astype(F32), qr_s.astype(F32), cmp_s, cache_nsa, slab_rows(nkv_new),
        slab_rows(state_win), slab_rows(_pad_axis1(win_s, PAGE_SIZE)),
        gates_s, db, ts, n_pages, pp_attn)

    o_f = jnp.concatenate([o_f_p, o_f_s.reshape(db * ts, FOX_W).astype(BF16)], axis=0)
    o_n = jnp.concatenate([o_n_p, o_n_s.reshape(db * ts, NSA_W).astype(BF16)], axis=0)

    gate = _matmul(h, w_merge, out_dtype=F32, name="merge_gate", **FULL_K,
                   epilogue=lambda acc, bias: jax.nn.sigmoid(acc + bias), extra=[(b_merge.reshape(1, -1), "row", 0)])
    m_a = _matmul(o_f, w_a, tm=1664, tn=512, tk=2048, out_dtype=F32, name="merge_fox",
                  epilogue=lambda acc, ga: ga * acc, extra=[(gate, "tile", 0)])
    tn_b = _tile(d, 512, LANES)
    mixed = _matmul(o_n, w_b, tm=1664, tn=512, tk=2048, out_dtype=BF16, name="merge_nsa",
                    epilogue=lambda acc, gb, ma: ma + gb * acc,
                    extra=[(gate, "tile", d // tn_b), (m_a, "tile", 0)])
    x1 = _matmul(mixed, w_o, out_dtype=F32, name="out_proj", **FULL_K,
                 epilogue=lambda acc, res: res + acc, extra=[(x, "tile", 0)])

    h2 = _rmsnorm(x1, g_ffn, BF16)
    pq = _matmul(h2, w_pq, out_dtype=BF16, name="peer_query", **FULL_K)
    a_sel, b_sel, g_sel = _peer_select(pq, peer_keys)
    act = _matmul(h2, peer_u, out_dtype=F32, nt=True, split_out=True, epilogue=_gelu_tanh, name="peer_act",
                  **FULL_K)
    coef = _peer_coef(a_sel, b_sel, g_sel, act)
    x2 = _matmul(coef, peer_v, tm=1664, tn=512, tk=2048, out_dtype=F32, name="peer_out",
                 epilogue=lambda acc, res: res + acc, extra=[(x1, "tile", 0)])

    h3 = _rmsnorm(x2, g_ple, BF16)
    x3 = _matmul(h3, w_ple_gate, out_dtype=F32, name="ple_gate", **FULL_K,
                 epilogue=lambda acc, res, p_rows, w_cols: (
                     res + jax.nn.sigmoid(acc) * _dot(p_rows, w_cols.astype(BF16))),
                 extra=[(x2, "tile", 0), (p_ple.astype(BF16), "lhs", 0), (w_ple, "rhs", 0)])

    caches = dict(
        fox_kv_p=fkv.reshape(b, t, 2, FOX_KV_HEADS, HEAD_DIM),
        fox_lf_p=logf.reshape(b, t, FOX_HEADS),
        nsa_kv_p=nkv.reshape(b, t, 4, NSA_KV_HEADS, HEAD_DIM),
        nsa_win_p=win.reshape(b, t, 2, NSA_KV_HEADS, HEAD_DIM)[:, t - min(NSA_WINDOW, t):],
        fox_kv_s=fkv_s.reshape(db, ts, 2, FOX_KV_HEADS, HEAD_DIM),
        fox_lf_s=logf_s,
        nsa_kv_s=nkv_s.reshape(db, ts, 4, NSA_KV_HEADS, HEAD_DIM),
        nsa_win_s=new_win.reshape(db, wb, 2, NSA_KV_HEADS, HEAD_DIM),
    )
    return x3, caches


def kernel(x_prompt, x_sample, cache_fox_kv, cache_fox_logf, cache_nsa_kv, state_nsa_win, page_table, p_prompt, p_sample, g_mix, w_in, b_f, w_ck, w_cv, w_a, w_b, w_merge, b_merge, w_o, g_ffn, w_pq, peer_keys, peer_u, peer_v, g_ple, w_ple_gate, w_ple, g_final):
    b, t, d = x_prompt.shape
    db, ts, _ = x_sample.shape
    depth = g_mix.shape[0]
    n_pages = page_table.shape[1]
    n_p = b * t
    dims = (b, t, db, ts, n_pages)
    pt_flat = page_table.reshape(-1).astype(jnp.int32)
    x = jnp.concatenate([x_prompt.reshape(n_p, d), x_sample.reshape(db * ts, d)], axis=0)
    per_layer = []
    for i in range(depth):
        p_ple = jnp.concatenate([p_prompt[i].reshape(n_p, -1), p_sample[i].reshape(db * ts, -1)], axis=0)
        x, caches = _layer(x, p_ple, cache_fox_kv[i], cache_fox_logf[i], cache_nsa_kv[i], state_nsa_win[i],
                           pt_flat, dims, g_mix[i], w_in[i], b_f[i], w_ck[i], w_cv[i], w_a[i], w_b[i],
                           w_merge[i], b_merge[i], w_o[i], g_ffn[i], w_pq[i], peer_keys[i], peer_u[i],
                           peer_v[i], g_ple[i], w_ple_gate[i], w_ple[i])
        per_layer.append(caches)
    y_p = _rmsnorm(x, g_final, F32, 0, n_p)
    y_s = _rmsnorm(x, g_final, F32, n_p, db * ts)
    stack = lambda name: jnp.stack([c[name] for c in per_layer])
    return (y_p.reshape(b, t, d), y_s.reshape(db, ts, d),
            stack("fox_kv_p"), stack("fox_lf_p"), stack("nsa_kv_p"), stack("nsa_win_p"),
            stack("fox_kv_s"), stack("fox_lf_s"), stack("nsa_kv_s"), stack("nsa_win_s"))
```

```python
import functools

import jax
import jax.numpy as jnp
from jax import lax
from jax.experimental import pallas as pl
from jax.experimental.pallas import tpu as pltpu

F32 = jnp.float32
BF16 = jnp.bfloat16
NEG_INF = float("-inf")

HEAD_DIM = 128
FOX_HEADS = 16
FOX_KV_HEADS = 4
FOX_GROUP = FOX_HEADS // FOX_KV_HEADS
NSA_HEADS = 16
NSA_KV_HEADS = 2
NSA_GROUP = NSA_HEADS // NSA_KV_HEADS
NSA_BLOCK = 64
BLOCK_SHIFT = NSA_BLOCK.bit_length() - 1
NSA_TOP_N = 16
NSA_WINDOW = 512
NSA_BRANCHES = 3
SEL_FORCE = 1e4
ROPE_THETA = 10000.0
PAGE_SIZE = 128
PEER_HEADS = 8
PEER_KEYS = 128
PEER_TOPK = 16
PEER_HALF = 128
RMS_EPS = 1e-6
SCALE = HEAD_DIM ** -0.5
LOG2E = 1.4426950408889634
SCALE2 = SCALE * LOG2E

FOX_W = FOX_HEADS * HEAD_DIM
FOX_KV_W = FOX_KV_HEADS * HEAD_DIM
NSA_W = NSA_HEADS * HEAD_DIM
NSA_KV_W = NSA_KV_HEADS * HEAD_DIM
LANES = 128
VMEM_LIMIT = 56 * 1024 * 1024
PAGES_PER_STEP = 8
ATTN_PAGES_PER_STEP = 32
PAGE_SLABS = 8
WIN_SLABS = 2 * NSA_KV_HEADS

NT_DIMS = (((1,), (1,)), ((), ()))
FULL_K = dict(tm=832, tn=512, tk=4096)
IN_PROJ_TN = FULL_K["tn"]


def _params(*sem):
    return pltpu.CompilerParams(dimension_semantics=sem, vmem_limit_bytes=VMEM_LIMIT)


def _tile(n, target, align=8):
    if n <= target:
        return n
    best = None
    for t in range(align, target + 1, align):
        if n % t == 0:
            best = t
    assert best is not None, (n, target, align)
    return best


def _dot(a, b):
    return jnp.dot(a, b, preferred_element_type=F32)


def _dot_nt(a, b):
    return lax.dot_general(a, b, NT_DIMS, preferred_element_type=F32)


def _dot_tn(a, b):
    k = a.shape[0]
    kp = -(-k // LANES) * LANES
    if kp != k:
        a = jnp.concatenate([a, jnp.zeros((kp - k, a.shape[1]), a.dtype)], axis=0)
        b = jnp.concatenate([b, jnp.zeros((kp - k, b.shape[1]), b.dtype)], axis=0)
    return _dot(a.T.astype(BF16), b.astype(BF16))


def _iota(shape, dim, dtype=jnp.int32):
    return lax.broadcasted_iota(dtype, shape, dim)


def _rmsnorm_body(x_ref, g_ref, o_ref):
    x = x_ref[...]
    y = x * lax.rsqrt(jnp.mean(x * x, axis=-1, keepdims=True) + RMS_EPS)
    o_ref[...] = (y * g_ref[...]).astype(o_ref.dtype)


def _rmsnorm(x, g, out_dtype, row0=0, n=None):
    d = x.shape[1]
    n = x.shape[0] if n is None else n
    tr = _tile(n, 256)
    while row0 % tr:
        tr //= 2
    first = row0 // tr
    return pl.pallas_call(
        _rmsnorm_body,
        grid=(n // tr,),
        in_specs=[pl.BlockSpec((tr, d), lambda i: (first + i, 0)), pl.BlockSpec((1, d), lambda i: (0, 0))],
        out_specs=pl.BlockSpec((tr, d), lambda i: (i, 0)),
        out_shape=jax.ShapeDtypeStruct((n, d), out_dtype),
        compiler_params=_params("parallel"),
        name="rmsnorm",
    )(x, g.reshape(1, d))


def _gelu_tanh(x):
    return 0.5 * x * (1.0 + jnp.tanh(0.7978845608028654 * (x + 0.044715 * (x * x * x))))


def _mm_body(*refs, n_extra, epilogue, nt, nk, split_out):
    x_ref, w_ref = refs[0], refs[1]
    extra = refs[2:2 + n_extra]
    o_ref = refs[2 + n_extra]
    def product():
        w = w_ref[...].astype(BF16)
        return _dot_nt(x_ref[...], w) if nt else _dot(x_ref[...], w)

    def finish(acc):
        res = epilogue(acc, *[e[...] for e in extra]).astype(o_ref.dtype)
        if split_out:
            for j in range(o_ref.shape[0]):
                o_ref[j] = res[:, j * LANES:(j + 1) * LANES]
        else:
            o_ref[...] = res

    if nk == 1:
        finish(product())
        return
    acc_ref = refs[3 + n_extra]
    k = pl.program_id(2)

    @pl.when(k == 0)
    def _():
        acc_ref[...] = jnp.zeros_like(acc_ref)

    acc_ref[...] += product()

    @pl.when(k == nk - 1)
    def _():
        finish(acc_ref[...])


def _matmul(x, w, *, tm, tn, tk, out_dtype, name, epilogue=None, extra=(), nt=False, split_out=False):
    m, kdim = x.shape
    n = w.shape[0] if nt else w.shape[1]
    tm, tn, tk = _tile(m, tm), _tile(n, tn, LANES), _tile(kdim, tk, LANES)
    nk = kdim // tk
    if epilogue is None:
        epilogue = lambda acc: acc
    in_specs = [
        pl.BlockSpec((tm, tk), lambda i, j, k: (i, k)),
        pl.BlockSpec((tn, tk), lambda i, j, k: (j, k)) if nt else pl.BlockSpec((tk, tn), lambda i, j, k: (k, j)),
    ]
    args = [x, w]
    for arr, kind, off in extra:
        if kind == "tile":
            in_specs.append(pl.BlockSpec((tm, tn), lambda i, j, k, off=off: (i, j + off)))
        elif kind == "row":
            in_specs.append(pl.BlockSpec((1, tn), lambda i, j, k, off=off: (0, j + off)))
        elif kind == "lhs":
            in_specs.append(pl.BlockSpec((tm, arr.shape[1]), lambda i, j, k: (i, 0)))
        else:
            assert kind == "rhs", kind
            in_specs.append(pl.BlockSpec((arr.shape[0], tn), lambda i, j, k: (0, j)))
        args.append(arr)
    if split_out:
        out_shape = jax.ShapeDtypeStruct((n // LANES, m, LANES), out_dtype)
        out_spec = pl.BlockSpec((tn // LANES, tm, LANES), lambda i, j, k: (j, i, 0))
    else:
        out_shape = jax.ShapeDtypeStruct((m, n), out_dtype)
        out_spec = pl.BlockSpec((tm, tn), lambda i, j, k: (i, j))
    body = functools.partial(_mm_body, n_extra=len(extra), epilogue=epilogue, nt=nt, nk=nk,
                             split_out=split_out)
    return pl.pallas_call(
        body,
        grid=(m // tm, n // tn, nk),
        in_specs=in_specs,
        out_specs=out_spec,
        out_shape=out_shape,
        scratch_shapes=[pltpu.VMEM((tm, tn), F32)] if nk > 1 else [],
        compiler_params=_params("parallel", "parallel", "arbitrary"),
        name=name,
    )(*args)


def _rope(x, cos, sin_signed):
    return x * cos + pltpu.roll(x, HEAD_DIM // 2, 1) * sin_signed


def _log_sigmoid(z):
    return jnp.minimum(z, 0.0) - jnp.log1p(jnp.exp(-jnp.abs(z)))


def _post_body(p_ref, cos_ref, sin_ref, bf_ref,
               qf_ref, fkv_ref, qn_ref, qr_ref, nkv_ref, win_ref, logf_ref, gate_ref):
    cos = cos_ref[...]
    sin = sin_ref[...]
    o = 0
    qf_ref[...] = p_ref[:, o:o + FOX_W].astype(BF16)
    o += FOX_W
    fkv_ref[...] = p_ref[:, o:o + 2 * FOX_KV_W]
    o += 2 * FOX_KV_W
    for h in range(NSA_HEADS):
        q = p_ref[:, o + h * HEAD_DIM:o + (h + 1) * HEAD_DIM]
        qn_ref[:, h * HEAD_DIM:(h + 1) * HEAD_DIM] = q.astype(BF16)
        qr_ref[:, h * HEAD_DIM:(h + 1) * HEAD_DIM] = _rope(q, cos, sin).astype(BF16)
    o += NSA_W
    nkv_ref[:, 0:2 * NSA_KV_W] = p_ref[:, o:o + 2 * NSA_KV_W]
    o += 2 * NSA_KV_W
    for g in range(NSA_KV_HEADS):
        k = p_ref[:, o + g * HEAD_DIM:o + (g + 1) * HEAD_DIM]
        nkv_ref[:, 2 * NSA_KV_W + g * HEAD_DIM:2 * NSA_KV_W + (g + 1) * HEAD_DIM] = _rope(k, cos, sin)
    o += NSA_KV_W
    nkv_ref[:, 3 * NSA_KV_W:4 * NSA_KV_W] = p_ref[:, o:o + NSA_KV_W]
    o += NSA_KV_W
    for g in range(NSA_KV_HEADS):
        k = p_ref[:, o + g * HEAD_DIM:o + (g + 1) * HEAD_DIM]
        win_ref[:, g * HEAD_DIM:(g + 1) * HEAD_DIM] = _rope(k, cos, sin)
    o += NSA_KV_W
    win_ref[:, NSA_KV_W:2 * NSA_KV_W] = p_ref[:, o:o + NSA_KV_W]
    o += NSA_KV_W
    logf = _log_sigmoid(p_ref[:, o:o + LANES] + bf_ref[...])
    logf_ref[...] = logf[:, 0:FOX_HEADS]
    gate_ref[...] = jax.nn.sigmoid(p_ref[:, o + LANES:o + 3 * LANES])


def _post_projection(p, cos, sin, bf_row, row0, n):
    tr = _tile(n, 256)
    while row0 % tr:
        tr //= 2
    first = row0 // tr
    row = lambda w: pl.BlockSpec((tr, w), lambda i: (i, 0))
    row_in = lambda w: pl.BlockSpec((tr, w), lambda i: (first + i, 0))
    out_shapes = [
        jax.ShapeDtypeStruct((n, FOX_W), BF16),
        jax.ShapeDtypeStruct((n, 2 * FOX_KV_W), F32),
        jax.ShapeDtypeStruct((n, NSA_W), BF16),
        jax.ShapeDtypeStruct((n, NSA_W), BF16),
        jax.ShapeDtypeStruct((n, 4 * NSA_KV_W), F32),
        jax.ShapeDtypeStruct((n, 2 * NSA_KV_W), F32),
        jax.ShapeDtypeStruct((n, FOX_HEADS), F32),
        jax.ShapeDtypeStruct((n, 2 * LANES), F32),
    ]
    return pl.pallas_call(
        _post_body,
        grid=(n // tr,),
        in_specs=[row_in(p.shape[1]), row_in(LANES), row_in(LANES),
                  pl.BlockSpec((1, LANES), lambda i: (0, 0))],
        out_specs=[row(s.shape[1]) for s in out_shapes],
        out_shape=out_shapes,
        compiler_params=_params("parallel"),
        name="post_projection",
    )(p, cos, sin, bf_row)


def _split3(x):
    hi = x.astype(BF16)
    r1 = x - hi.astype(F32)
    mid = r1.astype(BF16)
    lo = (r1 - mid.astype(F32)).astype(BF16)
    return hi, mid, lo


def _row_cumsum(x, u):
    hi, mid, lo = _split3(x)
    return (_dot(hi, u) + _dot(mid, u)) + _dot(lo, u)


def _cumsum_prompt_body(lf_ref, u_ref, o_ref, carry_ref):
    @pl.when(pl.program_id(1) == 0)
    def _():
        carry_ref[...] = jnp.zeros_like(carry_ref)

    c = _row_cumsum(lf_ref[...], u_ref[...]) + carry_ref[...]
    o_ref[...] = c
    r = c.shape[1]
    carry_ref[...] = c[:, r - 1:r]


def _upper_ones(r):
    return (jnp.arange(r)[:, None] <= jnp.arange(r)[None, :]).astype(BF16)


def _cumsum_prompt(lf_t):
    b, h, t = lf_t.shape
    r = _tile(t, 512, LANES)
    return pl.pallas_call(
        _cumsum_prompt_body,
        grid=(b, t // r),
        in_specs=[pl.BlockSpec((None, h, r), lambda i, j: (i, 0, j)),
                  pl.BlockSpec((r, r), lambda i, j: (0, 0))],
        out_specs=pl.BlockSpec((None, h, r), lambda i, j: (i, 0, j)),
        out_shape=jax.ShapeDtypeStruct((b, h, t), F32),
        scratch_shapes=[pltpu.VMEM((h, 1), F32)],
        compiler_params=_params("parallel", "arbitrary"),
        name="cumsum_prompt",
    )(lf_t, _upper_ones(r))


def _page_cumsum_body(x_ref, u_ref, o_ref):
    o_ref[...] = _row_cumsum(x_ref[...], u_ref[...])


def _page_cumsum(x):
    rows = x.shape[0]
    tr = _tile(rows, 2048)
    return pl.pallas_call(
        _page_cumsum_body,
        grid=(rows // tr,),
        in_specs=[pl.BlockSpec((tr, PAGE_SIZE), lambda i: (i, 0)),
                  pl.BlockSpec((PAGE_SIZE, PAGE_SIZE), lambda i: (0, 0))],
        out_specs=pl.BlockSpec((tr, PAGE_SIZE), lambda i: (i, 0)),
        out_shape=jax.ShapeDtypeStruct((rows, PAGE_SIZE), F32),
        compiler_params=_params("parallel"),
        name="page_cumsum",
    )(x, _upper_ones(PAGE_SIZE))


def _online_update_t(s, vt, m_ref, l_ref, acc_ref, guard):
    m_old = m_ref[...]
    m_new = jnp.maximum(m_old, jnp.max(s, axis=0, keepdims=True))
    m_use = jnp.where(m_new == NEG_INF, 0.0, m_new) if guard else m_new
    alpha = jnp.exp2(m_old - m_use)
    p = jnp.exp2(s - m_use)
    l_ref[...] = alpha * l_ref[...] + jnp.sum(p, axis=0, keepdims=True)
    acc_ref[...] = alpha * acc_ref[...] + _dot(vt, p.astype(BF16))
    m_ref[...] = m_new


def _pipelined_chunks(lo, hi, last, qk, consume, s_a, s_b):
    s_a[...] = qk(lo)

    def step(j, src, dst):
        dst[...] = qk(jnp.minimum(j + 1, last))
        consume(j, src[...])

    def pair(t, carry):
        j = lo + 2 * t
        step(j, s_a, s_b)

        @pl.when(j + 1 < hi)
        def _():
            step(j + 1, s_b, s_a)

        return carry

    lax.fori_loop(0, (hi - lo + 1) // 2, pair, 0)


def _online_update(s, pv_fn, m_ref, l_ref, acc_ref, guard):
    m_old = m_ref[...]
    m_new = jnp.maximum(m_old, jnp.max(s, axis=-1, keepdims=True))
    m_use = jnp.where(m_new == NEG_INF, 0.0, m_new) if guard else m_new
    alpha = jnp.exp2(m_old - m_use)
    p = jnp.exp2(s - m_use)
    l_ref[...] = alpha * l_ref[...] + jnp.sum(p, axis=-1, keepdims=True)
    acc_ref[...] = alpha * acc_ref[...] + pv_fn(p.astype(BF16))
    m_ref[...] = m_new


def _masked_softmax(s, keep, axis):
    s = jnp.where(keep > 0.5, s, NEG_INF)
    m = jnp.max(s, axis=axis, keepdims=True)
    m = jnp.where(m == NEG_INF, 0.0, m)
    e = jnp.exp2(s - m)
    return e / jnp.maximum(jnp.sum(e, axis=axis, keepdims=True), 1e-30)


def _normalize(acc, l):
    return acc / jnp.maximum(l, 1e-30)


def _init_state(*triples):
    for m_ref, l_ref, acc_ref in triples:
        m_ref[...] = jnp.full_like(m_ref, NEG_INF)
        l_ref[...] = jnp.zeros_like(l_ref)
        acc_ref[...] = jnp.zeros_like(acc_ref)


def _mask_heads(s, keep, n_heads, width):
    return jnp.concatenate(
        [jnp.where(keep, s[:, r * width:(r + 1) * width], NEG_INF) for r in range(n_heads)], axis=1)


def _fox_prompt_body(q_ref, qx_ref, k_ref, kx_ref, v_ref, o_ref,
                     qa_ref, ka_ref, vt_ref, sa_ref, sb_ref, m_ref, l_ref, acc_ref, *, tq, nq):
    i = pl.program_id(2)
    rr = FOX_GROUP

    @pl.when(i == 0)
    def _():
        def prep(j, carry):
            ks = pl.multiple_of(j * tq, tq)
            ka_ref[j, :, 0:HEAD_DIM] = k_ref[pl.ds(ks, tq), :].astype(BF16)
            ka_ref[j, :, HEAD_DIM:2 * HEAD_DIM] = kx_ref[pl.ds(ks, tq), :]
            vt_ref[j] = v_ref[pl.ds(ks, tq), :].T.astype(BF16)
            return carry

        lax.fori_loop(0, nq, prep, 0)

    for r in range(rr):
        qa_ref[r * tq:(r + 1) * tq, 0:HEAD_DIM] = q_ref[:, r * HEAD_DIM:(r + 1) * HEAD_DIM]
        qa_ref[r * tq:(r + 1) * tq, HEAD_DIM:2 * HEAD_DIM] = qx_ref[r]
    _init_state((m_ref, l_ref, acc_ref))

    qk = lambda j: _dot_nt(ka_ref[j], qa_ref[...]) * SCALE2
    full = lambda j, s: _online_update_t(s, vt_ref[j], m_ref, l_ref, acc_ref, guard=False)
    _pipelined_chunks(0, i, i, qk, full, sa_ref, sb_ref)

    def diagonal(src):
        causal = _iota((tq, tq), 0) <= _iota((tq, tq), 1)
        _online_update_t(_mask_heads(src[...], causal, rr, tq), vt_ref[i], m_ref, l_ref, acc_ref, guard=False)

    pl.when(lax.rem(i, 2) == 0)(lambda: diagonal(sa_ref))
    pl.when(lax.rem(i, 2) == 1)(lambda: diagonal(sb_ref))

    o_t = _normalize(acc_ref[...], l_ref[...])
    for r in range(rr):
        o_ref[:, r * HEAD_DIM:(r + 1) * HEAD_DIM] = o_t[:, r * tq:(r + 1) * tq].T.astype(o_ref.dtype)


def _fox_bias_columns(c_t, b, t):
    g_, r_ = FOX_KV_HEADS, FOX_GROUP
    pieces = jnp.stack(_split3(c_t * (1.0 / SCALE)), axis=-1).reshape(b, g_, r_, t, 3)
    kx = -jnp.transpose(pieces, (0, 1, 3, 2, 4)).reshape(b, g_, t, 3 * r_)
    kx = jnp.concatenate([kx, jnp.ones((b, g_, t, 3), BF16)], axis=-1)
    kx = jnp.pad(kx, ((0, 0), (0, 0), (0, 0), (0, LANES - kx.shape[-1])))
    own = (jnp.arange(r_)[:, None] == jnp.arange(3 * r_)[None, :] // 3).astype(BF16)
    own = jnp.broadcast_to(own[None, None, :, None, :], (b, g_, r_, t, 3 * r_))
    qx = jnp.concatenate([own, pieces], axis=-1)
    qx = jnp.pad(qx, ((0, 0), (0, 0), (0, 0), (0, 0), (0, LANES - qx.shape[-1])))
    return qx, kx


def _fox_prompt(qf, fkv, c_t, b, t):
    tq = _tile(t, 256, LANES)
    nq = t // tq
    g_, r_ = FOX_KV_HEADS, FOX_GROUP
    rows = r_ * tq
    qx, kx = _fox_bias_columns(c_t, b, t)
    return pl.pallas_call(
        functools.partial(_fox_prompt_body, tq=tq, nq=nq),
        grid=(b, g_, nq),
        in_specs=[
            pl.BlockSpec((tq, r_ * HEAD_DIM), lambda bi, g, i: (bi * nq + i, g)),
            pl.BlockSpec((None, None, r_, tq, LANES), lambda bi, g, i: (bi, g, 0, i, 0)),
            pl.BlockSpec((t, HEAD_DIM), lambda bi, g, i: (bi, g)),
            pl.BlockSpec((None, None, t, LANES), lambda bi, g, i: (bi, g, 0, 0)),
            pl.BlockSpec((t, HEAD_DIM), lambda bi, g, i: (bi, g_ + g)),
        ],
        out_specs=pl.BlockSpec((tq, r_ * HEAD_DIM), lambda bi, g, i: (bi * nq + i, g)),
        out_shape=jax.ShapeDtypeStruct((b * t, FOX_W), BF16),
        scratch_shapes=[pltpu.VMEM((rows, 2 * HEAD_DIM), BF16),
                        pltpu.VMEM((nq, tq, 2 * HEAD_DIM), BF16),
                        pltpu.VMEM((nq, HEAD_DIM, tq), BF16),
                        pltpu.VMEM((tq, rows), F32), pltpu.VMEM((tq, rows), F32),
                        pltpu.VMEM((1, rows), F32), pltpu.VMEM((1, rows), F32),
                        pltpu.VMEM((HEAD_DIM, rows), F32)],
        compiler_params=_params("parallel", "parallel", "arbitrary"),
        name="fox_prompt",
    )(qf, qx, fkv, kx, fkv)


def _compress_body(x_ref, w_ref, o_ref):
    x = x_ref[...]
    nb = x.shape[0] // NSA_BLOCK
    xb = x.reshape(nb, NSA_BLOCK, x.shape[1]) * w_ref[...][None]
    o_ref[...] = jnp.sum(xb, axis=1) * (1.0 / NSA_BLOCK)


def _compress_rows(nkv, w_cmp, n_rows):
    width = 2 * NSA_KV_W
    tr = _tile(n_rows, 512, NSA_BLOCK * 8)
    return pl.pallas_call(
        _compress_body,
        grid=(n_rows // tr,),
        in_specs=[pl.BlockSpec((tr, width), lambda i: (i, 0)),
                  pl.BlockSpec((NSA_BLOCK, width), lambda i: (0, 0))],
        out_specs=pl.BlockSpec((tr // NSA_BLOCK, width), lambda i: (i, 0)),
        out_shape=jax.ShapeDtypeStruct((n_rows // NSA_BLOCK, width), F32),
        compiler_params=_params("parallel"),
        name="compress_rows",
    )(nkv, w_cmp)


def _compress_paged_body(pt_ref, *refs, pp):
    page_refs, w_ref, o_ref = refs[:pp], refs[pp], refs[pp + 1]
    nb = PAGE_SIZE // NSA_BLOCK
    for u in range(pp):
        for j in range(2 * NSA_KV_HEADS):
            x = _slab(page_refs[u], j, PAGE_SLABS, PAGE_SIZE)
            xb = x.reshape(nb, NSA_BLOCK, HEAD_DIM) * w_ref[:, j * HEAD_DIM:(j + 1) * HEAD_DIM][None]
            o_ref[u, :, j * HEAD_DIM:(j + 1) * HEAD_DIM] = jnp.sum(xb, axis=1) * (1.0 / NSA_BLOCK)


def _paged_specs(block, n_pages, pp):
    return [pl.BlockSpec((None,) + block, lambda b, p, pt, u=u: (pt[b * n_pages + p * pp + u], 0, 0))
            for u in range(pp)]


def _slab(ref, j, n_slabs, n_rows):
    return ref[pl.ds(j, n_rows, stride=n_slabs), :]


def _compress_paged(pt_flat, cache_nsa, w_cmp, db, n_pages, pp):
    width = 2 * NSA_KV_W
    nb = PAGE_SIZE // NSA_BLOCK
    grid_spec = pltpu.PrefetchScalarGridSpec(
        num_scalar_prefetch=1,
        grid=(db, n_pages // pp),
        in_specs=_paged_specs((PAGE_SIZE * PAGE_SLABS, HEAD_DIM), n_pages, pp)
        + [pl.BlockSpec((NSA_BLOCK, width), lambda b, p, pt: (0, 0))],
        out_specs=pl.BlockSpec((None, pp, nb, width), lambda b, p, pt: (b, p, 0, 0)),
    )
    out = pl.pallas_call(
        functools.partial(_compress_paged_body, pp=pp),
        grid_spec=grid_spec,
        out_shape=jax.ShapeDtypeStruct((db, n_pages, nb, width), F32),
        compiler_params=_params("parallel", "arbitrary"),
        name="compress_paged",
    )(pt_flat, *([cache_nsa] * pp), w_cmp)
    return out.reshape(db, n_pages * nb, width)


def _select_blocks(score, n_sel, axis):
    axis = axis % score.ndim
    nb = score.shape[axis]
    blk = _iota(score.shape, axis).astype(F32)
    sel = jnp.zeros(score.shape, F32)
    work = score
    for _ in range(n_sel):
        mx = jnp.max(work, axis=axis, keepdims=True)
        ix = jnp.min(jnp.where(work == mx, blk, float(nb)), axis=axis, keepdims=True)
        hit = blk == ix
        sel = jnp.where(hit, jnp.where(mx >= 0.0, 1.0, 0.0), sel)
        work = jnp.where(hit, NEG_INF, work)
    return sel


def _expand_blocks(sel_b, key_start, n_keys):
    nb = sel_b.shape[1]
    key_blk = lax.shift_right_logical(key_start + _iota((nb, n_keys), 1), BLOCK_SHIFT)
    e = jnp.where(_iota((nb, n_keys), 0) == key_blk, 1.0, 0.0).astype(BF16)
    return _dot(sel_b, e)


def _nsa_prompt_body(qn_ref, qr_ref, kc_ref, vc_ref, ks_ref, vs_ref, kw_ref, vw_ref, g_ref, o_ref,
                     qs_ref, ksb_ref, vst_ref, kwb_ref, vwt_ref, sel_ref, oc_ref,
                     m_s, l_s, acc_s, m_w, l_w, acc_w, *, tq, nq, nb):
    i = pl.program_id(2)
    rr = NSA_GROUP
    bpc = tq // NSA_BLOCK

    @pl.when(i == 0)
    def _():
        def prep(j, carry):
            ks = pl.multiple_of(j * tq, tq)
            ksb_ref[j] = ks_ref[pl.ds(ks, tq), :].astype(BF16)
            vst_ref[j] = vs_ref[pl.ds(ks, tq), :].T.astype(BF16)
            kwb_ref[j] = kw_ref[pl.ds(ks, tq), :].astype(BF16)
            vwt_ref[j] = vw_ref[pl.ds(ks, tq), :].T.astype(BF16)
            return carry

        lax.fori_loop(0, nq, prep, 0)

    stack = lambda ref: jnp.concatenate([ref[:, r * HEAD_DIM:(r + 1) * HEAD_DIM] for r in range(rr)], axis=0)
    lanes = lambda a: jnp.concatenate([a] * rr, axis=1)
    qn = stack(qn_ref)
    qs_ref[...] = stack(qr_ref)
    pos_q = i * tq + _iota((1, tq), 1)
    _init_state((m_s, l_s, acc_s), (m_w, l_w, acc_w))

    blk = _iota((nb, tq), 0)
    valid = (blk + 1) * NSA_BLOCK - 1 <= pos_q
    s_c = _dot_nt(kc_ref[...].astype(BF16), qn) * SCALE2
    p_c = _masked_softmax(s_c, lanes(jnp.where(valid, 1.0, 0.0)), axis=0)
    oc_ref[...] = _dot_tn(vc_ref[...], p_c)
    imp = p_c[:, 0:tq]
    for r in range(1, rr):
        imp = imp + p_c[:, r * tq:(r + 1) * tq]
    cur = lax.shift_right_logical(pos_q, BLOCK_SHIFT)
    score = jnp.where(blk == cur, SEL_FORCE, jnp.where(valid, imp, -1.0))
    sel_ref[...] = _select_blocks(score, min(NSA_TOP_N, nb), axis=0)

    def sel_keep(j):
        return jnp.concatenate(
            [jnp.broadcast_to(sel_ref[pl.ds(j * bpc + u, 1), :], (NSA_BLOCK, tq)) for u in range(bpc)], axis=0)

    def sel_chunk(j, carry):
        s = _mask_heads(_dot_nt(ksb_ref[j], qs_ref[...]) * SCALE2, sel_keep(j) > 0.5, rr, tq)
        _online_update_t(s, vst_ref[j], m_s, l_s, acc_s, guard=True)
        return carry

    lax.fori_loop(0, i, sel_chunk, 0)
    causal = _iota((tq, tq), 0) <= _iota((tq, tq), 1)
    keep = jnp.where(causal, sel_keep(i), 0.0) > 0.5
    s = _mask_heads(_dot_nt(ksb_ref[i], qs_ref[...]) * SCALE2, keep, rr, tq)
    _online_update_t(s, vst_ref[i], m_s, l_s, acc_s, guard=True)

    def win_chunk(j, carry):
        dpos = pos_q - (j * tq + _iota((tq, 1), 0))
        keep = jnp.where(dpos >= 0, jnp.where(dpos < NSA_WINDOW, 1.0, 0.0), 0.0) > 0.5
        s = _mask_heads(_dot_nt(kwb_ref[j], qs_ref[...]) * SCALE2, keep, rr, tq)
        _online_update_t(s, vwt_ref[j], m_w, l_w, acc_w, guard=True)
        return carry

    lax.fori_loop(jnp.maximum(i - (NSA_WINDOW + tq - 1) // tq, 0), i + 1, win_chunk, 0)

    g_t = g_ref[...].T
    o_s = _normalize(acc_s[...], l_s[...])
    o_w = _normalize(acc_w[...], l_w[...])
    for r in range(rr):
        cols = slice(r * tq, (r + 1) * tq)
        gate = lambda br: g_t[br * rr + r:br * rr + r + 1, :]
        o = gate(0) * oc_ref[:, cols] + gate(1) * o_s[:, cols] + gate(2) * o_w[:, cols]
        o_ref[:, r * HEAD_DIM:(r + 1) * HEAD_DIM] = o.T.astype(o_ref.dtype)


def _nsa_prompt(qn, qr, cmp_p, nkv, win, gates, b, t):
    tq = _tile(t, 256, LANES)
    nq = t // tq
    nb = t // NSA_BLOCK
    g_, rows = NSA_KV_HEADS, NSA_GROUP * tq
    qspec = pl.BlockSpec((tq, NSA_GROUP * HEAD_DIM), lambda bi, g, i: (bi * nq + i, g))
    col = lambda off: pl.BlockSpec((t, HEAD_DIM), lambda bi, g, i: (bi, off + g))
    state = [pltpu.VMEM((1, rows), F32), pltpu.VMEM((1, rows), F32), pltpu.VMEM((HEAD_DIM, rows), F32)]
    return pl.pallas_call(
        functools.partial(_nsa_prompt_body, tq=tq, nq=nq, nb=nb),
        grid=(b, g_, nq),
        in_specs=[
            qspec, qspec,
            pl.BlockSpec((nb, HEAD_DIM), lambda bi, g, i: (bi, g)),
            pl.BlockSpec((nb, HEAD_DIM), lambda bi, g, i: (bi, g_ + g)),
            col(2 * g_), col(3 * g_),
            col(0), col(g_),
            pl.BlockSpec((tq, LANES), lambda bi, g, i: (bi * nq + i, g)),
        ],
        out_specs=qspec,
        out_shape=jax.ShapeDtypeStruct((b * t, NSA_W), BF16),
        scratch_shapes=[pltpu.VMEM((rows, HEAD_DIM), BF16),
                        pltpu.VMEM((nq, tq, HEAD_DIM), BF16), pltpu.VMEM((nq, HEAD_DIM, tq), BF16),
                        pltpu.VMEM((nq, tq, HEAD_DIM), BF16), pltpu.VMEM((nq, HEAD_DIM, tq), BF16),
                        pltpu.VMEM((nb, tq), F32),
                        pltpu.VMEM((HEAD_DIM, rows), F32)] + state + state,
        compiler_params=_params("parallel", "parallel", "arbitrary"),
        name="nsa_prompt",
    )(qn, qr, cmp_p, cmp_p, nkv, nkv, win, win, gates)


def _stack_heads(q_ref, col0, n_heads):
    return jnp.concatenate([q_ref[:, col0 + r * HEAD_DIM:col0 + (r + 1) * HEAD_DIM] for r in range(n_heads)], axis=0)


def _repeat_rows(a, times):
    return jnp.concatenate([jnp.broadcast_to(a[h:h + 1, :], (times, a.shape[1])) for h in range(a.shape[0])], axis=0)


def _fox_decode_body(pt_ref, q_ref, *refs, n_steps, pp, ts):
    kv_refs, lc_refs = refs[:pp], refs[pp:2 * pp]
    new_ref, lcn_ref, o_ref, qs_ref, carry_ref, m_ref, l_ref, acc_ref = refs[2 * pp:]
    p = pl.program_id(1)
    rr = FOX_GROUP
    rows = rr * ts

    @pl.when(p == 0)
    def _():
        _init_state((m_ref, l_ref, acc_ref))
        carry_ref[...] = jnp.zeros_like(carry_ref)
        for g in range(FOX_KV_HEADS):
            qs_ref[g] = _stack_heads(q_ref, g * rr * HEAD_DIM, rr).astype(BF16)

    def attend(pages, c_k, keep):
        bias = _repeat_rows(c_k * LOG2E, ts)
        s = jnp.concatenate(
            [jnp.concatenate([_dot_nt(qs_ref[g], _slab(pg, g, PAGE_SLABS, PAGE_SIZE).astype(BF16))
                              for pg in pages], axis=1)
             for g in range(FOX_KV_HEADS)], axis=0) * SCALE2 - bias
        if keep is not None:
            s = jnp.where(keep, s, NEG_INF)

        def pv(pb):
            out = []
            for g in range(FOX_KV_HEADS):
                acc = None
                for u, pg in enumerate(pages):
                    part = _dot(pb[g * rows:(g + 1) * rows, u * PAGE_SIZE:(u + 1) * PAGE_SIZE],
                                _slab(pg, FOX_KV_HEADS + g, PAGE_SLABS, PAGE_SIZE).astype(BF16))
                    acc = part if acc is None else acc + part
                out.append(acc)
            return jnp.concatenate(out, axis=0)

        _online_update(s, pv, m_ref, l_ref, acc_ref, guard=False)

    run = carry_ref[...]
    c_pages = []
    for u in range(pp):
        lc = lc_refs[u][...]
        c_pages.append(run + lc)
        run = run + lc[:, PAGE_SIZE - 1:PAGE_SIZE]
    carry_ref[...] = run
    attend(kv_refs, jnp.concatenate(c_pages, axis=1), None)

    @pl.when(p == n_steps - 1)
    def _():
        keep_i = jnp.where(_iota((ts, PAGE_SIZE), 1) <= _iota((ts, PAGE_SIZE), 0), 1.0, 0.0)
        keep = jnp.concatenate([keep_i] * FOX_HEADS, axis=0) > 0.5
        attend([new_ref], run + lcn_ref[...], keep)
        o = _normalize(acc_ref[...], l_ref[...])
        for h in range(FOX_HEADS):
            o_ref[:, h * HEAD_DIM:(h + 1) * HEAD_DIM] = o[h * ts:(h + 1) * ts, :]


def _fox_decode(pt_flat, q_s, cache_kv, lc_pool, new_pad, lc_new, db, ts, n_pages, pp):
    rows = FOX_HEADS * ts
    n_steps = n_pages // pp
    page = (PAGE_SIZE * PAGE_SLABS, HEAD_DIM)
    grid_spec = pltpu.PrefetchScalarGridSpec(
        num_scalar_prefetch=1,
        grid=(db, n_steps),
        in_specs=[pl.BlockSpec((None, ts, FOX_W), lambda b, p, pt: (b, 0, 0))]
        + _paged_specs(page, n_pages, pp)
        + _paged_specs((FOX_HEADS, PAGE_SIZE), n_pages, pp)
        + [pl.BlockSpec((None,) + page, lambda b, p, pt: (b, 0, 0)),
           pl.BlockSpec((None, FOX_HEADS, PAGE_SIZE), lambda b, p, pt: (b, 0, 0))],
        out_specs=pl.BlockSpec((None, ts, FOX_W), lambda b, p, pt: (b, 0, 0)),
        scratch_shapes=[pltpu.VMEM((FOX_KV_HEADS, FOX_GROUP * ts, HEAD_DIM), BF16),
                        pltpu.VMEM((FOX_HEADS, 1), F32),
                        pltpu.VMEM((rows, 1), F32), pltpu.VMEM((rows, 1), F32),
                        pltpu.VMEM((rows, HEAD_DIM), F32)],
    )
    return pl.pallas_call(
        functools.partial(_fox_decode_body, n_steps=n_steps, pp=pp, ts=ts),
        grid_spec=grid_spec,
        out_shape=jax.ShapeDtypeStruct((db, ts, FOX_W), F32),
        compiler_params=_params("parallel", "arbitrary"),
        name="fox_decode",
    )(pt_flat, q_s, *([cache_kv] * pp), *([lc_pool] * pp), new_pad, lc_new)


def _nsa_decode_body(pt_ref, qn_ref, qr_ref, cmp_ref, *refs, n_steps, pp, ts, past, nbp, wb):
    kv_refs = refs[:pp]
    (new_ref, win_ref, wnew_ref, g_ref, o_ref, nwin_ref,
     qrs_ref, oc_ref, ow_ref, sel_ref, m_ref, l_ref, acc_ref) = refs[pp:]
    p = pl.program_id(1)
    rr = NSA_GROUP
    rows = rr * ts
    gg = NSA_KV_HEADS
    tile_rows = lambda a, n: jnp.concatenate([a] * n, axis=0)
    qi_col = _iota((ts, 1), 0)

    @pl.when(p == 0)
    def _():
        _init_state((m_ref, l_ref, acc_ref))
        pos_q = past + qi_col
        blk = _iota((ts, nbp), 1)
        valid = (blk + 1) * NSA_BLOCK - 1 <= pos_q
        exists = blk * NSA_BLOCK < past + ts
        keep_c = tile_rows(jnp.where(valid, 1.0, 0.0), rr)
        cur = lax.shift_right_logical(pos_q, BLOCK_SHIFT)
        dpos = wb + qi_col - _iota((ts, wb), 1)
        keep_buf = jnp.where(dpos >= 0, jnp.where(dpos < NSA_WINDOW, 1.0, 0.0), 0.0)
        keep_new = jnp.where(_iota((ts, PAGE_SIZE), 1) <= qi_col, 1.0, 0.0)
        keep_w = tile_rows(jnp.concatenate([keep_buf, keep_new], axis=1), rr)
        for g in range(gg):
            qn = _stack_heads(qn_ref, g * rr * HEAD_DIM, rr).astype(BF16)
            qr = _stack_heads(qr_ref, g * rr * HEAD_DIM, rr).astype(BF16)
            qrs_ref[g] = qr
            kc = cmp_ref[:, g * HEAD_DIM:(g + 1) * HEAD_DIM].astype(BF16)
            vc = cmp_ref[:, NSA_KV_W + g * HEAD_DIM:NSA_KV_W + (g + 1) * HEAD_DIM].astype(BF16)
            p_c = _masked_softmax(_dot_nt(qn, kc) * SCALE2, keep_c, axis=-1)
            oc_ref[g * rows:(g + 1) * rows, :] = _dot(p_c.astype(BF16), vc)
            imp = p_c[0:ts]
            for r in range(1, rr):
                imp = imp + p_c[r * ts:(r + 1) * ts]
            score = jnp.where(blk == cur, SEL_FORCE, jnp.where(valid, imp, -1.0))
            score = jnp.where(exists, score, NEG_INF)
            sel = _select_blocks(score, min(NSA_TOP_N, (past + ts + NSA_BLOCK - 1) // NSA_BLOCK), axis=-1)
            sel_ref[g * rows:(g + 1) * rows, :] = tile_rows(sel, rr).astype(BF16)
            kw = _slab(win_ref, g, WIN_SLABS, wb).astype(BF16)
            vw = _slab(win_ref, gg + g, WIN_SLABS, wb).astype(BF16)
            kwn = _slab(wnew_ref, g, WIN_SLABS, PAGE_SIZE).astype(BF16)
            vwn = _slab(wnew_ref, gg + g, WIN_SLABS, PAGE_SIZE).astype(BF16)
            s_w = jnp.concatenate([_dot_nt(qr, kw), _dot_nt(qr, kwn)], axis=1) * SCALE2
            p_w = _masked_softmax(s_w, keep_w, axis=-1).astype(BF16)
            ow_ref[g * rows:(g + 1) * rows, :] = _dot(p_w[:, 0:wb], vw) + _dot(p_w[:, wb:wb + PAGE_SIZE], vwn)
        nwin_ref[0:(wb - ts) * WIN_SLABS, :] = win_ref[ts * WIN_SLABS:wb * WIN_SLABS, :]
        nwin_ref[(wb - ts) * WIN_SLABS:wb * WIN_SLABS, :] = wnew_ref[0:ts * WIN_SLABS, :]

    def attend(pages, key_start, keep_tok):
        n_keys = len(pages) * PAGE_SIZE
        keep = _expand_blocks(sel_ref[...], key_start, n_keys)
        if keep_tok is not None:
            keep = keep * keep_tok
        s = jnp.concatenate(
            [jnp.concatenate([_dot_nt(qrs_ref[g], _slab(pg, 2 * gg + g, PAGE_SLABS, PAGE_SIZE).astype(BF16))
                              for pg in pages], axis=1)
             for g in range(gg)], axis=0) * SCALE2
        s = jnp.where(keep > 0.5, s, NEG_INF)

        def pv(pb):
            out = []
            for g in range(gg):
                acc = None
                for u, pg in enumerate(pages):
                    part = _dot(pb[g * rows:(g + 1) * rows, u * PAGE_SIZE:(u + 1) * PAGE_SIZE],
                                _slab(pg, 3 * gg + g, PAGE_SLABS, PAGE_SIZE).astype(BF16))
                    acc = part if acc is None else acc + part
                out.append(acc)
            return jnp.concatenate(out, axis=0)

        _online_update(s, pv, m_ref, l_ref, acc_ref, guard=True)

    attend(kv_refs, p * (pp * PAGE_SIZE), None)

    @pl.when(p == n_steps - 1)
    def _():
        keep_tok = tile_rows(jnp.where(_iota((ts, PAGE_SIZE), 1) <= qi_col, 1.0, 0.0), gg * rr)
        attend([new_ref], past, keep_tok)
        o_s = _normalize(acc_ref[...], l_ref[...])
        gates = g_ref[...]

        def gate(br):
            return jnp.concatenate([gates[:, g * LANES + br * rr + r:g * LANES + br * rr + r + 1]
                                    for g in range(gg) for r in range(rr)], axis=0)

        o = gate(0) * oc_ref[...] + gate(1) * o_s + gate(2) * ow_ref[...]
        for h in range(NSA_HEADS):
            o_ref[:, h * HEAD_DIM:(h + 1) * HEAD_DIM] = o[h * ts:(h + 1) * ts, :]


def _nsa_decode(pt_flat, qn_s, qr_s, cmp_s, cache_nsa, new_pad, win_buf, wnew_pad, gates_s,
                db, ts, n_pages, pp):
    rows = NSA_HEADS * ts
    past = n_pages * PAGE_SIZE
    n_steps = n_pages // pp
    nbp = cmp_s.shape[1]
    wb = win_buf.shape[1] // WIN_SLABS
    page = (PAGE_SIZE * PAGE_SLABS, HEAD_DIM)
    whole = lambda shape: pl.BlockSpec((None,) + shape, lambda b, p, pt: (b,) + (0,) * len(shape))
    grid_spec = pltpu.PrefetchScalarGridSpec(
        num_scalar_prefetch=1,
        grid=(db, n_steps),
        in_specs=[whole((ts, NSA_W)), whole((ts, NSA_W)), whole((nbp, 2 * NSA_KV_W))]
        + _paged_specs(page, n_pages, pp)
        + [whole(page), whole((wb * WIN_SLABS, HEAD_DIM)), whole((PAGE_SIZE * WIN_SLABS, HEAD_DIM)),
           whole((ts, NSA_KV_HEADS * LANES))],
        out_specs=[whole((ts, NSA_W)), whole((wb * WIN_SLABS, HEAD_DIM))],
        scratch_shapes=[pltpu.VMEM((NSA_KV_HEADS, NSA_GROUP * ts, HEAD_DIM), BF16),
                        pltpu.VMEM((rows, HEAD_DIM), F32),
                        pltpu.VMEM((rows, HEAD_DIM), F32),
                        pltpu.VMEM((rows, nbp), BF16),
                        pltpu.VMEM((rows, 1), F32), pltpu.VMEM((rows, 1), F32),
                        pltpu.VMEM((rows, HEAD_DIM), F32)],
    )
    return pl.pallas_call(
        functools.partial(_nsa_decode_body, n_steps=n_steps, pp=pp, ts=ts, past=past, nbp=nbp, wb=wb),
        grid_spec=grid_spec,
        out_shape=[jax.ShapeDtypeStruct((db, ts, NSA_W), F32),
                   jax.ShapeDtypeStruct((db, wb * WIN_SLABS, HEAD_DIM), F32)],
        compiler_params=_params("parallel", "arbitrary"),
        name="nsa_decode",
    )(pt_flat, qn_s, qr_s, cmp_s, *([cache_nsa] * pp), new_pad, win_buf, wnew_pad, gates_s)


def _topk_rows(x, k):
    n = x.shape[0]
    row = _iota(x.shape, 0).astype(F32)
    vals, idxs = [], []
    for _ in range(k):
        mx = jnp.max(x, axis=0, keepdims=True)
        ix = jnp.min(jnp.where(x == mx, row, float(n)), axis=0, keepdims=True)
        x = jnp.where(row == ix, NEG_INF, x)
        vals.append(mx)
        idxs.append(ix)
    return jnp.concatenate(vals, axis=0), jnp.concatenate(idxs, axis=0)


def _peer_select_body(q_ref, keys_ref, a_ref, b_ref, g_ref):
    tm = q_ref.shape[0]
    kk = PEER_TOPK
    a_all, b_all, g_all = [], [], []
    for h in range(PEER_HEADS):
        sub = []
        for c in range(2):
            keys = keys_ref[h, c].astype(BF16)
            col0 = (h * 2 + c) * PEER_HALF
            sub.append(_topk_rows(_dot_nt(keys, q_ref[:, col0:col0 + PEER_HALF]), kk))
        (s1, i1), (s2, i2) = sub
        width = [kk // (i + 1) for i in range(kk)]
        n_cand = sum(width)
        pad = -(-n_cand // 8) * 8 - n_cand
        fill = lambda v: [jnp.full((pad, tm), v, F32)] if pad else []
        cand = jnp.concatenate([s1[i:i + 1] + s2[0:width[i]] for i in range(kk)] + fill(NEG_INF), axis=0)
        ca = jnp.concatenate([jnp.broadcast_to(i1[i:i + 1], (width[i], tm)) for i in range(kk)] + fill(0.0), axis=0)
        cb = jnp.concatenate([i2[0:width[i]] for i in range(kk)] + fill(0.0), axis=0)
        row = _iota(cand.shape, 0).astype(F32)
        tops, a_h, b_h = [], [], []
        for _ in range(kk):
            mx = jnp.max(cand, axis=0, keepdims=True)
            ix = jnp.min(jnp.where(cand == mx, row, float(cand.shape[0])), axis=0, keepdims=True)
            hit = row == ix
            a_h.append(jnp.sum(jnp.where(hit, ca, 0.0), axis=0, keepdims=True))
            b_h.append(jnp.sum(jnp.where(hit, cb, 0.0), axis=0, keepdims=True))
            cand = jnp.where(hit, NEG_INF, cand)
            tops.append(mx)
        top = jnp.concatenate(tops, axis=0)
        e = jnp.exp(top - top[0:1])
        g_all.append(e / jnp.sum(e, axis=0, keepdims=True))
        a_all.append(jnp.concatenate(a_h, axis=0))
        b_all.append(jnp.concatenate(b_h, axis=0))
    a_ref[...] = jnp.concatenate(a_all, axis=0).T
    b_ref[...] = jnp.concatenate(b_all, axis=0).T
    g_ref[...] = jnp.concatenate(g_all, axis=0).T


def _peer_select(q, peer_keys):
    n = q.shape[0]
    tm = _tile(n, LANES, LANES)
    slots = PEER_HEADS * PEER_TOPK
    out = jax.ShapeDtypeStruct((n, slots), F32)
    spec = pl.BlockSpec((tm, slots), lambda i: (i, 0))
    return pl.pallas_call(
        _peer_select_body,
        grid=(n // tm,),
        in_specs=[pl.BlockSpec((tm, q.shape[1]), lambda i: (i, 0)),
                  pl.BlockSpec(peer_keys.shape, lambda i: (0, 0, 0, 0))],
        out_specs=[spec, spec, spec],
        out_shape=[out, out, out],
        compiler_params=_params("parallel"),
        name="peer_select",
    )(q, peer_keys)


def _peer_coef_body(a_ref, b_ref, g_ref, act_ref, o_ref, w_ref):
    tt = a_ref.shape[0]
    slots = a_ref.shape[1]
    sub = _iota((PEER_KEYS, slots), 0).astype(F32).astype(BF16)
    one, zero = jnp.ones((), BF16), jnp.zeros((), BF16)
    for t in range(tt):
        g_row = g_ref[t:t + 1, :]
        g_hi = g_row.astype(BF16)
        g_lo = (g_row - g_hi.astype(F32)).astype(BF16)
        a_hot = jnp.where(sub == a_ref[t:t + 1, :].astype(BF16), one, zero)
        hit = sub == b_ref[t:t + 1, :].astype(BF16)
        b_hi = jnp.where(hit, g_hi, zero)
        b_lo = jnp.where(hit, g_lo, zero)
        w_ref[:, t, :] = _dot_nt(jnp.concatenate([a_hot, a_hot], axis=1),
                                 jnp.concatenate([b_hi, b_lo], axis=1))
    for a in range(PEER_KEYS):
        o_ref[:, a * PEER_KEYS:(a + 1) * PEER_KEYS] = (w_ref[a] * act_ref[a]).astype(o_ref.dtype)


def _peer_coef(a_sel, b_sel, g_sel, act):
    n, slots = a_sel.shape
    tt = _tile(n, 32, 16)
    spec = pl.BlockSpec((tt, slots), lambda i: (i, 0))
    cube = pl.BlockSpec((PEER_KEYS, tt, PEER_KEYS), lambda i: (0, i, 0))
    return pl.pallas_call(
        _peer_coef_body,
        grid=(n // tt,),
        in_specs=[spec, spec, spec, cube],
        out_specs=pl.BlockSpec((tt, PEER_KEYS * PEER_KEYS), lambda i: (i, 0)),
        out_shape=jax.ShapeDtypeStruct((n, PEER_KEYS * PEER_KEYS), BF16),
        scratch_shapes=[pltpu.VMEM((PEER_KEYS, tt, PEER_KEYS), F32)],
        compiler_params=_params("parallel"),
        name="peer_coef",
    )(a_sel, b_sel, g_sel, act)


def _rope_tables(pos):
    half = HEAD_DIM // 2
    inv = ROPE_THETA ** (-jnp.arange(half, dtype=F32) / half)
    ang = pos.astype(F32)[:, None] * inv[None, :]
    cos, sin = jnp.cos(ang), jnp.sin(ang)
    return jnp.concatenate([cos, cos], axis=-1), jnp.concatenate([-sin, sin], axis=-1)


def _align_w_in(w_in):
    d = w_in.shape[0]
    o_f = FOX_W + 2 * FOX_KV_W
    o_q = o_f + FOX_HEADS
    o_g = o_q + NSA_W + 6 * NSA_KV_W
    main = jnp.concatenate([w_in[:, :o_f], w_in[:, o_q:o_g]], axis=1)
    forget = jnp.pad(w_in[:, o_f:o_q], ((0, 0), (0, LANES - FOX_HEADS)))
    gates = w_in[:, o_g:].reshape(d, NSA_KV_HEADS, NSA_GROUP, NSA_BRANCHES)
    gates = jnp.transpose(gates, (0, 1, 3, 2)).reshape(d, NSA_KV_HEADS, NSA_BRANCHES * NSA_GROUP)
    gates = jnp.pad(gates, ((0, 0), (0, 0), (0, LANES - NSA_BRANCHES * NSA_GROUP)))
    cols = main.shape[1] + (1 + NSA_KV_HEADS) * LANES
    fill = jnp.zeros((d, -cols % IN_PROJ_TN), w_in.dtype)
    return jnp.concatenate([main, forget, gates.reshape(d, NSA_KV_HEADS * LANES), fill], axis=1)


def _pad_axis1(a, size):
    return jnp.pad(a, ((0, 0), (0, size - a.shape[1])) + ((0, 0),) * (a.ndim - 2))


def _layer(x, p_ple, cache_fox_kv, cache_fox_logf, cache_nsa_kv, state_win, pt_flat, dims,
           g_mix, w_in, b_f, w_ck, w_cv, w_a, w_b, w_merge, b_merge, w_o,
           g_ffn, w_pq, peer_keys, peer_u, peer_v, g_ple, w_ple_gate, w_ple):
    b, t, db, ts, n_pages = dims
    n, d = x.shape
    n_p = b * t
    past = n_pages * PAGE_SIZE
    pp = _tile(n_pages, PAGES_PER_STEP, 1)
    pp_attn = _tile(n_pages, ATTN_PAGES_PER_STEP, 1)

    h = _rmsnorm(x, g_mix, BF16)
    proj = _matmul(h, _align_w_in(w_in), out_dtype=F32, name="in_proj", **FULL_K)
    pos = jnp.concatenate([jnp.tile(jnp.arange(t), b), jnp.tile(past + jnp.arange(ts), db)])
    cos, sin = _rope_tables(pos)
    bf_row = jnp.pad(b_f, (0, LANES - FOX_HEADS)).reshape(1, LANES)
    qf, fkv, qn, qr, nkv, win, logf, gates = _post_projection(proj, cos, sin, bf_row, 0, n_p)
    sample_rows = _post_projection(proj, cos, sin, bf_row, n_p, n - n_p)
    qf_s, fkv_s, qn_s, qr_s, nkv_s, win_s, logf_s, gates_s = [a.reshape(db, ts, -1) for a in sample_rows]

    lf_p_t = jnp.transpose(logf.reshape(b, t, FOX_HEADS), (0, 2, 1))
    o_f_p = _fox_prompt(qf, fkv, _cumsum_prompt(lf_p_t), b, t)
    w_cmp = jnp.concatenate([jnp.tile(w_ck, (1, NSA_KV_HEADS)), jnp.tile(w_cv, (1, NSA_KV_HEADS))], axis=1)
    cmp_p = _compress_rows(nkv, w_cmp, n_p)
    o_n_p = _nsa_prompt(qn, qr, cmp_p, nkv, win, gates, b, t)

    n_pool = cache_fox_logf.shape[0]
    lc_pool = _page_cumsum(jnp.transpose(cache_fox_logf, (0, 2, 1)).reshape(n_pool * FOX_HEADS, PAGE_SIZE))
    lf_new_t = jnp.pad(jnp.transpose(logf_s, (0, 2, 1)), ((0, 0), (0, 0), (0, PAGE_SIZE - ts)))
    lc_new = _page_cumsum(lf_new_t.reshape(db * FOX_HEADS, PAGE_SIZE))
    slab_rows = lambda a: a.reshape(a.shape[0], -1, HEAD_DIM)
    fkv_new = slab_rows(_pad_axis1(fkv_s, PAGE_SIZE))
    o_f_s = _fox_decode(pt_flat, qf_s.astype(F32), slab_rows(cache_fox_kv),
                        lc_pool.reshape(n_pool, FOX_HEADS, PAGE_SIZE), fkv_new,
                        lc_new.reshape(db, FOX_HEADS, PAGE_SIZE), db, ts, n_pages, pp_attn)

    cache_nsa = slab_rows(cache_nsa_kv)
    nkv_new = _pad_axis1(nkv_s, PAGE_SIZE)
    cmp_new = _compress_rows(nkv_new.reshape(db * PAGE_SIZE, 4 * NSA_KV_W), w_cmp, db * PAGE_SIZE)
    cmp_s = jnp.concatenate([_compress_paged(pt_flat, cache_nsa, w_cmp, db, n_pages, pp),
                             cmp_new.reshape(db, PAGE_SIZE // NSA_BLOCK, 2 * NSA_KV_W)], axis=1)
    cmp_s = _pad_axis1(cmp_s, -(-cmp_s.shape[1] // LANES) * LANES)
    wb = state_win.shape[1]
    o_n_s, new_win = _nsa_decode(
        pt_flat, qn_s.astype(F32), qr_s.astype(F32), cmp_s, cache_nsa, slab_rows(nkv_new),
        slab_rows(state_win), slab_rows(_pad_axis1(win_s, PAGE_SIZE)),
        gates_s, db, ts, n_pages, pp_attn)

    o_f = jnp.concatenate([o_f_p, o_f_s.reshape(db * ts, FOX_W).astype(BF16)], axis=0)
    o_n = jnp.concatenate([o_n_p, o_n_s.reshape(db * ts, NSA_W).astype(BF16)], axis=0)

    gate = _matmul(h, w_merge, out_dtype=F32, name="merge_gate", **FULL_K,
                   epilogue=lambda acc, bias: jax.nn.sigmoid(acc + bias), extra=[(b_merge.reshape(1, -1), "row", 0)])
    m_a = _matmul(o_f, w_a, tm=1664, tn=512, tk=2048, out_dtype=F32, name="merge_fox",
                  epilogue=lambda acc, ga: ga * acc, extra=[(gate, "tile", 0)])
    tn_b = _tile(d, 512, LANES)
    mixed = _matmul(o_n, w_b, tm=1664, tn=512, tk=2048, out_dtype=BF16, name="merge_nsa",
                    epilogue=lambda acc, gb, ma: ma + gb * acc,
                    extra=[(gate, "tile", d // tn_b), (m_a, "tile", 0)])
    x1 = _matmul(mixed, w_o, out_dtype=F32, name="out_proj", **FULL_K,
                 epilogue=lambda acc, res: res + acc, extra=[(x, "tile", 0)])

    h2 = _rmsnorm(x1, g_ffn, BF16)
    pq = _matmul(h2, w_pq, out_dtype=BF16, name="peer_query", **FULL_K)
    a_sel, b_sel, g_sel = _peer_select(pq, peer_keys)
    act = _matmul(h2, peer_u, out_dtype=F32, nt=True, split_out=True, epilogue=_gelu_tanh, name="peer_act",
                  **FULL_K)
    coef = _peer_coef(a_sel, b_sel, g_sel, act)
    x2 = _matmul(coef, peer_v, tm=1664, tn=512, tk=2048, out_dtype=F32, name="peer_out",
                 epilogue=lambda acc, res: res + acc, extra=[(x1, "tile", 0)])

    h3 = _rmsnorm(x2, g_ple, BF16)
    x3 = _matmul(h3, w_ple_gate, out_dtype=F32, name="ple_gate", **FULL_K,
                 epilogue=lambda acc, res, p_rows, w_cols: (
                     res + jax.nn.sigmoid(acc) * _dot(p_rows, w_cols.astype(BF16))),
                 extra=[(x2, "tile", 0), (p_ple.astype(BF16), "lhs", 0), (w_ple, "rhs", 0)])

    caches = dict(
        fox_kv_p=fkv.reshape(b, t, 2, FOX_KV_HEADS, HEAD_DIM),
        fox_lf_p=logf.reshape(b, t, FOX_HEADS),
        nsa_kv_p=nkv.reshape(b, t, 4, NSA_KV_HEADS, HEAD_DIM),
        nsa_win_p=win.reshape(b, t, 2, NSA_KV_HEADS, HEAD_DIM)[:, t - min(NSA_WINDOW, t):],
        fox_kv_s=fkv_s.reshape(db, ts, 2, FOX_KV_HEADS, HEAD_DIM),
        fox_lf_s=logf_s,
        nsa_kv_s=nkv_s.reshape(db, ts, 4, NSA_KV_HEADS, HEAD_DIM),
        nsa_win_s=new_win.reshape(db, wb, 2, NSA_KV_HEADS, HEAD_DIM),
    )
    return x3, caches


def kernel(x_prompt, x_sample, cache_fox_kv, cache_fox_logf, cache_nsa_kv, state_nsa_win, page_table, p_prompt, p_sample, g_mix, w_in, b_f, w_ck, w_cv, w_a, w_b, w_merge, b_merge, w_o, g_ffn, w_pq, peer_keys, peer_u, peer_v, g_ple, w_ple_gate, w_ple, g_final):
    b, t, d = x_prompt.shape
    db, ts, _ = x_sample.shape
    depth = g_mix.shape[0]
    n_pages = page_table.shape[1]
    n_p = b * t
    dims = (b, t, db, ts, n_pages)
    pt_flat = page_table.reshape(-1).astype(jnp.int32)
    x = jnp.concatenate([x_prompt.reshape(n_p, d), x_sample.reshape(db * ts, d)], axis=0)
    per_layer = []
    for i in range(depth):
        p_ple = jnp.concatenate([p_prompt[i].reshape(n_p, -1), p_sample[i].reshape(db * ts, -1)], axis=0)
        x, caches = _layer(x, p_ple, cache_fox_kv[i], cache_fox_logf[i], cache_nsa_kv[i], state_nsa_win[i],
                           pt_flat, dims, g_mix[i], w_in[i], b_f[i], w_ck[i], w_cv[i], w_a[i], w_b[i],
                           w_merge[i], b_merge[i], w_o[i], g_ffn[i], w_pq[i], peer_keys[i], peer_u[i],
                           peer_v[i], g_ple[i], w_ple_gate[i], w_ple[i])
        per_layer.append(caches)
    y_p = _rmsnorm(x, g_final, F32, 0, n_p)
    y_s = _rmsnorm(x, g_final, F32, n_p, db * ts)
    stack = lambda name: jnp.stack([c[name] for c in per_layer])
    return (y_p.reshape(b, t, d), y_s.reshape(db, ts, d),
            stack("fox_kv_p"), stack("fox_lf_p"), stack("nsa_kv_p"), stack("nsa_win_p"),
            stack("fox_kv_s"), stack("fox_lf_s"), stack("nsa_kv_s"), stack("nsa_win_s"))
```
